```python
import math
import jax, jax.numpy as jnp
from jax import lax
import numpy as np

D_MODEL = 2048
BATCH = 8
SEQ = 2048
DEPTH = 1
DEC_BATCH = 8
DEC_SEQ = 16
PAST_LEN = 2048

CHUNK = 64
D_MIX = D_MODEL
D_POOL = D_MIX // 2
POOL_WINDOWS = (2, 4, 8, 16)
N_POOL_GROUPS = len(POOL_WINDOWS)
POOL_CH = D_POOL // N_POOL_GROUPS
POOL_HIST = max(POOL_WINDOWS) - 1
D_SSM = D_MIX - D_POOL
SSM_CH = 16
N_SSM_GROUPS = D_SSM // SSM_CH
SSM_STATE = 64
N_EXPERTS = 32
TOP_K = 4
D_FF = D_MODEL
SWIGLU_LIMIT = 7.0
SWIGLU_ALPHA = 1.702
MOE_BLOCK = 256
DN_ALPHA = (2 * DEPTH) ** 0.25
DN_BETA = (8 * DEPTH) ** -0.25
LN_EPS = 1e-5
F32 = jnp.float32

kernel_name = 'hybrid_pool_s5_moe_stream_step'


def _layer_norm(x, g, b):
    xf = x.astype(F32)
    mu = jnp.mean(xf, axis=-1, keepdims=True)
    var = jnp.mean(jnp.square(xf - mu), axis=-1, keepdims=True)
    return ((xf - mu) * lax.rsqrt(var + LN_EPS) * g.astype(F32) + b.astype(F32)).astype(x.dtype)


def _pool_mixer(u, u_hist, pos0, w_pool, pool_scale):
    bsz, seq, _ = u.shape
    ext = jnp.concatenate([u_hist.astype(F32), u.astype(F32)], axis=1)
    cs = jnp.concatenate([jnp.zeros((bsz, 1, D_POOL), F32), jnp.cumsum(ext, axis=1)], axis=1)
    hi = cs[:, POOL_HIST + 1:POOL_HIST + 1 + seq]
    pos = (pos0 + jnp.arange(seq)).astype(F32)[None, :, None]
    pooled = []
    for g, w in enumerate(POOL_WINDOWS):
        sl = slice(g * POOL_CH, (g + 1) * POOL_CH)
        lo = cs[:, POOL_HIST + 1 - w:POOL_HIST + 1 - w + seq, sl]
        cnt = jnp.minimum(pos + 1.0, float(w))
        pooled.append((hi[..., sl] - lo) / cnt)
    diff = jnp.concatenate(pooled, axis=-1) - ext[:, POOL_HIST:]
    diff = diff.reshape(bsz, seq, N_POOL_GROUPS, POOL_CH)
    y = jnp.einsum('blgc,gcd->blgd', diff, w_pool.astype(F32)).reshape(bsz, seq, D_POOL)
    y = y * pool_scale.astype(F32)
    return y.astype(u.dtype), ext[:, -POOL_HIST:].astype(u.dtype)


def _ssm_mixer(u, h0_re, h0_im, lambda_re, lambda_im, log_dt, b_re, b_im, c_re, c_im, d_skip, w_glu, b_glu):
    bsz, seq, _ = u.shape
    uf = u.astype(F32).reshape(bsz, seq, N_SSM_GROUPS, SSM_CH)
    lr, li = lambda_re.astype(F32), lambda_im.astype(F32)
    dt = jnp.exp(log_dt.astype(F32))[:, None]
    mag = jnp.exp(lr * dt)
    abar_re, abar_im = mag * jnp.cos(li * dt), mag * jnp.sin(li * dt)
    nr, ni = abar_re - 1.0, abar_im
    den = lr * lr + li * li
    k_re = (nr * lr + ni * li) / den
    k_im = (ni * lr - nr * li) / den
    br, bi = b_re.astype(F32), b_im.astype(F32)
    bb_re = k_re[..., None] * br - k_im[..., None] * bi
    bb_im = k_re[..., None] * bi + k_im[..., None] * br
    bu_re = jnp.einsum('blgc,gpc->blgp', uf, bb_re)
    bu_im = jnp.einsum('blgc,gpc->blgp', uf, bb_im)
    a_re = jnp.broadcast_to(abar_re[None, None], (1, seq, N_SSM_GROUPS, SSM_STATE))
    a_im = jnp.broadcast_to(abar_im[None, None], (1, seq, N_SSM_GROUPS, SSM_STATE))

    def combine(e1, e2):
        a1r, a1i, b1r, b1i = e1
        a2r, a2i, b2r, b2i = e2
        return (a2r * a1r - a2i * a1i, a2r * a1i + a2i * a1r,
                a2r * b1r - a2i * b1i + b2r, a2r * b1i + a2i * b1r + b2i)

    acc_re, acc_im, h_re, h_im = lax.associative_scan(combine, (a_re, a_im, bu_re, bu_im), axis=1)
    if h0_re is not None:
        s_re, s_im = h0_re.astype(F32)[:, None], h0_im.astype(F32)[:, None]
        h_re = h_re + acc_re * s_re - acc_im * s_im
        h_im = h_im + acc_re * s_im + acc_im * s_re
    y = (jnp.einsum('blgp,gcp->blgc', h_re, c_re.astype(F32))
         - jnp.einsum('blgp,gcp->blgc', h_im, c_im.astype(F32))
         + d_skip.astype(F32) * uf)
    y = jax.nn.gelu(y)
    y = y * jax.nn.sigmoid(jnp.einsum('blgc,gce->blge', y, w_glu.astype(F32)) + b_glu.astype(F32))
    return (y.reshape(bsz, seq, D_SSM).astype(u.dtype),
            h_re[:, -1].astype(u.dtype), h_im[:, -1].astype(u.dtype))


def _moe_block_rows(n_assign):
    target = -(-n_assign // N_EXPERTS)
    rows = 8
    while rows < min(target, MOE_BLOCK):
        rows *= 2
    return rows


def _moe(h, w_router, b_router, w_gate_up, b_gate_up, w_down, b_down):
    bsz, seq, d = h.shape
    n_tok = bsz * seq
    hf = h.reshape(n_tok, d)
    logits = (hf @ w_router).astype(F32) + b_router.astype(F32)
    top_val, top_idx = lax.top_k(logits, TOP_K)
    gates = jax.nn.softmax(top_val, axis=-1)
    n_assign = n_tok * TOP_K
    blk = _moe_block_rows(n_assign)
    n_blk = -(-n_assign // blk) + N_EXPERTS
    flat_e = top_idx.reshape(n_assign)
    flat_tok = jnp.arange(n_assign, dtype=jnp.int32) // TOP_K
    order = jnp.argsort(flat_e)
    sorted_e = flat_e[order]
    counts = jnp.bincount(flat_e, length=N_EXPERTS)
    padded = (counts + blk - 1) // blk * blk
    pad_end = jnp.cumsum(padded)
    pad_start = pad_end - padded
    start = jnp.cumsum(counts) - counts
    dest = pad_start[sorted_e] + (jnp.arange(n_assign) - start[sorted_e])
    n_rows = n_blk * blk
    row_tok = jnp.full((n_rows,), n_tok, jnp.int32).at[dest].set(flat_tok[order])
    row_gate = jnp.zeros((n_rows,), F32).at[dest].set(gates.reshape(n_assign)[order])
    blk_start = jnp.arange(n_blk) * blk
    blk_expert = jnp.minimum(jnp.sum(blk_start[:, None] >= pad_end[None, :], axis=1), N_EXPERTS - 1)
    xs = jnp.concatenate([hf, jnp.zeros((1, d), hf.dtype)], axis=0)[row_tok].reshape(n_blk, blk, d)

    def expert_block(args):
        xb, e = args
        gu = xb @ w_gate_up[e] + b_gate_up[e]
        gate = jnp.minimum(gu[:, :D_FF], SWIGLU_LIMIT)
        up = jnp.clip(gu[:, D_FF:], -SWIGLU_LIMIT, SWIGLU_LIMIT)
        act = (up + 1.0) * (gate * jax.nn.sigmoid(SWIGLU_ALPHA * gate))
        return act @ w_down[e] + b_down[e]

    ys = lax.map(expert_block, (xs, blk_expert)).reshape(n_rows, d)
    out = jnp.zeros((n_tok + 1, d), F32).at[row_tok].add(ys.astype(F32) * row_gate[:, None])[:n_tok]
    return out.reshape(bsz, seq, d).astype(h.dtype)


def _encoder_layer(x, c, pool_hist, h0_re, h0_im, pos0, lw):
    mod = (jax.nn.silu(c) @ lw['w_ada'] + lw['b_ada'])[:, None, :]
    sh1, sc1, g1, sh2, sc2, g2 = jnp.split(mod, 6, axis=-1)
    h = x * (1.0 + sc1) + sh1
    u = h @ lw['w_in']
    y_pool, new_hist = _pool_mixer(u[..., :D_POOL], pool_hist, pos0, lw['w_pool'], lw['pool_scale'])
    y_ssm, s_re, s_im = _ssm_mixer(u[..., D_POOL:], h0_re, h0_im, lw['lambda_re'], lw['lambda_im'],
                                   lw['log_dt'], lw['ssm_b_re'], lw['ssm_b_im'], lw['ssm_c_re'],
                                   lw['ssm_c_im'], lw['d_skip'], lw['w_glu'], lw['b_glu'])
    mix = jnp.concatenate([y_pool, y_ssm], axis=-1) @ lw['w_out']
    x = _layer_norm(DN_ALPHA * x + g1 * mix, lw['ln1_g'], lw['ln1_b'])
    h = x * (1.0 + sc2) + sh2
    ffn = _moe(h, lw['w_router'], lw['b_router'], lw['w_gate_up'], lw['b_gate_up'], lw['w_down'], lw['b_down'])
    x = _layer_norm(DN_ALPHA * x + g2 * ffn, lw['ln2_g'], lw['ln2_b'])
    return x, new_hist, s_re, s_im


def setup_inputs(seed: int = 0) -> dict:
    key = jax.random.key(seed)
    ks = iter(jax.random.split(key, 40))

    def nrm(shape, scale):
        return jax.random.normal(next(ks), shape, F32) * scale

    G, P, C = N_SSM_GROUPS, SSM_STATE, SSM_CH
    return {
        'x_prompt': nrm((BATCH, SEQ, D_MODEL), 1.0),
        'x_sample': nrm((DEC_BATCH, DEC_SEQ, D_MODEL), 1.0),
        'cache_pool': nrm((DEPTH, DEC_BATCH, POOL_HIST, D_POOL), 1.0),
        'state_ssm_re': nrm((DEPTH, DEC_BATCH, G, P), 0.1),
        'state_ssm_im': nrm((DEPTH, DEC_BATCH, G, P), 0.1),
        'c_prompt': nrm((BATCH, D_MODEL), 1.0),
        'c_sample': nrm((DEC_BATCH, D_MODEL), 1.0),
        'w_ada': nrm((DEPTH, D_MODEL, 6 * D_MODEL), 0.5 * D_MODEL ** -0.5),
        'b_ada': nrm((DEPTH, 6 * D_MODEL), 0.01),
        'w_in': nrm((DEPTH, D_MODEL, D_MIX), D_MODEL ** -0.5),
        'w_pool': nrm((DEPTH, N_POOL_GROUPS, POOL_CH, POOL_CH), POOL_CH ** -0.5),
        'pool_scale': 1.0 + nrm((DEPTH, D_POOL), 0.1),
        'lambda_re': -0.5 + nrm((DEPTH, G, P), 0.01),
        'lambda_im': math.pi * jnp.arange(P, dtype=F32)[None, None, :] + nrm((DEPTH, G, P), 0.01),
        'log_dt': jax.random.uniform(next(ks), (DEPTH, G), F32, math.log(1e-3), math.log(1e-1)),
        'ssm_b_re': nrm((DEPTH, G, P, C), (2 * C) ** -0.5),
        'ssm_b_im': nrm((DEPTH, G, P, C), (2 * C) ** -0.5),
        'ssm_c_re': nrm((DEPTH, G, C, P), (2 * P) ** -0.5),
        'ssm_c_im': nrm((DEPTH, G, C, P), (2 * P) ** -0.5),
        'd_skip': nrm((DEPTH, G, C), 1.0),
        'w_glu': nrm((DEPTH, G, C, C), C ** -0.5),
        'b_glu': nrm((DEPTH, G, C), 0.01),
        'w_out': nrm((DEPTH, D_MIX, D_MODEL), DN_BETA * D_MIX ** -0.5),
        'ln1_g': 1.0 + nrm((DEPTH, D_MODEL), 0.01),
        'ln1_b': nrm((DEPTH, D_MODEL), 0.01),
        'w_router': nrm((DEPTH, D_MODEL, N_EXPERTS), D_MODEL ** -0.5),
        'b_router': nrm((DEPTH, N_EXPERTS), 0.01),
        'w_gate_up': nrm((DEPTH, N_EXPERTS, D_MODEL, 2 * D_FF), D_MODEL ** -0.5),
        'b_gate_up': nrm((DEPTH, N_EXPERTS, 2 * D_FF), 0.01),
        'w_down': nrm((DEPTH, N_EXPERTS, D_FF, D_MODEL), DN_BETA * D_FF ** -0.5),
        'b_down': nrm((DEPTH, N_EXPERTS, D_MODEL), 0.01),
        'ln2_g': 1.0 + nrm((DEPTH, D_MODEL), 0.01),
        'ln2_b': nrm((DEPTH, D_MODEL), 0.01),
    }


def reference(x_prompt, x_sample, cache_pool, state_ssm_re, state_ssm_im, c_prompt, c_sample,
              w_ada, b_ada, w_in, w_pool, pool_scale, lambda_re, lambda_im, log_dt,
              ssm_b_re, ssm_b_im, ssm_c_re, ssm_c_im, d_skip, w_glu, b_glu, w_out, ln1_g, ln1_b,
              w_router, b_router, w_gate_up, b_gate_up, w_down, b_down, ln2_g, ln2_b):
    xp, xs = x_prompt, x_sample
    pool_p, re_p, im_p, pool_s, re_s, im_s = [], [], [], [], [], []
    for l in range(DEPTH):
        lw = {
            'w_ada': w_ada[l], 'b_ada': b_ada[l], 'w_in': w_in[l], 'w_pool': w_pool[l],
            'pool_scale': pool_scale[l], 'lambda_re': lambda_re[l], 'lambda_im': lambda_im[l],
            'log_dt': log_dt[l], 'ssm_b_re': ssm_b_re[l], 'ssm_b_im': ssm_b_im[l],
            'ssm_c_re': ssm_c_re[l], 'ssm_c_im': ssm_c_im[l], 'd_skip': d_skip[l],
            'w_glu': w_glu[l], 'b_glu': b_glu[l], 'w_out': w_out[l], 'ln1_g': ln1_g[l], 'ln1_b': ln1_b[l],
            'w_router': w_router[l], 'b_router': b_router[l], 'w_gate_up': w_gate_up[l],
            'b_gate_up': b_gate_up[l], 'w_down': w_down[l], 'b_down': b_down[l],
            'ln2_g': ln2_g[l], 'ln2_b': ln2_b[l],
        }
        empty_hist = jnp.zeros((xp.shape[0], POOL_HIST, D_POOL), xp.dtype)
        xp, hp, rp, ip = _encoder_layer(xp, c_prompt, empty_hist, None, None, 0, lw)
        xs, hs, rs, is_ = _encoder_layer(xs, c_sample, cache_pool[l], state_ssm_re[l], state_ssm_im[l],
                                         PAST_LEN, lw)
        pool_p.append(hp); re_p.append(rp); im_p.append(ip)
        pool_s.append(hs); re_s.append(rs); im_s.append(is_)
    return (xp, xs, jnp.stack(pool_p), jnp.stack(re_p), jnp.stack(im_p),
            jnp.stack(pool_s), jnp.stack(re_s), jnp.stack(im_s))
```

```python
import functools
import math

import jax
import jax.numpy as jnp
from jax import lax
from jax.experimental import pallas as pl
from jax.experimental.pallas import tpu as pltpu

F32 = jnp.float32
BF16 = jnp.bfloat16
I32 = jnp.int32
U32 = jnp.uint32

POOL_WINDOWS = (2, 4, 8, 16)
HIST_ROWS = 16
SSM_CH = 16
SSM_STATE = 64
TOP_K = 4
SWIGLU_LIMIT = 7.0
SWIGLU_ALPHA = 1.702
LN_EPS = 1e-5
DEPTH = 1
DN_ALPHA = (2 * DEPTH) ** 0.25
PAST_LEN = 2048

LANES = 128
SUBLANES = 8
MXU_DIM = 256
V7X_VMEM_BYTES = 64 * 1024 * 1024
MIB = 1024 * 1024

ROW_BLK = 256
MOE_RB = 512
MOE_CW = 256
MOE_MAX_BLK = 10


def _params(sem, vmem_mib):
    return pltpu.CompilerParams(dimension_semantics=sem, vmem_limit_bytes=min(vmem_mib * MIB, V7X_VMEM_BYTES - 4 * MIB))


def _layer_norm(v, g, b):
    mu = jnp.mean(v, axis=-1, keepdims=True)
    vc = v - mu
    var = jnp.mean(vc * vc, axis=-1, keepdims=True)
    return vc * lax.rsqrt(var + LN_EPS) * g + b


def _ada_kernel(c_ref, w_ref, b_ref, o_ref):
    c = c_ref[...]
    s = c * jax.nn.sigmoid(c)
    o_ref[...] = jnp.dot(s.astype(BF16), w_ref[...].astype(BF16), preferred_element_type=F32) + b_ref[...]


def _ada(c_all, w_ada, b_ada):
    r, d = c_all.shape
    n = w_ada.shape[1]
    tn = 1536
    return pl.pallas_call(
        _ada_kernel,
        grid=(n // tn,),
        in_specs=[pl.BlockSpec((r, d), lambda j: (0, 0)),
                  pl.BlockSpec((d, tn), lambda j: (0, j)),
                  pl.BlockSpec((1, tn), lambda j: (0, j))],
        out_specs=pl.BlockSpec((r, tn), lambda j: (0, j)),
        out_shape=jax.ShapeDtypeStruct((r, n), F32),
        compiler_params=_params(("arbitrary",), 40),
        name="ada_mod",
    )(c_all, w_ada, b_ada)


def _inproj_kernel(x_ref, sc_ref, sh_ref, w_ref, up_ref, us_ref):
    h = x_ref[...] * (1.0 + sc_ref[...]) + sh_ref[...]
    u = jnp.dot(h.astype(BF16), w_ref[...], preferred_element_type=F32)
    dp = up_ref.shape[-1]
    up_ref[...] = u[:, :dp]
    us_ref[...] = u[:, dp:]


def _inproj(x, sc, sh, w_bf):
    b, l, d = x.shape
    dm = w_bf.shape[1]
    dp = dm // 2
    tm = min(l, 512)
    row = pl.BlockSpec((None, tm, d), lambda bi, i: (bi, i, 0))
    mod = pl.BlockSpec((None, 1, d), lambda bi, i: (bi, 0, 0))
    half = pl.BlockSpec((None, tm, dp), lambda bi, i: (bi, i, 0))
    return pl.pallas_call(
        _inproj_kernel,
        grid=(b, l // tm),
        in_specs=[row, mod, mod, pl.BlockSpec((d, dm), lambda bi, i: (0, 0))],
        out_specs=[half, half],
        out_shape=[jax.ShapeDtypeStruct((b, l, dp), F32), jax.ShapeDtypeStruct((b, l, dm - dp), F32)],
        compiler_params=_params(("arbitrary", "arbitrary"), 48),
        name="in_proj",
    )(x, sc, sh, w_bf)


def _pool_kernel(u_ref, hist_ref, w_ref, scale_ref, y_ref, nh_ref, hbuf, *, tt, pos0):
    i = pl.program_id(1)

    @pl.when(i == 0)
    def _():
        hbuf[...] = hist_ref[...]

    u = u_ref[...]
    hb = hbuf[...]
    u16 = u.astype(BF16)
    h16 = hb.astype(BF16)
    pc = u.shape[1] // len(POOL_WINDOWS)
    d_main = lax.broadcasted_iota(I32, (tt, tt), 0) - lax.broadcasted_iota(I32, (tt, tt), 1)
    d_hist = (lax.broadcasted_iota(I32, (tt, HIST_ROWS), 0) + HIST_ROWS
              - lax.broadcasted_iota(I32, (tt, HIST_ROWS), 1))
    pos = (pos0 + i * tt + lax.broadcasted_iota(I32, (tt, 1), 0)).astype(F32)
    for g, w in enumerate(POOL_WINDOWS):
        sl = slice(g * pc, (g + 1) * pc)
        bm = jnp.where(d_main >= 0, jnp.where(d_main < w, 1.0, 0.0), 0.0).astype(BF16)
        bh = jnp.where(d_hist < w, 1.0, 0.0).astype(BF16)
        s = (jnp.dot(bm, u16[:, sl], preferred_element_type=F32)
             + jnp.dot(bh, h16[:, sl], preferred_element_type=F32))
        cnt = jnp.minimum(pos + 1.0, float(w))
        diff = s / cnt - u[:, sl]
        y = jnp.dot(diff.astype(BF16), w_ref[g], preferred_element_type=F32) * scale_ref[:, sl]
        y_ref[:, sl] = y.astype(y_ref.dtype)
    hbuf[...] = u[tt - HIST_ROWS:, :]

    @pl.when(i == pl.num_programs(1) - 1)
    def _():
        nh_ref[...] = hbuf[...]


def _pool(u_pool, hist16, w_pool_bf, pool_scale, pos0):
    b, l, dp = u_pool.shape
    tt = min(l, 256)
    g, pc, _ = w_pool_bf.shape
    kern = functools.partial(_pool_kernel, tt=tt, pos0=pos0)
    return pl.pallas_call(
        kern,
        grid=(b, l // tt),
        in_specs=[pl.BlockSpec((None, tt, dp), lambda bi, i: (bi, i, 0)),
                  pl.BlockSpec((None, HIST_ROWS, dp), lambda bi, i: (bi, 0, 0)),
                  pl.BlockSpec((g, pc, pc), lambda bi, i: (0, 0, 0)),
                  pl.BlockSpec((1, dp), lambda bi, i: (0, 0))],
        out_specs=[pl.BlockSpec((None, tt, dp), lambda bi, i: (bi, i, 0)),
                   pl.BlockSpec((None, HIST_ROWS, dp), lambda bi, i: (bi, 0, 0))],
        out_shape=[jax.ShapeDtypeStruct((b, l, dp), BF16), jax.ShapeDtypeStruct((b, HIST_ROWS, dp), F32)],
        scratch_shapes=[pltpu.VMEM((HIST_ROWS, dp), F32)],
        compiler_params=_params(("arbitrary", "arbitrary"), 32),
        name="pool_mixer",
    )(u_pool, hist16, w_pool_bf, pool_scale)


SSM_PAIR_GROUP = 8


def _ssm_kernel(u_ref, bc_ref, cc_ref, wg_ref, ar_ref, ai_ref, dsk_ref, bgl_ref, h0_ref,
                y_ref, hn_ref, s_ref, ubuf, hst, *, tt, pitch, nb):
    i = pl.program_id(0)
    n_slab = s_ref.shape[0]
    n_chunk = bc_ref.shape[0]
    slab_per_chunk = n_slab // n_chunk
    cw = bc_ref.shape[1]

    @pl.when(i == 0)
    def _():
        s_ref[...] = jnp.zeros_like(s_ref)
        ubuf[...] = jnp.zeros_like(ubuf)
        hst[...] = h0_ref[...]

    u = u_ref[...].reshape(nb * tt, u_ref.shape[-1])
    u16 = u.astype(BF16)
    for k in range(n_chunk):
        bu = jnp.dot(u16[:, k * cw:(k + 1) * cw], bc_ref[k], preferred_element_type=F32)
        for j in range(slab_per_chunk):
            for b in range(nb):
                s_ref[k * slab_per_chunk + j, b * pitch:b * pitch + tt, :] = (
                    bu[b * tt:(b + 1) * tt, j * LANES:(j + 1) * LANES])
    for b in range(nb):
        ubuf[b * pitch:b * pitch + tt, :] = u[b * tt:(b + 1) * tt, :]

    n_pair = n_slab // 2
    for pg in range(n_pair // SSM_PAIR_GROUP):
        qs = [pg * SSM_PAIR_GROUP + j for j in range(SSM_PAIR_GROUP)]
        ars = [ar_ref[q] for q in qs]
        ais = [ai_ref[q] for q in qs]

        def body(t, carry, qs=qs, ars=ars, ais=ais):
            hr, hi = carry
            nhr, nhi = [], []
            rows = pl.ds(t, nb, stride=pitch)
            for j, q in enumerate(qs):
                bur = s_ref[2 * q, rows, :]
                bui = s_ref[2 * q + 1, rows, :]
                r_ = ars[j] * hr[j] - ais[j] * hi[j] + bur
                i_ = ars[j] * hi[j] + ais[j] * hr[j] + bui
                s_ref[2 * q, rows, :] = r_
                s_ref[2 * q + 1, rows, :] = i_
                nhr.append(r_)
                nhi.append(i_)
            return tuple(nhr), tuple(nhi)

        init = (tuple(hst[2 * q] for q in qs), tuple(hst[2 * q + 1] for q in qs))
        hr, hi = lax.fori_loop(0, tt, body, init)
        for j, q in enumerate(qs):
            hst[2 * q] = hr[j]
            hst[2 * q + 1] = hi[j]

    ycs = []
    for n in range(n_chunk):
        hc = jnp.concatenate([s_ref[n * slab_per_chunk + j] for j in range(slab_per_chunk)], axis=1)
        ycs.append(jnp.dot(hc.astype(BF16), cc_ref[n], preferred_element_type=F32))
    y = jnp.concatenate(ycs, axis=1) + dsk_ref[...] * ubuf[...]
    y = jax.nn.gelu(y)
    y16 = y.astype(BF16)
    z = jnp.concatenate([jnp.dot(y16[:, k * cw:(k + 1) * cw], wg_ref[k], preferred_element_type=F32)
                         for k in range(n_chunk)], axis=1) + bgl_ref[...]
    out = y * jax.nn.sigmoid(z)
    for b in range(nb):
        y_ref[b] = out[b * pitch:b * pitch + tt].astype(y_ref.dtype)

    @pl.when(i == pl.num_programs(0) - 1)
    def _():
        hn_ref[...] = hst[...]


def _ssm(u_ssm, prm, h0):
    b, l, ds = u_ssm.shape
    tt = min(l, 64)
    pitch = tt + 4
    n_slab = h0.shape[0]
    n_chunk = prm["bc"].shape[0]
    kern = functools.partial(_ssm_kernel, tt=tt, pitch=pitch, nb=b)

    def const(a):
        nd = a.ndim
        return pl.BlockSpec(a.shape, lambda i, nd=nd: (0,) * nd)

    args = (u_ssm, prm["bc"], prm["cc"], prm["wg"], prm["ar"], prm["ai"], prm["dsk"], prm["bgl"], h0)
    return pl.pallas_call(
        kern,
        grid=(l // tt,),
        in_specs=[pl.BlockSpec((b, tt, ds), lambda i: (0, i, 0))] + [const(a) for a in args[1:]],
        out_specs=[pl.BlockSpec((b, tt, ds), lambda i: (0, i, 0)),
                   pl.BlockSpec((n_slab, b, LANES), lambda i: (0, 0, 0))],
        out_shape=[jax.ShapeDtypeStruct((b, l, ds), BF16), jax.ShapeDtypeStruct((n_slab, b, LANES), F32)],
        scratch_shapes=[pltpu.VMEM((n_slab, b * pitch, LANES), F32),
                        pltpu.VMEM((b * pitch, ds), F32),
                        pltpu.VMEM((n_slab, b, LANES), F32)],
        compiler_params=_params(("arbitrary",), 56),
        name="ssm_mixer",
    )(*args)


def _ssm_params(lambda_re, lambda_im, log_dt, b_re, b_im, c_re, c_im, d_skip, w_glu, b_glu, nb):
    g, p = lambda_re.shape
    ch = b_re.shape[-1]
    gpc = MXU_DIM // ch
    n_chunk = g // gpc
    ppc = gpc // 2
    lr, li = lambda_re.astype(F32), lambda_im.astype(F32)
    dt = jnp.exp(log_dt.astype(F32))[:, None]
    mag = jnp.exp(lr * dt)
    abar_re, abar_im = mag * jnp.cos(li * dt), mag * jnp.sin(li * dt)
    nr, ni = abar_re - 1.0, abar_im
    den = lr * lr + li * li
    k_re = (nr * lr + ni * li) / den
    k_im = (ni * lr - nr * li) / den
    br, bi = b_re.astype(F32), b_im.astype(F32)
    bb_re = k_re[..., None] * br - k_im[..., None] * bi
    bb_im = k_re[..., None] * bi + k_im[..., None] * br
    eye_p = jnp.eye(ppc, dtype=F32)
    eye_2 = jnp.eye(2, dtype=F32)
    bb = jnp.stack([bb_re, bb_im]).reshape(2, n_chunk, ppc, 2, p, ch)
    bc = jnp.einsum("rkqgpc,qx,gy->kqgcxryp", bb, eye_p, eye_2).reshape(n_chunk, gpc * ch, ppc * 2 * 2 * p)
    cm = jnp.stack([c_re.astype(F32), -c_im.astype(F32)]).reshape(2, n_chunk, ppc, 2, ch, p)
    cc = jnp.einsum("rkqgcp,qx,gy->kqrgpxyc", cm, eye_p, eye_2).reshape(n_chunk, ppc * 2 * 2 * p, gpc * ch)
    wg = jnp.einsum("kgce,gx->kgcxe", w_glu.astype(F32).reshape(n_chunk, gpc, ch, ch),
                    jnp.eye(gpc, dtype=F32)).reshape(n_chunk, gpc * ch, gpc * ch)
    n_pair = g // 2
    ar = jnp.broadcast_to(abar_re.reshape(n_pair, 1, 2 * p), (n_pair, nb, 2 * p))
    ai = jnp.broadcast_to(abar_im.reshape(n_pair, 1, 2 * p), (n_pair, nb, 2 * p))
    return dict(bc=bc.astype(BF16), cc=cc.astype(BF16), wg=wg.astype(BF16), ar=ar, ai=ai,
                dsk=d_skip.astype(F32).reshape(1, g * ch), bgl=b_glu.astype(F32).reshape(1, g * ch))


def _state_to_slabs(s_re, s_im):
    b, g, p = s_re.shape
    re = s_re.astype(F32).reshape(b, g // 2, 2 * p).transpose(1, 0, 2)
    im = s_im.astype(F32).reshape(b, g // 2, 2 * p).transpose(1, 0, 2)
    return jnp.stack([re, im], axis=1).reshape(g, b, 2 * p)


def _slabs_to_state(h, p):
    n_slab, b, _ = h.shape
    h = h.reshape(n_slab // 2, 2, b, 2 * p)
    re = h[:, 0].transpose(1, 0, 2).reshape(b, n_slab, p)
    im = h[:, 1].transpose(1, 0, 2).reshape(b, n_slab, p)
    return re, im


def _outproj_kernel(yp_ref, ys_ref, x_ref, g1_ref, sc2_ref, sh2_ref, w_ref, lng_ref, lnb_ref, wr_ref, br_ref,
                    x1_ref, hp_ref, idx_ref, gate_ref, cnt_ref, *, n_experts):
    first = jnp.logical_and(pl.program_id(0) == 0, pl.program_id(1) == 0)
    ymix = jnp.concatenate([yp_ref[...], ys_ref[...]], axis=1)
    mix = jnp.dot(ymix, w_ref[...], preferred_element_type=F32)
    x1 = _layer_norm(DN_ALPHA * x_ref[...] + g1_ref[...] * mix, lng_ref[...], lnb_ref[...])
    x1_ref[...] = x1
    h2 = x1 * (1.0 + sc2_ref[...]) + sh2_ref[...]
    hb = h2.astype(BF16)
    half = hb.shape[1] // 2
    lo = pltpu.bitcast(hb[:, :half].astype(F32), U32) >> 16
    hi = pltpu.bitcast(hb[:, half:].astype(F32), U32) & jnp.uint32(0xFFFF0000)
    hp_ref[...] = lo | hi

    logits = jnp.dot(hb, wr_ref[...], preferred_element_type=F32) + br_ref[...]
    tm = logits.shape[0]
    lane = lax.broadcasted_iota(I32, (tm, LANES), 1).astype(F32)
    cur = jnp.where(lane < n_experts, logits, -jnp.inf)
    vals, onehot = [], jnp.zeros((tm, LANES), F32)
    for k in range(TOP_K):
        m = jnp.max(cur, axis=1, keepdims=True)
        sel = jnp.min(jnp.where(cur == m, lane, float(LANES)), axis=1, keepdims=True)
        hit = lane == sel
        idx_ref[:, k:k + 1] = sel.astype(I32)
        vals.append(m)
        onehot = onehot + jnp.where(hit, 1.0, 0.0)
        cur = jnp.where(hit, -jnp.inf, cur)
    es = [jnp.exp(v - vals[0]) for v in vals]
    den = es[0] + es[1] + es[2] + es[3]
    for k in range(TOP_K):
        gate_ref[:, k:k + 1] = es[k] / den

    @pl.when(first)
    def _():
        cnt_ref[...] = jnp.zeros_like(cnt_ref)

    cnt_ref[0:1, :] += jnp.sum(onehot, axis=0, keepdims=True)


def _outproj(y_pool, y_ssm, x, g1, sc2, sh2, w_out_bf, ln_g, ln_b, wr_bf, br_pad, n_experts):
    b, l, d = x.shape
    dp = y_pool.shape[-1]
    tm = min(l, 256)
    nl = l // tm
    t = b * l
    row = pl.BlockSpec((None, tm, d), lambda bi, i: (bi, i, 0))
    halfrow = pl.BlockSpec((None, tm, dp), lambda bi, i: (bi, i, 0))
    mod = pl.BlockSpec((None, 1, d), lambda bi, i: (bi, 0, 0))
    vec = pl.BlockSpec((1, d), lambda bi, i: (0, 0))
    tok4 = pl.BlockSpec((tm, TOP_K), lambda bi, i: (bi * nl + i, 0))
    kern = functools.partial(_outproj_kernel, n_experts=n_experts)
    return pl.pallas_call(
        kern,
        grid=(b, nl),
        in_specs=[halfrow, halfrow, row, mod, mod, mod,
                  pl.BlockSpec((d, d), lambda bi, i: (0, 0)), vec, vec,
                  pl.BlockSpec((d, LANES), lambda bi, i: (0, 0)),
                  pl.BlockSpec((1, LANES), lambda bi, i: (0, 0))],
        out_specs=[row,
                   pl.BlockSpec((tm, d // 2), lambda bi, i: (bi * nl + i, 0)),
                   tok4, tok4,
                   pl.BlockSpec((SUBLANES, LANES), lambda bi, i: (0, 0))],
        out_shape=[jax.ShapeDtypeStruct((b, l, d), F32),
                   jax.ShapeDtypeStruct((t, d // 2), U32),
                   jax.ShapeDtypeStruct((t, TOP_K), I32),
                   jax.ShapeDtypeStruct((t, TOP_K), F32),
                   jax.ShapeDtypeStruct((SUBLANES, LANES), F32)],
        compiler_params=_params(("arbitrary", "arbitrary"), 48),
        name="out_proj_router",
    )(y_pool, y_ssm, x, g1, sc2, sh2, w_out_bf, ln_g, ln_b, wr_bf, br_pad)


def _plan_kernel(idx_ref, start_ref, pos_ref, carry):
    i = pl.program_id(0)

    @pl.when(i == 0)
    def _():
        carry[...] = jnp.zeros_like(carry)

    idx = idx_ref[...]
    tp = idx.shape[0]
    lane = lax.broadcasted_iota(I32, (tp, LANES), 1)
    hits = [lane == idx[:, k:k + 1] for k in range(TOP_K)]
    onehot = jnp.zeros((tp, LANES), F32)
    for h in hits:
        onehot = onehot + jnp.where(h, 1.0, 0.0)
    below = jnp.where(lax.broadcasted_iota(I32, (tp, tp), 0) > lax.broadcasted_iota(I32, (tp, tp), 1),
                      1.0, 0.0).astype(BF16)
    excl = jnp.dot(below, onehot.astype(BF16), preferred_element_type=F32)
    slot = start_ref[0:1, :] + carry[0:1, :] + excl
    for k in range(TOP_K):
        pos_ref[:, k:k + 1] = jnp.sum(jnp.where(hits[k], slot, 0.0), axis=1, keepdims=True).astype(I32)
    carry[0:1, :] += jnp.sum(onehot, axis=0, keepdims=True)


def _plan(idx_all, start_pad):
    t = idx_all.shape[0]
    tp = 128
    return pl.pallas_call(
        _plan_kernel,
        grid=(t // tp,),
        in_specs=[pl.BlockSpec((tp, TOP_K), lambda i: (i, 0)),
                  pl.BlockSpec((SUBLANES, LANES), lambda i: (0, 0))],
        out_specs=pl.BlockSpec((tp, TOP_K), lambda i: (i, 0)),
        out_shape=jax.ShapeDtypeStruct((t, TOP_K), I32),
        scratch_shapes=[pltpu.VMEM((SUBLANES, LANES), F32)],
        compiler_params=_params(("arbitrary",), 16),
        name="moe_plan",
    )(idx_all, start_pad)


def _dispatch_kernel(pos_ref, h_ref, *rest, td):
    xs_ref, sem = rest[-2], rest[-1]

    def row_copy(t, p):
        return pltpu.make_async_copy(h_ref.at[pl.ds(t, 1)], xs_ref.at[pl.ds(p, 1)], sem)

    def body(t, carry):
        for k in range(TOP_K):
            row_copy(t, pos_ref[0, 0, t * TOP_K + k]).start()
        return carry

    lax.fori_loop(0, td, body, 0)
    for k in range(TOP_K):
        pltpu.make_async_copy(h_ref, xs_ref.at[pl.ds(0, td)], sem).wait()


def _dispatch(pos, hp, n_rows, xs=None):
    t, w = hp.shape
    td = min(t, 256)
    pos3 = pos.reshape(t // td, 1, td * TOP_K)
    in_specs = [pl.BlockSpec((1, 1, td * TOP_K), lambda i: (i, 0, 0), memory_space=pltpu.SMEM),
                pl.BlockSpec((td, w), lambda i: (i, 0))]
    args = [pos3, hp]
    aliases = {}
    if xs is not None:
        in_specs.append(pl.BlockSpec(memory_space=pl.ANY))
        args.append(xs)
        aliases = {2: 0}
    return pl.pallas_call(
        functools.partial(_dispatch_kernel, td=td),
        grid=(t // td,),
        in_specs=in_specs,
        out_specs=pl.BlockSpec(memory_space=pl.ANY),
        out_shape=jax.ShapeDtypeStruct((n_rows, w), U32),
        scratch_shapes=[pltpu.SemaphoreType.DMA(())],
        input_output_aliases=aliases,
        compiler_params=_params(("arbitrary",), 16),
        name="moe_dispatch",
    )(*args)


def _moe_kernel(ie_ref, ir_ref, inb_ref, ni_ref, xs_ref, wg_ref, wu_ref, bg_ref, bu_ref, wd_ref, bd_ref,
                ys_ref, xbuf, act, wgu, wdb, ystage, xsem, ysem, *, nc):
    i = pl.program_id(0)
    c = pl.program_id(1)
    valid = i < ni_ref[0]
    nb = inb_ref[i]
    row0 = pl.multiple_of(ir_ref[i], ROW_BLK)
    nrb = lax.shift_right_logical(nb + 1, 1)
    half_blk = MOE_RB // ROW_BLK

    @pl.when(jnp.logical_and(i == 0, c == 0))
    def _():
        xbuf[...] = jnp.zeros_like(xbuf)

    def x_copy(j):
        return pltpu.make_async_copy(xs_ref.at[pl.ds(pl.multiple_of(row0 + j * ROW_BLK, ROW_BLK), ROW_BLK)],
                                     xbuf.at[pl.ds(pl.multiple_of(j * ROW_BLK, ROW_BLK), ROW_BLK)], xsem)

    @pl.when(jnp.logical_and(valid, c == 0))
    def _():
        def start(j, carry):
            x_copy(j).start()
            return carry

        def wait(j, carry):
            x_copy(j).wait()
            return carry

        lax.fori_loop(0, nb, start, 0)
        lax.fori_loop(0, nb, wait, 0)

    @pl.when(jnp.logical_and(valid, c < nc))
    def _():
        wgu[:, :MOE_CW] = wg_ref[...].astype(BF16)
        wgu[:, MOE_CW:] = wu_ref[...].astype(BF16)
        bias = jnp.concatenate([bg_ref[...], bu_ref[...]], axis=1)

        def rb(r, carry):
            r0 = pl.multiple_of(r * MOE_RB, MOE_RB)
            xw = xbuf[pl.ds(r0, MOE_RB), :]
            lo = pltpu.bitcast(xw << 16, F32).astype(BF16)
            hi = pltpu.bitcast(xw & jnp.uint32(0xFFFF0000), F32).astype(BF16)
            x = jnp.concatenate([lo, hi], axis=1)
            gu = jnp.dot(x, wgu[...], preferred_element_type=F32) + bias
            gate = jnp.minimum(gu[:, :MOE_CW], SWIGLU_LIMIT)
            up = jnp.clip(gu[:, MOE_CW:], -SWIGLU_LIMIT, SWIGLU_LIMIT)
            a = (up + 1.0) * (gate * jax.nn.sigmoid(SWIGLU_ALPHA * gate))
            act[c, pl.ds(r0, MOE_RB), :] = a.astype(BF16)
            return carry

        lax.fori_loop(0, nrb, rb, 0)

    @pl.when(jnp.logical_and(valid, c >= nc))
    def _():
        n = c - nc
        col0 = pl.multiple_of(n * MOE_CW, MOE_CW)
        wdb[...] = wd_ref[...].astype(BF16)

        def y_copy(r, h):
            slot = r & 1
            dst_row = pl.multiple_of(row0 + (r * half_blk + h) * ROW_BLK, ROW_BLK)
            return pltpu.make_async_copy(ystage.at[slot, pl.ds(h * ROW_BLK, ROW_BLK)],
                                         ys_ref.at[pl.ds(dst_row, ROW_BLK), pl.ds(col0, MOE_CW)],
                                         ysem.at[slot, h])

        def wait_block(r):
            for h in range(half_blk):
                @pl.when(r * half_blk + h < nb)
                def _(h=h):
                    y_copy(r, h).wait()

        def rb(r, carry):
            r0 = pl.multiple_of(r * MOE_RB, MOE_RB)

            @pl.when(r >= 2)
            def _():
                wait_block(r - 2)

            a = jnp.concatenate([act[cc, pl.ds(r0, MOE_RB), :] for cc in range(nc)], axis=1)
            y = jnp.dot(a, wdb[...], preferred_element_type=F32) + bd_ref[...]
            ystage[r & 1] = y
            for h in range(half_blk):
                @pl.when(r * half_blk + h < nb)
                def _(h=h):
                    y_copy(r, h).start()
            return carry

        lax.fori_loop(0, nrb, rb, 0)
        for d in (2, 1):
            @pl.when(nrb - d >= 0)
            def _(d=d):
                wait_block(nrb - d)


def _moe(xs, item_e, item_row0, item_nblk, n_items, w_gate_up, b_gate_up, w_down, b_down):
    n_rows, hw = xs.shape
    n_exp, d, ff2 = w_gate_up.shape
    ff = ff2 // 2
    assert d == 2 * hw and ff == d
    nc = ff // MOE_CW
    ni = item_e.shape[0]
    rbuf = MOE_MAX_BLK * ROW_BLK

    def item(i, n):
        return jnp.minimum(i, n[0] - 1)

    def c1(i, c, n):
        return jnp.where(i < n[0], jnp.minimum(c, nc - 1), nc - 1)

    def c2(i, c, n):
        return jnp.where(i < n[0], jnp.maximum(c - nc, 0), nc - 1)

    bgu3 = b_gate_up.reshape(n_exp, 1, ff2)
    bd3 = b_down.reshape(n_exp, 1, d)
    in_specs = [
        pl.BlockSpec(memory_space=pl.ANY),
        pl.BlockSpec((None, d, MOE_CW), lambda i, c, e, r, b, n: (e[item(i, n)], 0, c1(i, c, n))),
        pl.BlockSpec((None, d, MOE_CW), lambda i, c, e, r, b, n: (e[item(i, n)], 0, nc + c1(i, c, n))),
        pl.BlockSpec((None, 1, MOE_CW), lambda i, c, e, r, b, n: (e[item(i, n)], 0, c1(i, c, n))),
        pl.BlockSpec((None, 1, MOE_CW), lambda i, c, e, r, b, n: (e[item(i, n)], 0, nc + c1(i, c, n))),
        pl.BlockSpec((None, ff, MOE_CW), lambda i, c, e, r, b, n: (e[item(i, n)], 0, c2(i, c, n))),
        pl.BlockSpec((None, 1, MOE_CW), lambda i, c, e, r, b, n: (e[item(i, n)], 0, c2(i, c, n))),
    ]
    grid_spec = pltpu.PrefetchScalarGridSpec(
        num_scalar_prefetch=4,
        grid=(ni, 2 * nc),
        in_specs=in_specs,
        out_specs=pl.BlockSpec(memory_space=pl.ANY),
        scratch_shapes=[pltpu.VMEM((rbuf, hw), U32),
                        pltpu.VMEM((nc, rbuf, MOE_CW), BF16),
                        pltpu.VMEM((d, 2 * MOE_CW), BF16),
                        pltpu.VMEM((ff, MOE_CW), BF16),
                        pltpu.VMEM((2, MOE_RB, MOE_CW), F32),
                        pltpu.SemaphoreType.DMA(()),
                        pltpu.SemaphoreType.DMA((2, MOE_RB // ROW_BLK))],
    )
    return pl.pallas_call(
        functools.partial(_moe_kernel, nc=nc),
        grid_spec=grid_spec,
        out_shape=jax.ShapeDtypeStruct((n_rows, d), F32),
        compiler_params=_params(("arbitrary", "arbitrary"), 56),
        name="moe_experts",
    )(item_e, item_row0, item_nblk, n_items, xs, w_gate_up, w_gate_up, bgu3, bgu3, w_down, bd3)


def _combine_kernel(pos_ref, ys_ref, gate_ref, x1_ref, g2_ref, lng_ref, lnb_ref, o_ref, gbuf, sem, *, tc):
    def body(t, carry):
        for k in range(TOP_K):
            p = pos_ref[0, 0, t * TOP_K + k]
            pltpu.make_async_copy(ys_ref.at[pl.ds(p, 1)], gbuf.at[k, pl.ds(t, 1)], sem).start()
        return carry

    lax.fori_loop(0, tc, body, 0)
    for k in range(TOP_K):
        pltpu.make_async_copy(ys_ref.at[pl.ds(0, tc)], gbuf.at[k], sem).wait()
    gt = gate_ref[...]
    ffn = gt[:, 0:1] * gbuf[0]
    for k in range(1, TOP_K):
        ffn = ffn + gt[:, k:k + 1] * gbuf[k]
    o_ref[...] = _layer_norm(DN_ALPHA * x1_ref[...] + g2_ref[...] * ffn, lng_ref[...], lnb_ref[...])


def _combine(pos, ys, gates, x1, g2, ln_g, ln_b):
    b, l, d = x1.shape
    t = b * l
    tc = min(l, 128)
    per_b = l // tc
    pos3 = pos.reshape(t // tc, 1, tc * TOP_K)
    vec = pl.BlockSpec((1, d), lambda i: (0, 0))
    out = pl.pallas_call(
        functools.partial(_combine_kernel, tc=tc),
        grid=(t // tc,),
        in_specs=[pl.BlockSpec((1, 1, tc * TOP_K), lambda i: (i, 0, 0), memory_space=pltpu.SMEM),
                  pl.BlockSpec(memory_space=pl.ANY),
                  pl.BlockSpec((tc, TOP_K), lambda i: (i, 0)),
                  pl.BlockSpec((tc, d), lambda i: (i, 0)),
                  pl.BlockSpec((None, 1, d), lambda i: (i // per_b, 0, 0)),
                  vec, vec],
        out_specs=pl.BlockSpec((tc, d), lambda i: (i, 0)),
        out_shape=jax.ShapeDtypeStruct((t, d), F32),
        scratch_shapes=[pltpu.VMEM((TOP_K, tc, d), F32), pltpu.SemaphoreType.DMA(())],
        compiler_params=_params(("arbitrary",), 32),
        name="moe_combine",
    )(pos3, ys, gates, x1.reshape(t, d), g2, ln_g, ln_b)
    return out.reshape(b, l, d)


def _moe_items(counts, n_blk_total):
    n_exp = counts.shape[0]
    nblk = (counts + ROW_BLK - 1) // ROW_BLK
    blk_end = jnp.cumsum(nblk)
    blk0 = blk_end - nblk
    n_it = (nblk + MOE_MAX_BLK - 1) // MOE_MAX_BLK
    it_end = jnp.cumsum(n_it)
    it0 = it_end - n_it
    ni = n_exp + n_blk_total // MOE_MAX_BLK
    j = jnp.arange(ni, dtype=I32)
    e = jnp.minimum(jnp.sum(j[:, None] >= it_end[None, :], axis=1), n_exp - 1).astype(I32)
    local = j - it0[e]
    item_nblk = jnp.clip(nblk[e] - local * MOE_MAX_BLK, 0, MOE_MAX_BLK).astype(I32)
    item_row0 = ((blk0[e] + local * MOE_MAX_BLK) * ROW_BLK).astype(I32)
    n_items = it_end[-1:].astype(I32)
    item_nblk = jnp.where(j < n_items[0], item_nblk, 0)
    item_row0 = jnp.where(j < n_items[0], item_row0, 0)
    row_start = (blk0 * ROW_BLK).astype(F32)
    return e, item_row0, item_nblk, n_items, row_start


def _mixer_stream(x, mod, hist, h0, pos0, w):
    b, l, d = x.shape
    sh1, sc1, g1, sh2, sc2, g2 = [m.reshape(b, 1, d) for m in jnp.split(mod, 6, axis=-1)]
    u_pool, u_ssm = _inproj(x, sc1, sh1, w["w_in"])
    hist16 = jnp.concatenate([jnp.zeros((b, HIST_ROWS - hist.shape[1], hist.shape[2]), F32), hist.astype(F32)], axis=1)
    y_pool, new_hist = _pool(u_pool, hist16, w["w_pool"], w["pool_scale"], pos0)
    y_ssm, h_new = _ssm(u_ssm, w["ssm"], h0)
    x1, hp, idx, gates, cnt = _outproj(y_pool, y_ssm, x, g1, sc2, sh2, w["w_out"], w["ln1_g"], w["ln1_b"],
                                       w["w_router"], w["b_router"], w["n_experts"])
    s_re, s_im = _slabs_to_state(h_new, SSM_STATE)
    return dict(x1=x1, hp=hp, idx=idx, gates=gates, cnt=cnt, g2=g2,
                hist=new_hist[:, 1:, :], s_re=s_re, s_im=s_im)


def kernel(x_prompt, x_sample, cache_pool, state_ssm_re, state_ssm_im, c_prompt, c_sample, w_ada, b_ada, w_in, w_pool, pool_scale, lambda_re, lambda_im, log_dt, ssm_b_re, ssm_b_im, ssm_c_re, ssm_c_im, d_skip, w_glu, b_glu, w_out, ln1_g, ln1_b, w_router, b_router, w_gate_up, b_gate_up, w_down, b_down, ln2_g, ln2_b):
    assert w_ada.shape[0] == DEPTH
    bp, lp, d = x_prompt.shape
    bs, ls, _ = x_sample.shape
    assert bp == SUBLANES and bs == SUBLANES, "the S5 kernel puts the 8 streams on sublanes"
    n_exp = w_router.shape[-1]
    dp = w_pool.shape[1] * w_pool.shape[2]

    w = dict(
        w_in=w_in[0].astype(BF16),
        w_pool=w_pool[0].astype(BF16),
        pool_scale=pool_scale[0].astype(F32).reshape(1, dp),
        ssm=_ssm_params(lambda_re[0], lambda_im[0], log_dt[0], ssm_b_re[0], ssm_b_im[0], ssm_c_re[0], ssm_c_im[0],
                        d_skip[0], w_glu[0], b_glu[0], SUBLANES),
        w_out=w_out[0].astype(BF16),
        ln1_g=ln1_g[0].reshape(1, d), ln1_b=ln1_b[0].reshape(1, d),
        w_router=jnp.pad(w_router[0], ((0, 0), (0, LANES - n_exp))).astype(BF16),
        b_router=jnp.pad(b_router[0].astype(F32), (0, LANES - n_exp)).reshape(1, LANES),
        n_experts=n_exp,
    )

    mod = _ada(jnp.concatenate([c_prompt, c_sample], axis=0), w_ada[0], b_ada[0].reshape(1, -1))
    n_slab = lambda_re.shape[1]
    p = _mixer_stream(x_prompt, mod[:bp], jnp.zeros((bp, HIST_ROWS - 1, dp), F32),
                      jnp.zeros((n_slab, bp, LANES), F32), 0, w)
    s = _mixer_stream(x_sample, mod[bp:], cache_pool[0], _state_to_slabs(state_ssm_re[0], state_ssm_im[0]),
                      PAST_LEN, w)

    tp, ts = bp * lp, bs * ls
    n_assign = (tp + ts) * TOP_K
    n_blk_total = -(-n_assign // ROW_BLK) + n_exp
    n_rows = n_blk_total * ROW_BLK
    counts = (p["cnt"][0, :n_exp] + s["cnt"][0, :n_exp]).astype(I32)
    item_e, item_row0, item_nblk, n_items, row_start = _moe_items(counts, n_blk_total)
    start_pad = jnp.zeros((SUBLANES, LANES), F32).at[0, :n_exp].set(row_start)
    pos = _plan(jnp.concatenate([p["idx"], s["idx"]], axis=0), start_pad)
    pos_p, pos_s = pos[:tp], pos[tp:]
    xs = _dispatch(pos_p, p["hp"], n_rows)
    xs = _dispatch(pos_s, s["hp"], n_rows, xs)
    ys = _moe(xs, item_e, item_row0, item_nblk, n_items, w_gate_up[0], b_gate_up[0], w_down[0], b_down[0])
    g2l, b2l = ln2_g[0].reshape(1, d), ln2_b[0].reshape(1, d)
    y_p = _combine(pos_p, ys, p["gates"], p["x1"], p["g2"], g2l, b2l)
    y_s = _combine(pos_s, ys, s["gates"], s["x1"], s["g2"], g2l, b2l)

    return (y_p, y_s, p["hist"][None], p["s_re"][None], p["s_im"][None],
            s["hist"][None], s["s_re"][None], s["s_im"][None])
```

```python
import functools
import math

import jax
import jax.numpy as jnp
from jax import lax
from jax.experimental import pallas as pl
from jax.experimental.pallas import tpu as pltpu

F32 = jnp.float32
BF16 = jnp.bfloat16
I32 = jnp.int32
U32 = jnp.uint32

POOL_WINDOWS = (2, 4, 8, 16)
HIST_ROWS = 16
SSM_CH = 16
SSM_STATE = 64
TOP_K = 4
SWIGLU_LIMIT = 7.0
SWIGLU_ALPHA = 1.702
LN_EPS = 1e-5
DEPTH = 1
DN_ALPHA = (2 * DEPTH) ** 0.25
PAST_LEN = 2048

LANES = 128
SUBLANES = 8
MXU_DIM = 256
V7X_VMEM_BYTES = 64 * 1024 * 1024
MIB = 1024 * 1024

ROW_BLK = 256
MOE_RB = 512
MOE_CW = 256
MOE_MAX_BLK = 10


def _params(sem, need_bytes):
    limit = min(need_bytes + 16 * MIB, V7X_VMEM_BYTES - 4 * MIB)
    return pltpu.CompilerParams(dimension_semantics=sem, vmem_limit_bytes=limit)


def _layer_norm(v, g, b):
    mu = jnp.mean(v, axis=-1, keepdims=True)
    vc = v - mu
    var = jnp.mean(vc * vc, axis=-1, keepdims=True)
    return vc * lax.rsqrt(var + LN_EPS) * g + b


def _ada_kernel(c_ref, w_ref, b_ref, o_ref):
    c = c_ref[...]
    s = c * jax.nn.sigmoid(c)
    o_ref[...] = jnp.dot(s.astype(BF16), w_ref[...].astype(BF16), preferred_element_type=F32) + b_ref[...]


def _ada(c_all, w_ada, b_ada):
    r, d = c_all.shape
    n = w_ada.shape[1]
    tn = 1536
    return pl.pallas_call(
        _ada_kernel,
        grid=(n // tn,),
        in_specs=[pl.BlockSpec((r, d), lambda j: (0, 0)),
                  pl.BlockSpec((d, tn), lambda j: (0, j)),
                  pl.BlockSpec((1, tn), lambda j: (0, j))],
        out_specs=pl.BlockSpec((r, tn), lambda j: (0, j)),
        out_shape=jax.ShapeDtypeStruct((r, n), F32),
        compiler_params=_params(("arbitrary",), 2 * (d * tn * 4 + r * (d + tn) * 4)),
        name="ada_mod",
    )(c_all, w_ada, b_ada)


def _inproj_kernel(x_ref, sc_ref, sh_ref, w_ref, up_ref, us_ref):
    h = x_ref[...] * (1.0 + sc_ref[...]) + sh_ref[...]
    u = jnp.dot(h.astype(BF16), w_ref[...], preferred_element_type=F32)
    dp = up_ref.shape[-1]
    up_ref[...] = u[:, :dp]
    us_ref[...] = u[:, dp:]


def _inproj(x, sc, sh, w_bf):
    b, l, d = x.shape
    dm = w_bf.shape[1]
    dp = dm // 2
    tm = min(l, 512)
    row = pl.BlockSpec((None, tm, d), lambda bi, i: (bi, i, 0))
    mod = pl.BlockSpec((None, 1, d), lambda bi, i: (bi, 0, 0))
    half = pl.BlockSpec((None, tm, dp), lambda bi, i: (bi, i, 0))
    return pl.pallas_call(
        _inproj_kernel,
        grid=(b, l // tm),
        in_specs=[row, mod, mod, pl.BlockSpec((d, dm), lambda bi, i: (0, 0))],
        out_specs=[half, half],
        out_shape=[jax.ShapeDtypeStruct((b, l, dp), F32), jax.ShapeDtypeStruct((b, l, dm - dp), F32)],
        compiler_params=_params(("arbitrary", "arbitrary"), 2 * (d * dm * 2 + tm * (d + dm) * 4)),
        name="in_proj",
    )(x, sc, sh, w_bf)


def _pool_kernel(u_ref, hist_ref, w_ref, scale_ref, y_ref, nh_ref, hbuf, *, tt, pos0):
    i = pl.program_id(1)

    @pl.when(i == 0)
    def _():
        hbuf[...] = hist_ref[...]

    u = u_ref[...]
    hb = hbuf[...]
    u16 = u.astype(BF16)
    h16 = hb.astype(BF16)
    pc = u.shape[1] // len(POOL_WINDOWS)
    d_main = lax.broadcasted_iota(I32, (tt, tt), 0) - lax.broadcasted_iota(I32, (tt, tt), 1)
    d_hist = (lax.broadcasted_iota(I32, (tt, HIST_ROWS), 0) + HIST_ROWS
              - lax.broadcasted_iota(I32, (tt, HIST_ROWS), 1))
    pos = (pos0 + i * tt + lax.broadcasted_iota(I32, (tt, 1), 0)).astype(F32)
    for g, w in enumerate(POOL_WINDOWS):
        sl = slice(g * pc, (g + 1) * pc)
        bm = jnp.where(d_main >= 0, jnp.where(d_main < w, 1.0, 0.0), 0.0).astype(BF16)
        bh = jnp.where(d_hist < w, 1.0, 0.0).astype(BF16)
        s = (jnp.dot(bm, u16[:, sl], preferred_element_type=F32)
             + jnp.dot(bh, h16[:, sl], preferred_element_type=F32))
        cnt = jnp.minimum(pos + 1.0, float(w))
        diff = s / cnt - u[:, sl]
        y = jnp.dot(diff.astype(BF16), w_ref[g], preferred_element_type=F32) * scale_ref[:, sl]
        y_ref[:, sl] = y.astype(y_ref.dtype)
    hbuf[...] = u[tt - HIST_ROWS:, :]

    @pl.when(i == pl.num_programs(1) - 1)
    def _():
        nh_ref[...] = hbuf[...]


def _pool(u_pool, hist16, w_pool_bf, pool_scale, pos0):
    b, l, dp = u_pool.shape
    tt = min(l, 256)
    g, pc, _ = w_pool_bf.shape
    kern = functools.partial(_pool_kernel, tt=tt, pos0=pos0)
    return pl.pallas_call(
        kern,
        grid=(b, l // tt),
        in_specs=[pl.BlockSpec((None, tt, dp), lambda bi, i: (bi, i, 0)),
                  pl.BlockSpec((None, HIST_ROWS, dp), lambda bi, i: (bi, 0, 0)),
                  pl.BlockSpec((g, pc, pc), lambda bi, i: (0, 0, 0)),
                  pl.BlockSpec((1, dp), lambda bi, i: (0, 0))],
        out_specs=[pl.BlockSpec((None, tt, dp), lambda bi, i: (bi, i, 0)),
                   pl.BlockSpec((None, HIST_ROWS, dp), lambda bi, i: (bi, 0, 0))],
        out_shape=[jax.ShapeDtypeStruct((b, l, dp), BF16), jax.ShapeDtypeStruct((b, HIST_ROWS, dp), F32)],
        scratch_shapes=[pltpu.VMEM((HIST_ROWS, dp), F32)],
        compiler_params=_params(("arbitrary", "arbitrary"), 2 * (tt * dp * 6 + g * pc * pc * 2 + 2 * HIST_ROWS * dp * 4)),
        name="pool_mixer",
    )(u_pool, hist16, w_pool_bf, pool_scale)


SSM_PAIR_GROUP = 8


def _ssm_kernel(u_ref, bc_ref, cc_ref, wg_ref, ar_ref, ai_ref, dsk_ref, bgl_ref, h0_ref,
                y_ref, hn_ref, s_ref, ubuf, hst, *, tt, pitch, nb):
    i = pl.program_id(0)
    n_slab = s_ref.shape[0]
    n_chunk = bc_ref.shape[0]
    slab_per_chunk = n_slab // n_chunk
    cw = bc_ref.shape[1]

    @pl.when(i == 0)
    def _():
        s_ref[...] = jnp.zeros_like(s_ref)
        ubuf[...] = jnp.zeros_like(ubuf)
        hst[...] = h0_ref[...]

    u = u_ref[...].reshape(nb * tt, u_ref.shape[-1])
    u16 = u.astype(BF16)
    for k in range(n_chunk):
        bu = jnp.dot(u16[:, k * cw:(k + 1) * cw], bc_ref[k], preferred_element_type=F32)
        for j in range(slab_per_chunk):
            for b in range(nb):
                s_ref[k * slab_per_chunk + j, b * pitch:b * pitch + tt, :] = (
                    bu[b * tt:(b + 1) * tt, j * LANES:(j + 1) * LANES])
    for b in range(nb):
        ubuf[b * pitch:b * pitch + tt, :] = u[b * tt:(b + 1) * tt, :]

    n_pair = n_slab // 2
    for pg in range(n_pair // SSM_PAIR_GROUP):
        qs = [pg * SSM_PAIR_GROUP + j for j in range(SSM_PAIR_GROUP)]
        ars = [ar_ref[q] for q in qs]
        ais = [ai_ref[q] for q in qs]

        def body(t, carry, qs=qs, ars=ars, ais=ais):
            hr, hi = carry
            nhr, nhi = [], []
            rows = pl.ds(t, nb, stride=pitch)
            for j, q in enumerate(qs):
                bur = s_ref[2 * q, rows, :]
                bui = s_ref[2 * q + 1, rows, :]
                r_ = ars[j] * hr[j] - ais[j] * hi[j] + bur
                i_ = ars[j] * hi[j] + ais[j] * hr[j] + bui
                s_ref[2 * q, rows, :] = r_
                s_ref[2 * q + 1, rows, :] = i_
                nhr.append(r_)
                nhi.append(i_)
            return tuple(nhr), tuple(nhi)

        init = (tuple(hst[2 * q] for q in qs), tuple(hst[2 * q + 1] for q in qs))
        hr, hi = lax.fori_loop(0, tt, body, init)
        for j, q in enumerate(qs):
            hst[2 * q] = hr[j]
            hst[2 * q + 1] = hi[j]

    ycs = []
    for n in range(n_chunk):
        hc = jnp.concatenate([s_ref[n * slab_per_chunk + j] for j in range(slab_per_chunk)], axis=1)
        ycs.append(jnp.dot(hc.astype(BF16), cc_ref[n], preferred_element_type=F32))
    y = jnp.concatenate(ycs, axis=1) + dsk_ref[...] * ubuf[...]
    y = jax.nn.gelu(y)
    y16 = y.astype(BF16)
    z = jnp.concatenate([jnp.dot(y16[:, k * cw:(k + 1) * cw], wg_ref[k], preferred_element_type=F32)
                         for k in range(n_chunk)], axis=1) + bgl_ref[...]
    out = y * jax.nn.sigmoid(z)
    for b in range(nb):
        y_ref[b] = out[b * pitch:b * pitch + tt].astype(y_ref.dtype)

    @pl.when(i == pl.num_programs(0) - 1)
    def _():
        hn_ref[...] = hst[...]


def _ssm(u_ssm, prm, h0):
    b, l, ds = u_ssm.shape
    tt = min(l, 64)
    pitch = tt + 4
    n_slab = h0.shape[0]
    n_chunk = prm["bc"].shape[0]
    kern = functools.partial(_ssm_kernel, tt=tt, pitch=pitch, nb=b)

    def const(a):
        nd = a.ndim
        return pl.BlockSpec(a.shape, lambda i, nd=nd: (0,) * nd)

    args = (u_ssm, prm["bc"], prm["cc"], prm["wg"], prm["ar"], prm["ai"], prm["dsk"], prm["bgl"], h0)
    return pl.pallas_call(
        kern,
        grid=(l // tt,),
        in_specs=[pl.BlockSpec((b, tt, ds), lambda i: (0, i, 0))] + [const(a) for a in args[1:]],
        out_specs=[pl.BlockSpec((b, tt, ds), lambda i: (0, i, 0)),
                   pl.BlockSpec((n_slab, b, LANES), lambda i: (0, 0, 0))],
        out_shape=[jax.ShapeDtypeStruct((b, l, ds), BF16), jax.ShapeDtypeStruct((n_slab, b, LANES), F32)],
        scratch_shapes=[pltpu.VMEM((n_slab, b * pitch, LANES), F32),
                        pltpu.VMEM((b * pitch, ds), F32),
                        pltpu.VMEM((n_slab, b, LANES), F32)],
        compiler_params=_params(("arbitrary",), (n_slab * LANES + ds) * b * pitch * 4 + 2 * b * tt * ds * 6
                                + 2 * sum(a.size * a.dtype.itemsize for a in args[1:])),
        name="ssm_mixer",
    )(*args)


def _ssm_params(lambda_re, lambda_im, log_dt, b_re, b_im, c_re, c_im, d_skip, w_glu, b_glu, nb):
    g, p = lambda_re.shape
    ch = b_re.shape[-1]
    gpc = MXU_DIM // ch
    n_chunk = g // gpc
    ppc = gpc // 2
    lr, li = lambda_re.astype(F32), lambda_im.astype(F32)
    dt = jnp.exp(log_dt.astype(F32))[:, None]
    mag = jnp.exp(lr * dt)
    abar_re, abar_im = mag * jnp.cos(li * dt), mag * jnp.sin(li * dt)
    nr, ni = abar_re - 1.0, abar_im
    den = lr * lr + li * li
    k_re = (nr * lr + ni * li) / den
    k_im = (ni * lr - nr * li) / den
    br, bi = b_re.astype(F32), b_im.astype(F32)
    bb_re = k_re[..., None] * br - k_im[..., None] * bi
    bb_im = k_re[..., None] * bi + k_im[..., None] * br
    eye_p = jnp.eye(ppc, dtype=F32)
    eye_2 = jnp.eye(2, dtype=F32)
    bb = jnp.stack([bb_re, bb_im]).reshape(2, n_chunk, ppc, 2, p, ch)
    bc = jnp.einsum("rkqgpc,qx,gy->kqgcxryp", bb, eye_p, eye_2).reshape(n_chunk, gpc * ch, ppc * 2 * 2 * p)
    cm = jnp.stack([c_re.astype(F32), -c_im.astype(F32)]).reshape(2, n_chunk, ppc, 2, ch, p)
    cc = jnp.einsum("rkqgcp,qx,gy->kqrgpxyc", cm, eye_p, eye_2).reshape(n_chunk, ppc * 2 * 2 * p, gpc * ch)
    wg = jnp.einsum("kgce,gx->kgcxe", w_glu.astype(F32).reshape(n_chunk, gpc, ch, ch),
                    jnp.eye(gpc, dtype=F32)).reshape(n_chunk, gpc * ch, gpc * ch)
    n_pair = g // 2
    ar = jnp.broadcast_to(abar_re.reshape(n_pair, 1, 2 * p), (n_pair, nb, 2 * p))
    ai = jnp.broadcast_to(abar_im.reshape(n_pair, 1, 2 * p), (n_pair, nb, 2 * p))
    return dict(bc=bc.astype(BF16), cc=cc.astype(BF16), wg=wg.astype(BF16), ar=ar, ai=ai,
                dsk=d_skip.astype(F32).reshape(1, g * ch), bgl=b_glu.astype(F32).reshape(1, g * ch))


def _state_to_slabs(s_re, s_im):
    b, g, p = s_re.shape
    re = s_re.astype(F32).reshape(b, g // 2, 2 * p).transpose(1, 0, 2)
    im = s_im.astype(F32).reshape(b, g // 2, 2 * p).transpose(1, 0, 2)
    return jnp.stack([re, im], axis=1).reshape(g, b, 2 * p)


def _slabs_to_state(h, p):
    n_slab, b, _ = h.shape
    h = h.reshape(n_slab // 2, 2, b, 2 * p)
    re = h[:, 0].transpose(1, 0, 2).reshape(b, n_slab, p)
    im = h[:, 1].transpose(1, 0, 2).reshape(b, n_slab, p)
    return re, im


def _outproj_kernel(yp_ref, ys_ref, x_ref, g1_ref, sc2_ref, sh2_ref, w_ref, lng_ref, lnb_ref, wr_ref, br_ref,
                    x1_ref, hp_ref, idx_ref, gate_ref, cnt_ref, *, n_experts):
    first = jnp.logical_and(pl.program_id(0) == 0, pl.program_id(1) == 0)
    ymix = jnp.concatenate([yp_ref[...], ys_ref[...]], axis=1)
    mix = jnp.dot(ymix, w_ref[...], preferred_element_type=F32)
    x1 = _layer_norm(DN_ALPHA * x_ref[...] + g1_ref[...] * mix, lng_ref[...], lnb_ref[...])
    x1_ref[...] = x1
    h2 = x1 * (1.0 + sc2_ref[...]) + sh2_ref[...]
    hp_ref[...] = h2
    logits = jnp.dot(h2.astype(BF16), wr_ref[...], preferred_element_type=F32) + br_ref[...]
    tm = logits.shape[0]
    lane = lax.broadcasted_iota(I32, (tm, LANES), 1).astype(F32)
    cur = jnp.where(lane < n_experts, logits, -jnp.inf)
    vals, onehot = [], jnp.zeros((tm, LANES), F32)
    for k in range(TOP_K):
        m = jnp.max(cur, axis=1, keepdims=True)
        sel = jnp.min(jnp.where(cur == m, lane, float(LANES)), axis=1, keepdims=True)
        hit = lane == sel
        idx_ref[:, k:k + 1] = sel.astype(I32)
        vals.append(m)
        onehot = onehot + jnp.where(hit, 1.0, 0.0)
        cur = jnp.where(hit, -jnp.inf, cur)
    es = [jnp.exp(v - vals[0]) for v in vals]
    den = es[0] + es[1] + es[2] + es[3]
    for k in range(TOP_K):
        gate_ref[:, k:k + 1] = es[k] / den

    @pl.when(first)
    def _():
        cnt_ref[...] = jnp.zeros_like(cnt_ref)

    cnt_ref[0:1, :] += jnp.sum(onehot, axis=0, keepdims=True)


def _outproj(y_pool, y_ssm, x, g1, sc2, sh2, w_out_bf, ln_g, ln_b, wr_bf, br_pad, n_experts):
    b, l, d = x.shape
    dp = y_pool.shape[-1]
    tm = min(l, 256)
    nl = l // tm
    t = b * l
    row = pl.BlockSpec((None, tm, d), lambda bi, i: (bi, i, 0))
    halfrow = pl.BlockSpec((None, tm, dp), lambda bi, i: (bi, i, 0))
    mod = pl.BlockSpec((None, 1, d), lambda bi, i: (bi, 0, 0))
    vec = pl.BlockSpec((1, d), lambda bi, i: (0, 0))
    tok4 = pl.BlockSpec((tm, TOP_K), lambda bi, i: (bi * nl + i, 0))
    kern = functools.partial(_outproj_kernel, n_experts=n_experts)
    return pl.pallas_call(
        kern,
        grid=(b, nl),
        in_specs=[halfrow, halfrow, row, mod, mod, mod,
                  pl.BlockSpec((d, d), lambda bi, i: (0, 0)), vec, vec,
                  pl.BlockSpec((d, LANES), lambda bi, i: (0, 0)),
                  pl.BlockSpec((1, LANES), lambda bi, i: (0, 0))],
        out_specs=[row,
                   pl.BlockSpec((tm, d), lambda bi, i: (bi * nl + i, 0)),
                   tok4, tok4,
                   pl.BlockSpec((SUBLANES, LANES), lambda bi, i: (0, 0))],
        out_shape=[jax.ShapeDtypeStruct((b, l, d), F32),
                   jax.ShapeDtypeStruct((t, d), F32),
                   jax.ShapeDtypeStruct((t, TOP_K), I32),
                   jax.ShapeDtypeStruct((t, TOP_K), F32),
                   jax.ShapeDtypeStruct((SUBLANES, LANES), F32)],
        compiler_params=_params(("arbitrary", "arbitrary"), 2 * (d * d * 2 + d * LANES * 2 + tm * d * 14)),
        name="out_proj_router",
    )(y_pool, y_ssm, x, g1, sc2, sh2, w_out_bf, ln_g, ln_b, wr_bf, br_pad)


def _plan_kernel(idx_ref, start_ref, pos_ref, carry):
    i = pl.program_id(0)

    @pl.when(i == 0)
    def _():
        carry[...] = jnp.zeros_like(carry)

    idx = idx_ref[...]
    tp = idx.shape[0]
    lane = lax.broadcasted_iota(I32, (tp, LANES), 1)
    hits = [lane == idx[:, k:k + 1] for k in range(TOP_K)]
    onehot = jnp.zeros((tp, LANES), F32)
    for h in hits:
        onehot = onehot + jnp.where(h, 1.0, 0.0)
    below = jnp.where(lax.broadcasted_iota(I32, (tp, tp), 0) > lax.broadcasted_iota(I32, (tp, tp), 1),
                      1.0, 0.0).astype(BF16)
    excl = jnp.dot(below, onehot.astype(BF16), preferred_element_type=F32)
    slot = start_ref[0:1, :] + carry[0:1, :] + excl
    for k in range(TOP_K):
        pos_ref[:, k:k + 1] = jnp.sum(jnp.where(hits[k], slot, 0.0), axis=1, keepdims=True).astype(I32)
    carry[0:1, :] += jnp.sum(onehot, axis=0, keepdims=True)


def _plan(idx_all, start_pad):
    t = idx_all.shape[0]
    tp = 128
    return pl.pallas_call(
        _plan_kernel,
        grid=(t // tp,),
        in_specs=[pl.BlockSpec((tp, TOP_K), lambda i: (i, 0)),
                  pl.BlockSpec((SUBLANES, LANES), lambda i: (0, 0))],
        out_specs=pl.BlockSpec((tp, TOP_K), lambda i: (i, 0)),
        out_shape=jax.ShapeDtypeStruct((t, TOP_K), I32),
        scratch_shapes=[pltpu.VMEM((SUBLANES, LANES), F32)],
        compiler_params=_params(("arbitrary",), 4 * tp * LANES * 4),
        name="moe_plan",
    )(idx_all, start_pad)


DISPATCH_UNROLL = 2


def _dispatch_kernel(fill_ref, pos_ref, hp_ref, hs_ref, xs_ref, zbuf, sem, zsem, *, td, n_p_steps, n_blk):
    i = pl.program_id(0)

    def fill_copy(b):
        return pltpu.make_async_copy(zbuf, xs_ref.at[pl.ds(pl.multiple_of(b * ROW_BLK, ROW_BLK), ROW_BLK)], zsem)

    @pl.when(i == 0)
    def _():
        zbuf[...] = jnp.zeros_like(zbuf)

        def start(b, carry):
            @pl.when(fill_ref[b] != 0)
            def _():
                fill_copy(b).start()
            return carry

        def wait(b, carry):
            @pl.when(fill_ref[b] != 0)
            def _():
                fill_copy(b).wait()
            return carry

        lax.fori_loop(0, n_blk, start, 0)
        lax.fori_loop(0, n_blk, wait, 0)

    def scatter(src_ref):
        def body(t2, carry):
            for tt in range(DISPATCH_UNROLL):
                t = t2 * DISPATCH_UNROLL + tt
                for k in range(TOP_K):
                    p = pos_ref[0, 0, t * TOP_K + k]
                    pltpu.make_async_copy(src_ref.at[pl.ds(t, 1)], xs_ref.at[pl.ds(p, 1)], sem).start()
            return carry

        lax.fori_loop(0, td // DISPATCH_UNROLL, body, 0)
        for k in range(TOP_K):
            pltpu.make_async_copy(src_ref, xs_ref.at[pl.ds(0, td)], sem).wait()

    @pl.when(i < n_p_steps)
    def _():
        scatter(hp_ref)

    @pl.when(i >= n_p_steps)
    def _():
        scatter(hs_ref)


def _dispatch(pos, h_prompt, h_sample, fill_flags, n_rows):
    tp, w = h_prompt.shape
    ts = h_sample.shape[0]
    td = math.gcd(math.gcd(tp, ts), 128)
    n_p_steps = tp // td
    n_steps = n_p_steps + ts // td
    n_blk = fill_flags.shape[0]
    pos3 = pos.reshape(n_steps, 1, td * TOP_K)
    grid_spec = pltpu.PrefetchScalarGridSpec(
        num_scalar_prefetch=1,
        grid=(n_steps,),
        in_specs=[pl.BlockSpec((1, 1, td * TOP_K), lambda i, f: (i, 0, 0), memory_space=pltpu.SMEM),
                  pl.BlockSpec((td, w), lambda i, f: (jnp.minimum(i, n_p_steps - 1), 0)),
                  pl.BlockSpec((td, w), lambda i, f: (jnp.maximum(i - n_p_steps, 0), 0))],
        out_specs=pl.BlockSpec(memory_space=pl.ANY),
        scratch_shapes=[pltpu.VMEM((ROW_BLK, w), F32),
                        pltpu.SemaphoreType.DMA(()), pltpu.SemaphoreType.DMA(())],
    )
    need = 2 * 2 * td * w * 4 + ROW_BLK * w * 4
    return pl.pallas_call(
        functools.partial(_dispatch_kernel, td=td, n_p_steps=n_p_steps, n_blk=n_blk),
        grid_spec=grid_spec,
        out_shape=jax.ShapeDtypeStruct((n_rows, w), F32),
        compiler_params=_params(("arbitrary",), need),
        name="moe_dispatch",
    )(fill_flags, pos3, h_prompt, h_sample)


def _moe_kernel(ie_ref, ir_ref, inb_ref, meta_ref, xs_ref, wg_ref, wu_ref, bg_ref, bu_ref, wd_ref, bd_ref,
                ys_ref, xbuf, act, wgu, wdb, ystage, zbuf, xsem, ysem, zsem, *, nc, n_blk):
    i = pl.program_id(0)
    c = pl.program_id(1)
    n_items = meta_ref[0]
    used_blk = meta_ref[1]
    valid = i < n_items
    nb = inb_ref[i]
    row0 = pl.multiple_of(ir_ref[i], ROW_BLK)
    blk_per_pair = 2 * MOE_RB // ROW_BLK
    n_pair = nb // blk_per_pair
    rem = nb - n_pair * blk_per_pair
    has_rb = rem >= MOE_RB // ROW_BLK
    has_blk = (rem % (MOE_RB // ROW_BLK)) != 0
    base_rb = pl.multiple_of(n_pair * 2 * MOE_RB, ROW_BLK)
    base_blk = pl.multiple_of(base_rb + jnp.where(has_rb, MOE_RB, 0), ROW_BLK)
    first = jnp.logical_and(i == 0, c == 0)
    last = jnp.logical_and(i == pl.num_programs(0) - 1, c == pl.num_programs(1) - 1)

    def x_copy(it, j):
        src0 = pl.multiple_of(ir_ref[it] + j * ROW_BLK, ROW_BLK)
        return pltpu.make_async_copy(xs_ref.at[pl.ds(src0, ROW_BLK)],
                                     xbuf.at[pl.ds(pl.multiple_of(j * ROW_BLK, ROW_BLK), ROW_BLK)], xsem)

    def x_start(it):
        def start(j, carry):
            x_copy(it, j).start()
            return carry
        lax.fori_loop(0, inb_ref[it], start, 0)

    def x_wait(it):
        def wait(j, carry):
            x_copy(it, j).wait()
            return carry
        lax.fori_loop(0, inb_ref[it], wait, 0)

    def tail_copy(b, col):
        return pltpu.make_async_copy(
            zbuf, ys_ref.at[pl.ds(pl.multiple_of(b * ROW_BLK, ROW_BLK), ROW_BLK), pl.ds(col * MOE_CW, MOE_CW)], zsem)

    n_col = ys_ref.shape[1] // MOE_CW

    @pl.when(first)
    def _():
        x_start(0)
        zbuf[...] = jnp.zeros_like(zbuf)

        def start(b, carry):
            for col in range(n_col):
                tail_copy(b, col).start()
            return carry
        lax.fori_loop(used_blk, n_blk, start, 0)

    @pl.when(last)
    def _():
        def wait(b, carry):
            for col in range(n_col):
                tail_copy(b, col).wait()
            return carry
        lax.fori_loop(used_blk, n_blk, wait, 0)

    @pl.when(jnp.logical_and(valid, c == 0))
    def _():
        x_wait(i)

    @pl.when(jnp.logical_and(c == nc, i + 1 < n_items))
    def _():
        x_start(i + 1)

    @pl.when(jnp.logical_and(valid, c < nc))
    def _():
        bias = []
        for s in range(MOE_CW // LANES):
            cols = slice(s * LANES, (s + 1) * LANES)
            wgu[:, 2 * s * LANES:(2 * s + 1) * LANES] = wg_ref[:, cols].astype(BF16)
            wgu[:, (2 * s + 1) * LANES:(2 * s + 2) * LANES] = wu_ref[:, cols].astype(BF16)
            bias += [bg_ref[:, cols], bu_ref[:, cols]]
        bias = jnp.concatenate(bias, axis=1)

        def piece(r0, rows):
            x = xbuf[pl.ds(r0, rows), :].astype(BF16)
            gu = jnp.dot(x, wgu[...], preferred_element_type=F32) + bias
            for s in range(MOE_CW // LANES):
                gate = jnp.minimum(gu[:, 2 * s * LANES:(2 * s + 1) * LANES], SWIGLU_LIMIT)
                up = jnp.clip(gu[:, (2 * s + 1) * LANES:(2 * s + 2) * LANES], -SWIGLU_LIMIT, SWIGLU_LIMIT)
                a = (up + 1.0) * (gate * jax.nn.sigmoid(SWIGLU_ALPHA * gate))
                act[c, pl.ds(r0, rows), s * LANES:(s + 1) * LANES] = a.astype(BF16)

        def pair(p, carry):
            r0 = pl.multiple_of(p * 2 * MOE_RB, ROW_BLK)
            piece(r0, MOE_RB)
            piece(pl.multiple_of(r0 + MOE_RB, ROW_BLK), MOE_RB)
            return carry

        lax.fori_loop(0, n_pair, pair, 0)

        @pl.when(has_rb)
        def _():
            piece(base_rb, MOE_RB)

        @pl.when(has_blk)
        def _():
            piece(base_blk, ROW_BLK)

    @pl.when(jnp.logical_and(valid, c >= nc))
    def _():
        n = c - nc
        col0 = pl.multiple_of(n * MOE_CW, MOE_CW)
        wdb[...] = wd_ref[...].astype(BF16)

        def y_copy(r0, rows, slot):
            dst_row = pl.multiple_of(row0 + r0, ROW_BLK)
            return pltpu.make_async_copy(ystage.at[slot, pl.ds(0, rows)],
                                         ys_ref.at[pl.ds(dst_row, rows), pl.ds(col0, MOE_CW)], ysem.at[slot])

        def piece(r0, rows, slot):
            a = jnp.concatenate([act[cc, pl.ds(r0, rows), :] for cc in range(nc)], axis=1)
            ystage[slot, 0:rows, :] = jnp.dot(a, wdb[...], preferred_element_type=F32) + bd_ref[...]
            y_copy(r0, rows, slot).start()

        def pair(p, carry):
            r0 = pl.multiple_of(p * 2 * MOE_RB, ROW_BLK)

            @pl.when(p > 0)
            def _():
                y_copy(r0, MOE_RB, 0).wait()
                y_copy(r0, MOE_RB, 1).wait()

            piece(r0, MOE_RB, 0)
            piece(pl.multiple_of(r0 + MOE_RB, ROW_BLK), MOE_RB, 1)
            return carry

        lax.fori_loop(0, n_pair, pair, 0)

        @pl.when(n_pair > 0)
        def _():
            y_copy(0, MOE_RB, 0).wait()
            y_copy(0, MOE_RB, 1).wait()

        @pl.when(has_rb)
        def _():
            piece(base_rb, MOE_RB, 0)

        @pl.when(has_blk)
        def _():
            piece(base_blk, ROW_BLK, 1)

        @pl.when(has_rb)
        def _():
            y_copy(base_rb, MOE_RB, 0).wait()

        @pl.when(has_blk)
        def _():
            y_copy(base_blk, ROW_BLK, 1).wait()


def _moe(xs, item_e, item_row0, item_nblk, meta, w_gate_up, b_gate_up, w_down, b_down):
    n_rows, d = xs.shape
    n_exp, _, ff2 = w_gate_up.shape
    ff = ff2 // 2
    assert w_gate_up.shape[1] == d and w_down.shape[1:] == (ff, d) and ff == d
    nc = ff // MOE_CW
    ni = item_e.shape[0]
    rbuf = MOE_MAX_BLK * ROW_BLK

    def item(i, n):
        return jnp.minimum(i, n[0] - 1)

    def c1(i, c, n):
        return jnp.where(i < n[0], jnp.minimum(c, nc - 1), nc - 1)

    def c2(i, c, n):
        return jnp.where(i < n[0], jnp.maximum(c - nc, 0), nc - 1)

    bgu3 = b_gate_up.reshape(n_exp, 1, ff2)
    bd3 = b_down.reshape(n_exp, 1, d)
    in_specs = [
        pl.BlockSpec(memory_space=pl.ANY),
        pl.BlockSpec((None, d, MOE_CW), lambda i, c, e, r, b, n: (e[item(i, n)], 0, c1(i, c, n))),
        pl.BlockSpec((None, d, MOE_CW), lambda i, c, e, r, b, n: (e[item(i, n)], 0, nc + c1(i, c, n))),
        pl.BlockSpec((None, 1, MOE_CW), lambda i, c, e, r, b, n: (e[item(i, n)], 0, c1(i, c, n))),
        pl.BlockSpec((None, 1, MOE_CW), lambda i, c, e, r, b, n: (e[item(i, n)], 0, nc + c1(i, c, n))),
        pl.BlockSpec((None, ff, MOE_CW), lambda i, c, e, r, b, n: (e[item(i, n)], 0, c2(i, c, n))),
        pl.BlockSpec((None, 1, MOE_CW), lambda i, c, e, r, b, n: (e[item(i, n)], 0, c2(i, c, n))),
    ]
    grid_spec = pltpu.PrefetchScalarGridSpec(
        num_scalar_prefetch=4,
        grid=(ni, 2 * nc),
        in_specs=in_specs,
        out_specs=pl.BlockSpec(memory_space=pl.ANY),
        scratch_shapes=[pltpu.VMEM((rbuf, d), F32),
                        pltpu.VMEM((nc, rbuf, MOE_CW), BF16),
                        pltpu.VMEM((d, 2 * MOE_CW), BF16),
                        pltpu.VMEM((ff, MOE_CW), BF16),
                        pltpu.VMEM((2, MOE_RB, MOE_CW), F32),
                        pltpu.VMEM((ROW_BLK, MOE_CW), F32),
                        pltpu.SemaphoreType.DMA(()),
                        pltpu.SemaphoreType.DMA((2,)),
                        pltpu.SemaphoreType.DMA(())],
    )
    need = (rbuf * d * 4 + rbuf * ff * 2 + d * 2 * MOE_CW * 2 + ff * MOE_CW * 2 + 2 * MOE_RB * MOE_CW * 4
            + ROW_BLK * MOE_CW * 4 + 2 * 3 * d * MOE_CW * 4)
    return pl.pallas_call(
        functools.partial(_moe_kernel, nc=nc, n_blk=n_rows // ROW_BLK),
        grid_spec=grid_spec,
        out_shape=jax.ShapeDtypeStruct((n_rows, d), F32),
        compiler_params=_params(("arbitrary", "arbitrary"), need),
        name="moe_experts",
    )(item_e, item_row0, item_nblk, meta, xs, w_gate_up, w_gate_up, bgu3, bgu3, w_down, bd3)


def _combine_kernel(pos_ref, posn_ref, ys_ref, gate_ref, x1_ref, g2_ref, lng_ref, lnb_ref, o_ref, gbuf, sem, *, tc):
    i = pl.program_id(0)
    slot = i % 2

    def issue(p_ref, s):
        def body(t2, carry):
            for tt in range(DISPATCH_UNROLL):
                t = t2 * DISPATCH_UNROLL + tt
                for k in range(TOP_K):
                    p = p_ref[0, 0, t * TOP_K + k]
                    pltpu.make_async_copy(ys_ref.at[pl.ds(p, 1)], gbuf.at[s, k, pl.ds(t, 1)], sem.at[s]).start()
            return carry
        lax.fori_loop(0, tc // DISPATCH_UNROLL, body, 0)

    @pl.when(i == 0)
    def _():
        issue(pos_ref, 0)

    @pl.when(i + 1 < pl.num_programs(0))
    def _():
        issue(posn_ref, 1 - slot)

    for k in range(TOP_K):
        pltpu.make_async_copy(ys_ref.at[pl.ds(0, tc)], gbuf.at[slot, k], sem.at[slot]).wait()
    gt = gate_ref[...]
    ffn = gt[:, 0:1] * gbuf[slot, 0]
    for k in range(1, TOP_K):
        ffn = ffn + gt[:, k:k + 1] * gbuf[slot, k]
    o_ref[...] = _layer_norm(DN_ALPHA * x1_ref[...] + g2_ref[...] * ffn, lng_ref[...], lnb_ref[...])


def _combine(pos, ys, gates, x1, g2, ln_g, ln_b):
    b, l, d = x1.shape
    t = b * l
    tc = min(l, 128)
    per_b = l // tc
    n_steps = t // tc
    pos3 = pos.reshape(n_steps, 1, tc * TOP_K)
    vec = pl.BlockSpec((1, d), lambda i: (0, 0))
    out = pl.pallas_call(
        functools.partial(_combine_kernel, tc=tc),
        grid=(n_steps,),
        in_specs=[pl.BlockSpec((1, 1, tc * TOP_K), lambda i: (i, 0, 0), memory_space=pltpu.SMEM),
                  pl.BlockSpec((1, 1, tc * TOP_K), lambda i: (jnp.minimum(i + 1, n_steps - 1), 0, 0),
                               memory_space=pltpu.SMEM),
                  pl.BlockSpec(memory_space=pl.ANY),
                  pl.BlockSpec((tc, TOP_K), lambda i: (i, 0)),
                  pl.BlockSpec((tc, d), lambda i: (i, 0)),
                  pl.BlockSpec((None, 1, d), lambda i: (i // per_b, 0, 0)),
                  vec, vec],
        out_specs=pl.BlockSpec((tc, d), lambda i: (i, 0)),
        out_shape=jax.ShapeDtypeStruct((t, d), F32),
        scratch_shapes=[pltpu.VMEM((2, TOP_K, tc, d), F32), pltpu.SemaphoreType.DMA((2,))],
        compiler_params=_params(("arbitrary",), (2 * TOP_K + 2 * 2) * tc * d * 4),
        name="moe_combine",
    )(pos3, pos3, ys, gates, x1.reshape(t, d), g2, ln_g, ln_b)
    return out.reshape(b, l, d)


def _moe_items(counts, n_blk_total):
    n_exp = counts.shape[0]
    nblk = (counts + ROW_BLK - 1) // ROW_BLK
    blk_end = jnp.cumsum(nblk)
    blk0 = blk_end - nblk
    n_it = (nblk + MOE_MAX_BLK - 1) // MOE_MAX_BLK
    it_end = jnp.cumsum(n_it)
    it0 = it_end - n_it
    ni = n_exp + n_blk_total // MOE_MAX_BLK
    j = jnp.arange(ni, dtype=I32)
    e = jnp.minimum(jnp.sum(j[:, None] >= it_end[None, :], axis=1), n_exp - 1).astype(I32)
    local = j - it0[e]
    item_nblk = jnp.clip(nblk[e] - local * MOE_MAX_BLK, 0, MOE_MAX_BLK).astype(I32)
    item_row0 = ((blk0[e] + local * MOE_MAX_BLK) * ROW_BLK).astype(I32)
    n_items = it_end[-1].astype(I32)
    item_nblk = jnp.where(j < n_items, item_nblk, 0)
    item_row0 = jnp.where(j < n_items, item_row0, 0)
    row_start = (blk0 * ROW_BLK).astype(F32)
    used_blk = blk_end[-1].astype(I32)
    meta = jnp.stack([n_items, used_blk])
    blk = jnp.arange(n_blk_total, dtype=I32)
    partial = jnp.any((blk[:, None] == (blk_end - 1)[None, :]) & ((counts % ROW_BLK) != 0)[None, :], axis=1)
    fill = jnp.logical_or(partial, blk >= used_blk).astype(I32)
    return e, item_row0, item_nblk, meta, row_start, fill


def _mixer_stream(x, mod, hist, h0, pos0, w):
    b, l, d = x.shape
    sh1, sc1, g1, sh2, sc2, g2 = [m.reshape(b, 1, d) for m in jnp.split(mod, 6, axis=-1)]
    u_pool, u_ssm = _inproj(x, sc1, sh1, w["w_in"])
    hist16 = jnp.concatenate([jnp.zeros((b, HIST_ROWS - hist.shape[1], hist.shape[2]), F32), hist.astype(F32)], axis=1)
    y_pool, new_hist = _pool(u_pool, hist16, w["w_pool"], w["pool_scale"], pos0)
    y_ssm, h_new = _ssm(u_ssm, w["ssm"], h0)
    x1, hp, idx, gates, cnt = _outproj(y_pool, y_ssm, x, g1, sc2, sh2, w["w_out"], w["ln1_g"], w["ln1_b"],
                                       w["w_router"], w["b_router"], w["n_experts"])
    s_re, s_im = _slabs_to_state(h_new, SSM_STATE)
    return dict(x1=x1, hp=hp, idx=idx, gates=gates, cnt=cnt, g2=g2,
                hist=new_hist[:, 1:, :], s_re=s_re, s_im=s_im)


def kernel(x_prompt, x_sample, cache_pool, state_ssm_re, state_ssm_im, c_prompt, c_sample, w_ada, b_ada, w_in, w_pool, pool_scale, lambda_re, lambda_im, log_dt, ssm_b_re, ssm_b_im, ssm_c_re, ssm_c_im, d_skip, w_glu, b_glu, w_out, ln1_g, ln1_b, w_router, b_router, w_gate_up, b_gate_up, w_down, b_down, ln2_g, ln2_b):
    assert w_ada.shape[0] == DEPTH
    bp, lp, d = x_prompt.shape
    bs, ls, _ = x_sample.shape
    assert bp == SUBLANES and bs == SUBLANES, "the S5 kernel puts the 8 streams on sublanes"
    n_exp = w_router.shape[-1]
    dp = w_pool.shape[1] * w_pool.shape[2]

    w = dict(
        w_in=w_in[0].astype(BF16),
        w_pool=w_pool[0].astype(BF16),
        pool_scale=pool_scale[0].astype(F32).reshape(1, dp),
        ssm=_ssm_params(lambda_re[0], lambda_im[0], log_dt[0], ssm_b_re[0], ssm_b_im[0], ssm_c_re[0], ssm_c_im[0],
                        d_skip[0], w_glu[0], b_glu[0], SUBLANES),
        w_out=w_out[0].astype(BF16),
        ln1_g=ln1_g[0].reshape(1, d), ln1_b=ln1_b[0].reshape(1, d),
        w_router=jnp.pad(w_router[0], ((0, 0), (0, LANES - n_exp))).astype(BF16),
        b_router=jnp.pad(b_router[0].astype(F32), (0, LANES - n_exp)).reshape(1, LANES),
        n_experts=n_exp,
    )

    mod = _ada(jnp.concatenate([c_prompt, c_sample], axis=0), w_ada[0], b_ada[0].reshape(1, -1))
    n_slab = lambda_re.shape[1]
    p = _mixer_stream(x_prompt, mod[:bp], jnp.zeros((bp, HIST_ROWS - 1, dp), F32),
                      jnp.zeros((n_slab, bp, LANES), F32), 0, w)
    s = _mixer_stream(x_sample, mod[bp:], cache_pool[0], _state_to_slabs(state_ssm_re[0], state_ssm_im[0]),
                      PAST_LEN, w)

    tp, ts = bp * lp, bs * ls
    n_assign = (tp + ts) * TOP_K
    n_blk_total = -(-n_assign // ROW_BLK) + n_exp
    n_rows = n_blk_total * ROW_BLK
    counts = (p["cnt"][0, :n_exp] + s["cnt"][0, :n_exp]).astype(I32)
    item_e, item_row0, item_nblk, meta, row_start, fill = _moe_items(counts, n_blk_total)
    start_pad = jnp.zeros((SUBLANES, LANES), F32).at[0, :n_exp].set(row_start)
    pos = _plan(jnp.concatenate([p["idx"], s["idx"]], axis=0), start_pad)
    pos_p, pos_s = pos[:tp], pos[tp:]
    xs = _dispatch(pos, p["hp"], s["hp"], fill, n_rows)
    ys = _moe(xs, item_e, item_row0, item_nblk, meta, w_gate_up[0], b_gate_up[0], w_down[0], b_down[0])
    g2l, b2l = ln2_g[0].reshape(1, d), ln2_b[0].reshape(1, d)
    y_p = _combine(pos_p, ys, p["gates"], p["x1"], p["g2"], g2l, b2l)
    y_s = _combine(pos_s, ys, s["gates"], s["x1"], s["g2"], g2l, b2l)

    return (y_p, y_s, p["hist"][None], p["s_re"][None], p["s_im"][None],
            s["hist"][None], s["s_re"][None], s["s_im"][None])
```

```python
import functools
import math

import jax
import jax.numpy as jnp
from jax import lax
from jax.experimental import pallas as pl
from jax.experimental.pallas import tpu as pltpu

F32 = jnp.float32
BF16 = jnp.bfloat16
I32 = jnp.int32
U32 = jnp.uint32

POOL_WINDOWS = (2, 4, 8, 16)
HIST_ROWS = 16
SSM_CH = 16
SSM_STATE = 64
TOP_K = 4
SWIGLU_LIMIT = 7.0
SWIGLU_ALPHA = 1.702
LN_EPS = 1e-5
DEPTH = 1
DN_ALPHA = (2 * DEPTH) ** 0.25
PAST_LEN = 2048

LANES = 128
SUBLANES = 8
MXU_DIM = 256
V7X_VMEM_BYTES = 64 * 1024 * 1024
MIB = 1024 * 1024

ROW_BLK = 256
MOE_RB = 512
MOE_CW = 256
MOE_MAX_BLK = 10


def _params(sem, need_bytes):
    limit = min(need_bytes + 16 * MIB, V7X_VMEM_BYTES - 4 * MIB)
    return pltpu.CompilerParams(dimension_semantics=sem, vmem_limit_bytes=limit)


def _layer_norm(v, g, b):
    mu = jnp.mean(v, axis=-1, keepdims=True)
    vc = v - mu
    var = jnp.mean(vc * vc, axis=-1, keepdims=True)
    return vc * lax.rsqrt(var + LN_EPS) * g + b


def _ada_kernel(c_ref, w_ref, b_ref, o_ref):
    c = c_ref[...]
    s = c * jax.nn.sigmoid(c)
    o_ref[...] = jnp.dot(s.astype(BF16), w_ref[...].astype(BF16), preferred_element_type=F32) + b_ref[...]


def _ada(c_all, w_ada, b_ada):
    r, d = c_all.shape
    n = w_ada.shape[1]
    tn = 1536
    return pl.pallas_call(
        _ada_kernel,
        grid=(n // tn,),
        in_specs=[pl.BlockSpec((r, d), lambda j: (0, 0)),
                  pl.BlockSpec((d, tn), lambda j: (0, j)),
                  pl.BlockSpec((1, tn), lambda j: (0, j))],
        out_specs=pl.BlockSpec((r, tn), lambda j: (0, j)),
        out_shape=jax.ShapeDtypeStruct((r, n), F32),
        compiler_params=_params(("arbitrary",), 2 * (d * tn * 4 + r * (d + tn) * 4)),
        name="ada_mod",
    )(c_all, w_ada, b_ada)


def _inproj_kernel(x_ref, sc_ref, sh_ref, w_ref, up_ref, us_ref):
    h = x_ref[...] * (1.0 + sc_ref[...]) + sh_ref[...]
    u = jnp.dot(h.astype(BF16), w_ref[...], preferred_element_type=F32)
    dp = up_ref.shape[-1]
    up_ref[...] = u[:, :dp]
    us_ref[...] = u[:, dp:]


def _inproj(x, sc, sh, w_bf):
    b, l, d = x.shape
    dm = w_bf.shape[1]
    dp = dm // 2
    tm = min(l, 512)
    row = pl.BlockSpec((None, tm, d), lambda bi, i: (bi, i, 0))
    mod = pl.BlockSpec((None, 1, d), lambda bi, i: (bi, 0, 0))
    half = pl.BlockSpec((None, tm, dp), lambda bi, i: (bi, i, 0))
    return pl.pallas_call(
        _inproj_kernel,
        grid=(b, l // tm),
        in_specs=[row, mod, mod, pl.BlockSpec((d, dm), lambda bi, i: (0, 0))],
        out_specs=[half, half],
        out_shape=[jax.ShapeDtypeStruct((b, l, dp), F32), jax.ShapeDtypeStruct((b, l, dm - dp), F32)],
        compiler_params=_params(("arbitrary", "arbitrary"), 2 * (d * dm * 2 + tm * (d + dm) * 4)),
        name="in_proj",
    )(x, sc, sh, w_bf)


def _pool_kernel(u_ref, hist_ref, w_ref, scale_ref, y_ref, nh_ref, hbuf, *, tt, pos0):
    i = pl.program_id(1)

    @pl.when(i == 0)
    def _():
        hbuf[...] = hist_ref[...]

    u = u_ref[...]
    hb = hbuf[...]
    u16 = u.astype(BF16)
    h16 = hb.astype(BF16)
    pc = u.shape[1] // len(POOL_WINDOWS)
    d_main = lax.broadcasted_iota(I32, (tt, tt), 0) - lax.broadcasted_iota(I32, (tt, tt), 1)
    d_hist = (lax.broadcasted_iota(I32, (tt, HIST_ROWS), 0) + HIST_ROWS
              - lax.broadcasted_iota(I32, (tt, HIST_ROWS), 1))
    pos = (pos0 + i * tt + lax.broadcasted_iota(I32, (tt, 1), 0)).astype(F32)
    for g, w in enumerate(POOL_WINDOWS):
        sl = slice(g * pc, (g + 1) * pc)
        bm = jnp.where(d_main >= 0, jnp.where(d_main < w, 1.0, 0.0), 0.0).astype(BF16)
        bh = jnp.where(d_hist < w, 1.0, 0.0).astype(BF16)
        s = (jnp.dot(bm, u16[:, sl], preferred_element_type=F32)
             + jnp.dot(bh, h16[:, sl], preferred_element_type=F32))
        cnt = jnp.minimum(pos + 1.0, float(w))
        diff = s / cnt - u[:, sl]
        y = jnp.dot(diff.astype(BF16), w_ref[g], preferred_element_type=F32) * scale_ref[:, sl]
        y_ref[:, sl] = y.astype(y_ref.dtype)
    hbuf[...] = u[tt - HIST_ROWS:, :]

    @pl.when(i == pl.num_programs(1) - 1)
    def _():
        nh_ref[...] = hbuf[...]


def _pool(u_pool, hist16, w_pool_bf, pool_scale, pos0):
    b, l, dp = u_pool.shape
    tt = min(l, 256)
    g, pc, _ = w_pool_bf.shape
    kern = functools.partial(_pool_kernel, tt=tt, pos0=pos0)
    return pl.pallas_call(
        kern,
        grid=(b, l // tt),
        in_specs=[pl.BlockSpec((None, tt, dp), lambda bi, i: (bi, i, 0)),
                  pl.BlockSpec((None, HIST_ROWS, dp), lambda bi, i: (bi, 0, 0)),
                  pl.BlockSpec((g, pc, pc), lambda bi, i: (0, 0, 0)),
                  pl.BlockSpec((1, dp), lambda bi, i: (0, 0))],
        out_specs=[pl.BlockSpec((None, tt, dp), lambda bi, i: (bi, i, 0)),
                   pl.BlockSpec((None, HIST_ROWS, dp), lambda bi, i: (bi, 0, 0))],
        out_shape=[jax.ShapeDtypeStruct((b, l, dp), BF16), jax.ShapeDtypeStruct((b, HIST_ROWS, dp), F32)],
        scratch_shapes=[pltpu.VMEM((HIST_ROWS, dp), F32)],
        compiler_params=_params(("arbitrary", "arbitrary"), 2 * (tt * dp * 6 + g * pc * pc * 2 + 2 * HIST_ROWS * dp * 4)),
        name="pool_mixer",
    )(u_pool, hist16, w_pool_bf, pool_scale)


SSM_PAIR_GROUP = 8


def _ssm_kernel(u_ref, bc_ref, cc_ref, wg_ref, ar_ref, ai_ref, dsk_ref, bgl_ref, h0_ref,
                y_ref, hn_ref, s_ref, ubuf, hst, *, tt, pitch, nb):
    i = pl.program_id(0)
    n_slab = s_ref.shape[0]
    n_chunk = bc_ref.shape[0]
    slab_per_chunk = n_slab // n_chunk
    cw = bc_ref.shape[1]

    @pl.when(i == 0)
    def _():
        s_ref[...] = jnp.zeros_like(s_ref)
        ubuf[...] = jnp.zeros_like(ubuf)
        hst[...] = h0_ref[...]

    u = u_ref[...].reshape(nb * tt, u_ref.shape[-1])
    u16 = u.astype(BF16)
    for k in range(n_chunk):
        bu = jnp.dot(u16[:, k * cw:(k + 1) * cw], bc_ref[k], preferred_element_type=F32)
        for j in range(slab_per_chunk):
            for b in range(nb):
                s_ref[k * slab_per_chunk + j, b * pitch:b * pitch + tt, :] = (
                    bu[b * tt:(b + 1) * tt, j * LANES:(j + 1) * LANES])
    for b in range(nb):
        ubuf[b * pitch:b * pitch + tt, :] = u[b * tt:(b + 1) * tt, :]

    n_pair = n_slab // 2
    for pg in range(n_pair // SSM_PAIR_GROUP):
        qs = [pg * SSM_PAIR_GROUP + j for j in range(SSM_PAIR_GROUP)]
        ars = [ar_ref[q] for q in qs]
        ais = [ai_ref[q] for q in qs]

        def body(t, carry, qs=qs, ars=ars, ais=ais):
            hr, hi = carry
            nhr, nhi = [], []
            rows = pl.ds(t, nb, stride=pitch)
            for j, q in enumerate(qs):
                bur = s_ref[2 * q, rows, :]
                bui = s_ref[2 * q + 1, rows, :]
                r_ = ars[j] * hr[j] - ais[j] * hi[j] + bur
                i_ = ars[j] * hi[j] + ais[j] * hr[j] + bui
                s_ref[2 * q, rows, :] = r_
                s_ref[2 * q + 1, rows, :] = i_
                nhr.append(r_)
                nhi.append(i_)
            return tuple(nhr), tuple(nhi)

        init = (tuple(hst[2 * q] for q in qs), tuple(hst[2 * q + 1] for q in qs))
        hr, hi = lax.fori_loop(0, tt, body, init)
        for j, q in enumerate(qs):
            hst[2 * q] = hr[j]
            hst[2 * q + 1] = hi[j]

    ycs = []
    for n in range(n_chunk):
        hc = jnp.concatenate([s_ref[n * slab_per_chunk + j] for j in range(slab_per_chunk)], axis=1)
        ycs.append(jnp.dot(hc.astype(BF16), cc_ref[n], preferred_element_type=F32))
    y = jnp.concatenate(ycs, axis=1) + dsk_ref[...] * ubuf[...]
    y = jax.nn.gelu(y)
    y16 = y.astype(BF16)
    z = jnp.concatenate([jnp.dot(y16[:, k * cw:(k + 1) * cw], wg_ref[k], preferred_element_type=F32)
                         for k in range(n_chunk)], axis=1) + bgl_ref[...]
    out = y * jax.nn.sigmoid(z)
    for b in range(nb):
        y_ref[b] = out[b * pitch:b * pitch + tt].astype(y_ref.dtype)

    @pl.when(i == pl.num_programs(0) - 1)
    def _():
        hn_ref[...] = hst[...]


def _ssm(u_ssm, prm, h0):
    b, l, ds = u_ssm.shape
    tt = min(l, 64)
    pitch = tt + 4
    n_slab = h0.shape[0]
    n_chunk = prm["bc"].shape[0]
    kern = functools.partial(_ssm_kernel, tt=tt, pitch=pitch, nb=b)

    def const(a):
        nd = a.ndim
        return pl.BlockSpec(a.shape, lambda i, nd=nd: (0,) * nd)

    args = (u_ssm, prm["bc"], prm["cc"], prm["wg"], prm["ar"], prm["ai"], prm["dsk"], prm["bgl"], h0)
    return pl.pallas_call(
        kern,
        grid=(l // tt,),
        in_specs=[pl.BlockSpec((b, tt, ds), lambda i: (0, i, 0))] + [const(a) for a in args[1:]],
        out_specs=[pl.BlockSpec((b, tt, ds), lambda i: (0, i, 0)),
                   pl.BlockSpec((n_slab, b, LANES), lambda i: (0, 0, 0))],
        out_shape=[jax.ShapeDtypeStruct((b, l, ds), BF16), jax.ShapeDtypeStruct((n_slab, b, LANES), F32)],
        scratch_shapes=[pltpu.VMEM((n_slab, b * pitch, LANES), F32),
                        pltpu.VMEM((b * pitch, ds), F32),
                        pltpu.VMEM((n_slab, b, LANES), F32)],
        compiler_params=_params(("arbitrary",), (n_slab * LANES + ds) * b * pitch * 4 + 2 * b * tt * ds * 6
                                + 2 * sum(a.size * a.dtype.itemsize for a in args[1:])),
        name="ssm_mixer",
    )(*args)


def _ssm_params(lambda_re, lambda_im, log_dt, b_re, b_im, c_re, c_im, d_skip, w_glu, b_glu, nb):
    g, p = lambda_re.shape
    ch = b_re.shape[-1]
    gpc = MXU_DIM // ch
    n_chunk = g // gpc
    ppc = gpc // 2
    lr, li = lambda_re.astype(F32), lambda_im.astype(F32)
    dt = jnp.exp(log_dt.astype(F32))[:, None]
    mag = jnp.exp(lr * dt)
    abar_re, abar_im = mag * jnp.cos(li * dt), mag * jnp.sin(li * dt)
    nr, ni = abar_re - 1.0, abar_im
    den = lr * lr + li * li
    k_re = (nr * lr + ni * li) / den
    k_im = (ni * lr - nr * li) / den
    br, bi = b_re.astype(F32), b_im.astype(F32)
    bb_re = k_re[..., None] * br - k_im[..., None] * bi
    bb_im = k_re[..., None] * bi + k_im[..., None] * br
    eye_p = jnp.eye(ppc, dtype=F32)
    eye_2 = jnp.eye(2, dtype=F32)
    bb = jnp.stack([bb_re, bb_im]).reshape(2, n_chunk, ppc, 2, p, ch)
    bc = jnp.einsum("rkqgpc,qx,gy->kqgcxryp", bb, eye_p, eye_2).reshape(n_chunk, gpc * ch, ppc * 2 * 2 * p)
    cm = jnp.stack([c_re.astype(F32), -c_im.astype(F32)]).reshape(2, n_chunk, ppc, 2, ch, p)
    cc = jnp.einsum("rkqgcp,qx,gy->kqrgpxyc", cm, eye_p, eye_2).reshape(n_chunk, ppc * 2 * 2 * p, gpc * ch)
    wg = jnp.einsum("kgce,gx->kgcxe", w_glu.astype(F32).reshape(n_chunk, gpc, ch, ch),
                    jnp.eye(gpc, dtype=F32)).reshape(n_chunk, gpc * ch, gpc * ch)
    n_pair = g // 2
    ar = jnp.broadcast_to(abar_re.reshape(n_pair, 1, 2 * p), (n_pair, nb, 2 * p))
    ai = jnp.broadcast_to(abar_im.reshape(n_pair, 1, 2 * p), (n_pair, nb, 2 * p))
    return dict(bc=bc.astype(BF16), cc=cc.astype(BF16), wg=wg.astype(BF16), ar=ar, ai=ai,
                dsk=d_skip.astype(F32).reshape(1, g * ch), bgl=b_glu.astype(F32).reshape(1, g * ch))


def _state_to_slabs(s_re, s_im):
    b, g, p = s_re.shape
    re = s_re.astype(F32).reshape(b, g // 2, 2 * p).transpose(1, 0, 2)
    im = s_im.astype(F32).reshape(b, g // 2, 2 * p).transpose(1, 0, 2)
    return jnp.stack([re, im], axis=1).reshape(g, b, 2 * p)


def _slabs_to_state(h, p):
    n_slab, b, _ = h.shape
    h = h.reshape(n_slab // 2, 2, b, 2 * p)
    re = h[:, 0].transpose(1, 0, 2).reshape(b, n_slab, p)
    im = h[:, 1].transpose(1, 0, 2).reshape(b, n_slab, p)
    return re, im


def _outproj_kernel(yp_ref, ys_ref, x_ref, g1_ref, sc2_ref, sh2_ref, w_ref, lng_ref, lnb_ref, wr_ref, br_ref,
                    x1_ref, hp_ref, idx_ref, gate_ref, cnt_ref, *, n_experts):
    first = jnp.logical_and(pl.program_id(0) == 0, pl.program_id(1) == 0)
    ymix = jnp.concatenate([yp_ref[...], ys_ref[...]], axis=1)
    mix = jnp.dot(ymix, w_ref[...], preferred_element_type=F32)
    x1 = _layer_norm(DN_ALPHA * x_ref[...] + g1_ref[...] * mix, lng_ref[...], lnb_ref[...])
    x1_ref[...] = x1
    h2 = x1 * (1.0 + sc2_ref[...]) + sh2_ref[...]
    hp_ref[...] = h2
    logits = jnp.dot(h2.astype(BF16), wr_ref[...], preferred_element_type=F32) + br_ref[...]
    tm = logits.shape[0]
    lane = lax.broadcasted_iota(I32, (tm, LANES), 1).astype(F32)
    cur = jnp.where(lane < n_experts, logits, -jnp.inf)
    vals, onehot = [], jnp.zeros((tm, LANES), F32)
    for k in range(TOP_K):
        m = jnp.max(cur, axis=1, keepdims=True)
        sel = jnp.min(jnp.where(cur == m, lane, float(LANES)), axis=1, keepdims=True)
        hit = lane == sel
        idx_ref[:, k:k + 1] = sel.astype(I32)
        vals.append(m)
        onehot = onehot + jnp.where(hit, 1.0, 0.0)
        cur = jnp.where(hit, -jnp.inf, cur)
    es = [jnp.exp(v - vals[0]) for v in vals]
    den = es[0] + es[1] + es[2] + es[3]
    for k in range(TOP_K):
        gate_ref[:, k:k + 1] = es[k] / den

    @pl.when(first)
    def _():
        cnt_ref[...] = jnp.zeros_like(cnt_ref)

    cnt_ref[0:1, :] += jnp.sum(onehot, axis=0, keepdims=True)


def _outproj(y_pool, y_ssm, x, g1, sc2, sh2, w_out_bf, ln_g, ln_b, wr_bf, br_pad, n_experts):
    b, l, d = x.shape
    dp = y_pool.shape[-1]
    tm = min(l, 256)
    nl = l // tm
    t = b * l
    row = pl.BlockSpec((None, tm, d), lambda bi, i: (bi, i, 0))
    halfrow = pl.BlockSpec((None, tm, dp), lambda bi, i: (bi, i, 0))
    mod = pl.BlockSpec((None, 1, d), lambda bi, i: (bi, 0, 0))
    vec = pl.BlockSpec((1, d), lambda bi, i: (0, 0))
    tok4 = pl.BlockSpec((tm, TOP_K), lambda bi, i: (bi * nl + i, 0))
    kern = functools.partial(_outproj_kernel, n_experts=n_experts)
    return pl.pallas_call(
        kern,
        grid=(b, nl),
        in_specs=[halfrow, halfrow, row, mod, mod, mod,
                  pl.BlockSpec((d, d), lambda bi, i: (0, 0)), vec, vec,
                  pl.BlockSpec((d, LANES), lambda bi, i: (0, 0)),
                  pl.BlockSpec((1, LANES), lambda bi, i: (0, 0))],
        out_specs=[row,
                   pl.BlockSpec((tm, d), lambda bi, i: (bi * nl + i, 0)),
                   tok4, tok4,
                   pl.BlockSpec((SUBLANES, LANES), lambda bi, i: (0, 0))],
        out_shape=[jax.ShapeDtypeStruct((b, l, d), F32),
                   jax.ShapeDtypeStruct((t, d), F32),
                   jax.ShapeDtypeStruct((t, TOP_K), I32),
                   jax.ShapeDtypeStruct((t, TOP_K), F32),
                   jax.ShapeDtypeStruct((SUBLANES, LANES), F32)],
        compiler_params=_params(("arbitrary", "arbitrary"), 2 * (d * d * 2 + d * LANES * 2 + tm * d * 14)),
        name="out_proj_router",
    )(y_pool, y_ssm, x, g1, sc2, sh2, w_out_bf, ln_g, ln_b, wr_bf, br_pad)


def _plan_kernel(idx_ref, start_ref, pos_ref, carry):
    i = pl.program_id(0)

    @pl.when(i == 0)
    def _():
        carry[...] = jnp.zeros_like(carry)

    idx = idx_ref[...]
    tp = idx.shape[0]
    lane = lax.broadcasted_iota(I32, (tp, LANES), 1)
    hits = [lane == idx[:, k:k + 1] for k in range(TOP_K)]
    onehot = jnp.zeros((tp, LANES), F32)
    for h in hits:
        onehot = onehot + jnp.where(h, 1.0, 0.0)
    below = jnp.where(lax.broadcasted_iota(I32, (tp, tp), 0) > lax.broadcasted_iota(I32, (tp, tp), 1),
                      1.0, 0.0).astype(BF16)
    excl = jnp.dot(below, onehot.astype(BF16), preferred_element_type=F32)
    slot = start_ref[0:1, :] + carry[0:1, :] + excl
    for k in range(TOP_K):
        pos_ref[:, k:k + 1] = jnp.sum(jnp.where(hits[k], slot, 0.0), axis=1, keepdims=True).astype(I32)
    carry[0:1, :] += jnp.sum(onehot, axis=0, keepdims=True)


def _plan(idx_all, start_pad):
    t = idx_all.shape[0]
    tp = 128
    return pl.pallas_call(
        _plan_kernel,
        grid=(t // tp,),
        in_specs=[pl.BlockSpec((tp, TOP_K), lambda i: (i, 0)),
                  pl.BlockSpec((SUBLANES, LANES), lambda i: (0, 0))],
        out_specs=pl.BlockSpec((tp, TOP_K), lambda i: (i, 0)),
        out_shape=jax.ShapeDtypeStruct((t, TOP_K), I32),
        scratch_shapes=[pltpu.VMEM((SUBLANES, LANES), F32)],
        compiler_params=_params(("arbitrary",), 4 * tp * LANES * 4),
        name="moe_plan",
    )(idx_all, start_pad)


DISPATCH_UNROLL = 4


def _dispatch_kernel(fill_ref, pos_ref, hp_ref, hs_ref, xs_ref, zbuf, sem, zsem, *, td, n_p_steps, n_blk):
    i = pl.program_id(0)

    def fill_copy(b):
        return pltpu.make_async_copy(zbuf, xs_ref.at[pl.ds(pl.multiple_of(b * ROW_BLK, ROW_BLK), ROW_BLK)], zsem)

    @pl.when(i == 0)
    def _():
        zbuf[...] = jnp.zeros_like(zbuf)

        def start(b, carry):
            @pl.when(fill_ref[b] != 0)
            def _():
                fill_copy(b).start()
            return carry

        def wait(b, carry):
            @pl.when(fill_ref[b] != 0)
            def _():
                fill_copy(b).wait()
            return carry

        lax.fori_loop(0, n_blk, start, 0)
        lax.fori_loop(0, n_blk, wait, 0)

    def scatter(src_ref):
        def body(t2, carry):
            for tt in range(DISPATCH_UNROLL):
                t = t2 * DISPATCH_UNROLL + tt
                for k in range(TOP_K):
                    p = pos_ref[0, 0, t * TOP_K + k]
                    pltpu.make_async_copy(src_ref.at[pl.ds(t, 1)], xs_ref.at[pl.ds(p, 1)], sem).start()
            return carry

        lax.fori_loop(0, td // DISPATCH_UNROLL, body, 0)
        for k in range(TOP_K):
            pltpu.make_async_copy(src_ref, xs_ref.at[pl.ds(0, td)], sem).wait()

    @pl.when(i < n_p_steps)
    def _():
        scatter(hp_ref)

    @pl.when(i >= n_p_steps)
    def _():
        scatter(hs_ref)


def _dispatch(pos, h_prompt, h_sample, fill_flags, n_rows):
    tp, w = h_prompt.shape
    ts = h_sample.shape[0]
    td = math.gcd(math.gcd(tp, ts), 128)
    n_p_steps = tp // td
    n_steps = n_p_steps + ts // td
    n_blk = fill_flags.shape[0]
    pos3 = pos.reshape(n_steps, 1, td * TOP_K)
    grid_spec = pltpu.PrefetchScalarGridSpec(
        num_scalar_prefetch=1,
        grid=(n_steps,),
        in_specs=[pl.BlockSpec((1, 1, td * TOP_K), lambda i, f: (i, 0, 0), memory_space=pltpu.SMEM),
                  pl.BlockSpec((td, w), lambda i, f: (jnp.minimum(i, n_p_steps - 1), 0)),
                  pl.BlockSpec((td, w), lambda i, f: (jnp.maximum(i - n_p_steps, 0), 0))],
        out_specs=pl.BlockSpec(memory_space=pl.ANY),
        scratch_shapes=[pltpu.VMEM((ROW_BLK, w), F32),
                        pltpu.SemaphoreType.DMA(()), pltpu.SemaphoreType.DMA(())],
    )
    need = 2 * 2 * td * w * 4 + ROW_BLK * w * 4
    return pl.pallas_call(
        functools.partial(_dispatch_kernel, td=td, n_p_steps=n_p_steps, n_blk=n_blk),
        grid_spec=grid_spec,
        out_shape=jax.ShapeDtypeStruct((n_rows, w), F32),
        compiler_params=_params(("arbitrary",), need),
        name="moe_dispatch",
    )(fill_flags, pos3, h_prompt, h_sample)


def _moe_kernel(ie_ref, ir_ref, inb_ref, meta_ref, xs_ref, wg_ref, wu_ref, bg_ref, bu_ref, wd_ref, bd_ref,
                ys_ref, xbuf, act, ystage, zbuf, xsem, ysem, zsem, *, nc, n_blk):
    i = pl.program_id(0)
    c = pl.program_id(1)
    n_items = meta_ref[0]
    used_blk = meta_ref[1]
    valid = i < n_items
    nb = inb_ref[i]
    row0 = pl.multiple_of(ir_ref[i], ROW_BLK)
    blk_per_pair = 2 * MOE_RB // ROW_BLK
    n_pair = nb // blk_per_pair
    rem = nb - n_pair * blk_per_pair
    has_rb = rem >= MOE_RB // ROW_BLK
    has_blk = (rem % (MOE_RB // ROW_BLK)) != 0
    base_rb = pl.multiple_of(n_pair * 2 * MOE_RB, ROW_BLK)
    base_blk = pl.multiple_of(base_rb + jnp.where(has_rb, MOE_RB, 0), ROW_BLK)
    first = jnp.logical_and(i == 0, c == 0)
    last = jnp.logical_and(i == pl.num_programs(0) - 1, c == pl.num_programs(1) - 1)

    def x_copy(it, j):
        src0 = pl.multiple_of(ir_ref[it] + j * ROW_BLK, ROW_BLK)
        return pltpu.make_async_copy(xs_ref.at[pl.ds(src0, ROW_BLK)],
                                     xbuf.at[pl.ds(pl.multiple_of(j * ROW_BLK, ROW_BLK), ROW_BLK)], xsem)

    def x_start(it):
        def start(j, carry):
            x_copy(it, j).start()
            return carry
        lax.fori_loop(0, inb_ref[it], start, 0)

    def x_wait(it):
        def wait(j, carry):
            x_copy(it, j).wait()
            return carry
        lax.fori_loop(0, inb_ref[it], wait, 0)

    def tail_copy(b, col):
        return pltpu.make_async_copy(
            zbuf, ys_ref.at[pl.ds(pl.multiple_of(b * ROW_BLK, ROW_BLK), ROW_BLK), pl.ds(col * MOE_CW, MOE_CW)], zsem)

    n_col = ys_ref.shape[1] // MOE_CW

    @pl.when(first)
    def _():
        x_start(0)
        zbuf[...] = jnp.zeros_like(zbuf)

        def start(b, carry):
            for col in range(n_col):
                tail_copy(b, col).start()
            return carry
        lax.fori_loop(used_blk, n_blk, start, 0)

    @pl.when(last)
    def _():
        def wait(b, carry):
            for col in range(n_col):
                tail_copy(b, col).wait()
            return carry
        lax.fori_loop(used_blk, n_blk, wait, 0)

    @pl.when(jnp.logical_and(valid, c == 0))
    def _():
        x_wait(i)

    @pl.when(jnp.logical_and(c == nc, i + 1 < n_items))
    def _():
        x_start(i + 1)

    @pl.when(jnp.logical_and(valid, c < nc))
    def _():
        def piece(r0, rows):
            w, bias = [], []
            for s in range(MOE_CW // LANES):
                cols = slice(s * LANES, (s + 1) * LANES)
                w += [wg_ref[:, cols].astype(BF16), wu_ref[:, cols].astype(BF16)]
                bias += [bg_ref[:, cols], bu_ref[:, cols]]
            x = xbuf[pl.ds(r0, rows), :].astype(BF16)
            gu = (jnp.dot(x, jnp.concatenate(w, axis=1), preferred_element_type=F32)
                  + jnp.concatenate(bias, axis=1))
            for s in range(MOE_CW // LANES):
                gate = jnp.minimum(gu[:, 2 * s * LANES:(2 * s + 1) * LANES], SWIGLU_LIMIT)
                up = jnp.clip(gu[:, (2 * s + 1) * LANES:(2 * s + 2) * LANES], -SWIGLU_LIMIT, SWIGLU_LIMIT)
                a = (up + 1.0) * (gate * jax.nn.sigmoid(SWIGLU_ALPHA * gate))
                act[c, pl.ds(r0, rows), s * LANES:(s + 1) * LANES] = a.astype(BF16)

        def pair(p, carry):
            r0 = pl.multiple_of(p * 2 * MOE_RB, ROW_BLK)
            piece(r0, MOE_RB)
            piece(pl.multiple_of(r0 + MOE_RB, ROW_BLK), MOE_RB)
            return carry

        lax.fori_loop(0, n_pair, pair, 0)

        @pl.when(has_rb)
        def _():
            piece(base_rb, MOE_RB)

        @pl.when(has_blk)
        def _():
            piece(base_blk, ROW_BLK)

    @pl.when(jnp.logical_and(valid, c >= nc))
    def _():
        n = c - nc
        col0 = pl.multiple_of(n * MOE_CW, MOE_CW)

        def y_copy(r0, rows, slot):
            dst_row = pl.multiple_of(row0 + r0, ROW_BLK)
            return pltpu.make_async_copy(ystage.at[slot, pl.ds(0, rows)],
                                         ys_ref.at[pl.ds(dst_row, rows), pl.ds(col0, MOE_CW)], ysem.at[slot])

        def piece(r0, rows, slot):
            a = jnp.concatenate([act[cc, pl.ds(r0, rows), :] for cc in range(nc)], axis=1)
            ystage[slot, 0:rows, :] = (jnp.dot(a, wd_ref[...].astype(BF16), preferred_element_type=F32)
                                       + bd_ref[...])
            y_copy(r0, rows, slot).start()

        def pair(p, carry):
            r0 = pl.multiple_of(p * 2 * MOE_RB, ROW_BLK)

            @pl.when(p > 0)
            def _():
                y_copy(r0, MOE_RB, 0).wait()
                y_copy(r0, MOE_RB, 1).wait()

            piece(r0, MOE_RB, 0)
            piece(pl.multiple_of(r0 + MOE_RB, ROW_BLK), MOE_RB, 1)
            return carry

        lax.fori_loop(0, n_pair, pair, 0)

        @pl.when(n_pair > 0)
        def _():
            y_copy(0, MOE_RB, 0).wait()
            y_copy(0, MOE_RB, 1).wait()

        @pl.when(has_rb)
        def _():
            piece(base_rb, MOE_RB, 0)

        @pl.when(has_blk)
        def _():
            piece(base_blk, ROW_BLK, 1)

        @pl.when(has_rb)
        def _():
            y_copy(base_rb, MOE_RB, 0).wait()

        @pl.when(has_blk)
        def _():
            y_copy(base_blk, ROW_BLK, 1).wait()


def _moe(xs, item_e, item_row0, item_nblk, meta, w_gate_up, b_gate_up, w_down, b_down):
    n_rows, d = xs.shape
    n_exp, _, ff2 = w_gate_up.shape
    ff = ff2 // 2
    assert w_gate_up.shape[1] == d and w_down.shape[1:] == (ff, d) and ff == d
    nc = ff // MOE_CW
    ni = item_e.shape[0]
    rbuf = MOE_MAX_BLK * ROW_BLK

    def item(i, n):
        return jnp.minimum(i, n[0] - 1)

    def c1(i, c, n):
        return jnp.where(i < n[0], jnp.minimum(c, nc - 1), nc - 1)

    def c2(i, c, n):
        return jnp.where(i < n[0], jnp.maximum(c - nc, 0), nc - 1)

    bgu3 = b_gate_up.reshape(n_exp, 1, ff2)
    bd3 = b_down.reshape(n_exp, 1, d)
    in_specs = [
        pl.BlockSpec(memory_space=pl.ANY),
        pl.BlockSpec((None, d, MOE_CW), lambda i, c, e, r, b, n: (e[item(i, n)], 0, c1(i, c, n))),
        pl.BlockSpec((None, d, MOE_CW), lambda i, c, e, r, b, n: (e[item(i, n)], 0, nc + c1(i, c, n))),
        pl.BlockSpec((None, 1, MOE_CW), lambda i, c, e, r, b, n: (e[item(i, n)], 0, c1(i, c, n))),
        pl.BlockSpec((None, 1, MOE_CW), lambda i, c, e, r, b, n: (e[item(i, n)], 0, nc + c1(i, c, n))),
        pl.BlockSpec((None, ff, MOE_CW), lambda i, c, e, r, b, n: (e[item(i, n)], 0, c2(i, c, n))),
        pl.BlockSpec((None, 1, MOE_CW), lambda i, c, e, r, b, n: (e[item(i, n)], 0, c2(i, c, n))),
    ]
    grid_spec = pltpu.PrefetchScalarGridSpec(
        num_scalar_prefetch=4,
        grid=(ni, 2 * nc),
        in_specs=in_specs,
        out_specs=pl.BlockSpec(memory_space=pl.ANY),
        scratch_shapes=[pltpu.VMEM((rbuf, d), F32),
                        pltpu.VMEM((nc, rbuf, MOE_CW), BF16),
                        pltpu.VMEM((2, MOE_RB, MOE_CW), F32),
                        pltpu.VMEM((ROW_BLK, MOE_CW), F32),
                        pltpu.SemaphoreType.DMA(()),
                        pltpu.SemaphoreType.DMA((2,)),
                        pltpu.SemaphoreType.DMA(())],
    )
    need = (rbuf * d * 4 + rbuf * ff * 2 + 2 * MOE_RB * MOE_CW * 4 + ROW_BLK * MOE_CW * 4
            + 2 * 3 * d * MOE_CW * 4)
    return pl.pallas_call(
        functools.partial(_moe_kernel, nc=nc, n_blk=n_rows // ROW_BLK),
        grid_spec=grid_spec,
        out_shape=jax.ShapeDtypeStruct((n_rows, d), F32),
        compiler_params=_params(("arbitrary", "arbitrary"), need),
        name="moe_experts",
    )(item_e, item_row0, item_nblk, meta, xs, w_gate_up, w_gate_up, bgu3, bgu3, w_down, bd3)


def _combine_kernel(pos_ref, posn_ref, ys_ref, gate_ref, x1_ref, g2_ref, lng_ref, lnb_ref, o_ref, gbuf, sem, *, tc):
    i = pl.program_id(0)
    slot = i % 2

    def issue(p_ref, s):
        def body(t2, carry):
            for tt in range(DISPATCH_UNROLL):
                t = t2 * DISPATCH_UNROLL + tt
                for k in range(TOP_K):
                    p = p_ref[0, 0, t * TOP_K + k]
                    pltpu.make_async_copy(ys_ref.at[pl.ds(p, 1)], gbuf.at[s, k, pl.ds(t, 1)], sem.at[s]).start()
            return carry
        lax.fori_loop(0, tc // DISPATCH_UNROLL, body, 0)

    @pl.when(i == 0)
    def _():
        issue(pos_ref, 0)

    @pl.when(i + 1 < pl.num_programs(0))
    def _():
        issue(posn_ref, 1 - slot)

    for k in range(TOP_K):
        pltpu.make_async_copy(ys_ref.at[pl.ds(0, tc)], gbuf.at[slot, k], sem.at[slot]).wait()
    gt = gate_ref[...]
    ffn = gt[:, 0:1] * gbuf[slot, 0]
    for k in range(1, TOP_K):
        ffn = ffn + gt[:, k:k + 1] * gbuf[slot, k]
    o_ref[...] = _layer_norm(DN_ALPHA * x1_ref[...] + g2_ref[...] * ffn, lng_ref[...], lnb_ref[...])


def _combine(pos, ys, gates, x1, g2, ln_g, ln_b):
    b, l, d = x1.shape
    t = b * l
    tc = min(l, 128)
    per_b = l // tc
    n_steps = t // tc
    pos3 = pos.reshape(n_steps, 1, tc * TOP_K)
    vec = pl.BlockSpec((1, d), lambda i: (0, 0))
    out = pl.pallas_call(
        functools.partial(_combine_kernel, tc=tc),
        grid=(n_steps,),
        in_specs=[pl.BlockSpec((1, 1, tc * TOP_K), lambda i: (i, 0, 0), memory_space=pltpu.SMEM),
                  pl.BlockSpec((1, 1, tc * TOP_K), lambda i: (jnp.minimum(i + 1, n_steps - 1), 0, 0),
                               memory_space=pltpu.SMEM),
                  pl.BlockSpec(memory_space=pl.ANY),
                  pl.BlockSpec((tc, TOP_K), lambda i: (i, 0)),
                  pl.BlockSpec((tc, d), lambda i: (i, 0)),
                  pl.BlockSpec((None, 1, d), lambda i: (i // per_b, 0, 0)),
                  vec, vec],
        out_specs=pl.BlockSpec((tc, d), lambda i: (i, 0)),
        out_shape=jax.ShapeDtypeStruct((t, d), F32),
        scratch_shapes=[pltpu.VMEM((2, TOP_K, tc, d), F32), pltpu.SemaphoreType.DMA((2,))],
        compiler_params=_params(("arbitrary",), (2 * TOP_K + 2 * 2) * tc * d * 4),
        name="moe_combine",
    )(pos3, pos3, ys, gates, x1.reshape(t, d), g2, ln_g, ln_b)
    return out.reshape(b, l, d)


def _moe_items(counts, n_blk_total):
    n_exp = counts.shape[0]
    nblk = (counts + ROW_BLK - 1) // ROW_BLK
    blk_end = jnp.cumsum(nblk)
    blk0 = blk_end - nblk
    n_it = (nblk + MOE_MAX_BLK - 1) // MOE_MAX_BLK
    it_end = jnp.cumsum(n_it)
    it0 = it_end - n_it
    ni = n_exp + n_blk_total // MOE_MAX_BLK
    j = jnp.arange(ni, dtype=I32)
    e = jnp.minimum(jnp.sum(j[:, None] >= it_end[None, :], axis=1), n_exp - 1).astype(I32)
    local = j - it0[e]
    item_nblk = jnp.clip(nblk[e] - local * MOE_MAX_BLK, 0, MOE_MAX_BLK).astype(I32)
    item_row0 = ((blk0[e] + local * MOE_MAX_BLK) * ROW_BLK).astype(I32)
    n_items = it_end[-1].astype(I32)
    item_nblk = jnp.where(j < n_items, item_nblk, 0)
    item_row0 = jnp.where(j < n_items, item_row0, 0)
    row_start = (blk0 * ROW_BLK).astype(F32)
    used_blk = blk_end[-1].astype(I32)
    meta = jnp.stack([n_items, used_blk])
    blk = jnp.arange(n_blk_total, dtype=I32)
    partial = jnp.any((blk[:, None] == (blk_end - 1)[None, :]) & ((counts % ROW_BLK) != 0)[None, :], axis=1)
    fill = jnp.logical_or(partial, blk >= used_blk).astype(I32)
    return e, item_row0, item_nblk, meta, row_start, fill


def _mixer_stream(x, mod, hist, h0, pos0, w):
    b, l, d = x.shape
    sh1, sc1, g1, sh2, sc2, g2 = [m.reshape(b, 1, d) for m in jnp.split(mod, 6, axis=-1)]
    u_pool, u_ssm = _inproj(x, sc1, sh1, w["w_in"])
    hist16 = jnp.concatenate([jnp.zeros((b, HIST_ROWS - hist.shape[1], hist.shape[2]), F32), hist.astype(F32)], axis=1)
    y_pool, new_hist = _pool(u_pool, hist16, w["w_pool"], w["pool_scale"], pos0)
    y_ssm, h_new = _ssm(u_ssm, w["ssm"], h0)
    x1, hp, idx, gates, cnt = _outproj(y_pool, y_ssm, x, g1, sc2, sh2, w["w_out"], w["ln1_g"], w["ln1_b"],
                                       w["w_router"], w["b_router"], w["n_experts"])
    s_re, s_im = _slabs_to_state(h_new, SSM_STATE)
    return dict(x1=x1, hp=hp, idx=idx, gates=gates, cnt=cnt, g2=g2,
                hist=new_hist[:, 1:, :], s_re=s_re, s_im=s_im)


def kernel(x_prompt, x_sample, cache_pool, state_ssm_re, state_ssm_im, c_prompt, c_sample, w_ada, b_ada, w_in, w_pool, pool_scale, lambda_re, lambda_im, log_dt, ssm_b_re, ssm_b_im, ssm_c_re, ssm_c_im, d_skip, w_glu, b_glu, w_out, ln1_g, ln1_b, w_router, b_router, w_gate_up, b_gate_up, w_down, b_down, ln2_g, ln2_b):
    assert w_ada.shape[0] == DEPTH
    bp, lp, d = x_prompt.shape
    bs, ls, _ = x_sample.shape
    assert bp == SUBLANES and bs == SUBLANES, "the S5 kernel puts the 8 streams on sublanes"
    n_exp = w_router.shape[-1]
    dp = w_pool.shape[1] * w_pool.shape[2]

    w = dict(
        w_in=w_in[0].astype(BF16),
        w_pool=w_pool[0].astype(BF16),
        pool_scale=pool_scale[0].astype(F32).reshape(1, dp),
        ssm=_ssm_params(lambda_re[0], lambda_im[0], log_dt[0], ssm_b_re[0], ssm_b_im[0], ssm_c_re[0], ssm_c_im[0],
                        d_skip[0], w_glu[0], b_glu[0], SUBLANES),
        w_out=w_out[0].astype(BF16),
        ln1_g=ln1_g[0].reshape(1, d), ln1_b=ln1_b[0].reshape(1, d),
        w_router=jnp.pad(w_router[0], ((0, 0), (0, LANES - n_exp))).astype(BF16),
        b_router=jnp.pad(b_router[0].astype(F32), (0, LANES - n_exp)).reshape(1, LANES),
        n_experts=n_exp,
    )

    mod = _ada(jnp.concatenate([c_prompt, c_sample], axis=0), w_ada[0], b_ada[0].reshape(1, -1))
    n_slab = lambda_re.shape[1]
    p = _mixer_stream(x_prompt, mod[:bp], jnp.zeros((bp, HIST_ROWS - 1, dp), F32),
                      jnp.zeros((n_slab, bp, LANES), F32), 0, w)
    s = _mixer_stream(x_sample, mod[bp:], cache_pool[0], _state_to_slabs(state_ssm_re[0], state_ssm_im[0]),
                      PAST_LEN, w)

    tp, ts = bp * lp, bs * ls
    n_assign = (tp + ts) * TOP_K
    n_blk_total = -(-n_assign // ROW_BLK) + n_exp
    n_rows = n_blk_total * ROW_BLK
    counts = (p["cnt"][0, :n_exp] + s["cnt"][0, :n_exp]).astype(I32)
    item_e, item_row0, item_nblk, meta, row_start, fill = _moe_items(counts, n_blk_total)
    start_pad = jnp.zeros((SUBLANES, LANES), F32).at[0, :n_exp].set(row_start)
    pos = _plan(jnp.concatenate([p["idx"], s["idx"]], axis=0), start_pad)
    pos_p, pos_s = pos[:tp], pos[tp:]
    xs = _dispatch(pos, p["hp"], s["hp"], fill, n_rows)
    ys = _moe(xs, item_e, item_row0, item_nblk, meta, w_gate_up[0], b_gate_up[0], w_down[0], b_down[0])
    g2l, b2l = ln2_g[0].reshape(1, d), ln2_b[0].reshape(1, d)
    y_p = _combine(pos_p, ys, p["gates"], p["x1"], p["g2"], g2l, b2l)
    y_s = _combine(pos_s, ys, s["gates"], s["x1"], s["g2"], g2l, b2l)

    return (y_p, y_s, p["hist"][None], p["s_re"][None], p["s_im"][None],
            s["hist"][None], s["s_re"][None], s["s_im"][None])
```

```python
import functools
import math

import jax
import jax.numpy as jnp
from jax import lax
from jax.experimental import pallas as pl
from jax.experimental.pallas import tpu as pltpu

F32 = jnp.float32
BF16 = jnp.bfloat16
I32 = jnp.int32
U32 = jnp.uint32

POOL_WINDOWS = (2, 4, 8, 16)
HIST_ROWS = 16
SSM_CH = 16
SSM_STATE = 64
TOP_K = 4
SWIGLU_LIMIT = 7.0
SWIGLU_ALPHA = 1.702
LN_EPS = 1e-5
DEPTH = 1
DN_ALPHA = (2 * DEPTH) ** 0.25
PAST_LEN = 2048

LANES = 128
SUBLANES = 8
MXU_DIM = 256
V7X_VMEM_BYTES = 64 * 1024 * 1024
MIB = 1024 * 1024

ROW_BLK = 256
MOE_RB = 512
MOE_CW = 256
MOE_MAX_BLK = 10
YS_SLOTS = 6


def _params(sem, need_bytes):
    limit = min(need_bytes + 16 * MIB, V7X_VMEM_BYTES - 4 * MIB)
    return pltpu.CompilerParams(dimension_semantics=sem, vmem_limit_bytes=limit)


def _layer_norm(v, g, b):
    mu = jnp.mean(v, axis=-1, keepdims=True)
    vc = v - mu
    var = jnp.mean(vc * vc, axis=-1, keepdims=True)
    return vc * lax.rsqrt(var + LN_EPS) * g + b


def _ada_kernel(c_ref, w_ref, b_ref, o_ref):
    c = c_ref[...]
    s = c * jax.nn.sigmoid(c)
    o_ref[...] = jnp.dot(s.astype(BF16), w_ref[...].astype(BF16), preferred_element_type=F32) + b_ref[...]


def _ada(c_all, w_ada, b_ada):
    r, d = c_all.shape
    n = w_ada.shape[1]
    tn = 1536
    return pl.pallas_call(
        _ada_kernel,
        grid=(n // tn,),
        in_specs=[pl.BlockSpec((r, d), lambda j: (0, 0)),
                  pl.BlockSpec((d, tn), lambda j: (0, j)),
                  pl.BlockSpec((1, tn), lambda j: (0, j))],
        out_specs=pl.BlockSpec((r, tn), lambda j: (0, j)),
        out_shape=jax.ShapeDtypeStruct((r, n), F32),
        compiler_params=_params(("arbitrary",), 2 * (d * tn * 4 + r * (d + tn) * 4)),
        name="ada_mod",
    )(c_all, w_ada, b_ada)


def _inproj_kernel(x_ref, sc_ref, sh_ref, w_ref, up_ref, us_ref):
    h = x_ref[...] * (1.0 + sc_ref[...]) + sh_ref[...]
    u = jnp.dot(h.astype(BF16), w_ref[...], preferred_element_type=F32)
    dp = up_ref.shape[-1]
    up_ref[...] = u[:, :dp]
    us_ref[...] = u[:, dp:]


def _inproj(x, sc, sh, w_bf):
    b, l, d = x.shape
    dm = w_bf.shape[1]
    dp = dm // 2
    tm = min(l, 512)
    row = pl.BlockSpec((None, tm, d), lambda bi, i: (bi, i, 0))
    mod = pl.BlockSpec((None, 1, d), lambda bi, i: (bi, 0, 0))
    half = pl.BlockSpec((None, tm, dp), lambda bi, i: (bi, i, 0))
    return pl.pallas_call(
        _inproj_kernel,
        grid=(b, l // tm),
        in_specs=[row, mod, mod, pl.BlockSpec((d, dm), lambda bi, i: (0, 0))],
        out_specs=[half, half],
        out_shape=[jax.ShapeDtypeStruct((b, l, dp), F32), jax.ShapeDtypeStruct((b, l, dm - dp), F32)],
        compiler_params=_params(("arbitrary", "arbitrary"), 2 * (d * dm * 2 + tm * (d + dm) * 4)),
        name="in_proj",
    )(x, sc, sh, w_bf)


def _pool_kernel(u_ref, hist_ref, w_ref, scale_ref, y_ref, nh_ref, hbuf, *, tt, pos0):
    i = pl.program_id(1)

    @pl.when(i == 0)
    def _():
        hbuf[...] = hist_ref[...]

    u = u_ref[...]
    hb = hbuf[...]
    u16 = u.astype(BF16)
    h16 = hb.astype(BF16)
    pc = u.shape[1] // len(POOL_WINDOWS)
    d_main = lax.broadcasted_iota(I32, (tt, tt), 0) - lax.broadcasted_iota(I32, (tt, tt), 1)
    d_hist = (lax.broadcasted_iota(I32, (tt, HIST_ROWS), 0) + HIST_ROWS
              - lax.broadcasted_iota(I32, (tt, HIST_ROWS), 1))
    pos = (pos0 + i * tt + lax.broadcasted_iota(I32, (tt, 1), 0)).astype(F32)
    for g, w in enumerate(POOL_WINDOWS):
        sl = slice(g * pc, (g + 1) * pc)
        bm = jnp.where(d_main >= 0, jnp.where(d_main < w, 1.0, 0.0), 0.0).astype(BF16)
        bh = jnp.where(d_hist < w, 1.0, 0.0).astype(BF16)
        s = (jnp.dot(bm, u16[:, sl], preferred_element_type=F32)
             + jnp.dot(bh, h16[:, sl], preferred_element_type=F32))
        cnt = jnp.minimum(pos + 1.0, float(w))
        diff = s / cnt - u[:, sl]
        y = jnp.dot(diff.astype(BF16), w_ref[g], preferred_element_type=F32) * scale_ref[:, sl]
        y_ref[:, sl] = y.astype(y_ref.dtype)
    hbuf[...] = u[tt - HIST_ROWS:, :]

    @pl.when(i == pl.num_programs(1) - 1)
    def _():
        nh_ref[...] = hbuf[...]


def _pool(u_pool, hist16, w_pool_bf, pool_scale, pos0):
    b, l, dp = u_pool.shape
    tt = min(l, 256)
    g, pc, _ = w_pool_bf.shape
    kern = functools.partial(_pool_kernel, tt=tt, pos0=pos0)
    return pl.pallas_call(
        kern,
        grid=(b, l // tt),
        in_specs=[pl.BlockSpec((None, tt, dp), lambda bi, i: (bi, i, 0)),
                  pl.BlockSpec((None, HIST_ROWS, dp), lambda bi, i: (bi, 0, 0)),
                  pl.BlockSpec((g, pc, pc), lambda bi, i: (0, 0, 0)),
                  pl.BlockSpec((1, dp), lambda bi, i: (0, 0))],
        out_specs=[pl.BlockSpec((None, tt, dp), lambda bi, i: (bi, i, 0)),
                   pl.BlockSpec((None, HIST_ROWS, dp), lambda bi, i: (bi, 0, 0))],
        out_shape=[jax.ShapeDtypeStruct((b, l, dp), BF16), jax.ShapeDtypeStruct((b, HIST_ROWS, dp), F32)],
        scratch_shapes=[pltpu.VMEM((HIST_ROWS, dp), F32)],
        compiler_params=_params(("arbitrary", "arbitrary"), 2 * (tt * dp * 6 + g * pc * pc * 2 + 2 * HIST_ROWS * dp * 4)),
        name="pool_mixer",
    )(u_pool, hist16, w_pool_bf, pool_scale)


SSM_PAIR_GROUP = 8


def _ssm_kernel(u_ref, bc_ref, cc_ref, wg_ref, ar_ref, ai_ref, dsk_ref, bgl_ref, h0_ref,
                y_ref, hn_ref, s_ref, ubuf, hst, *, tt, pitch, nb):
    i = pl.program_id(0)
    n_slab = s_ref.shape[0]
    n_chunk = bc_ref.shape[0]
    slab_per_chunk = n_slab // n_chunk
    cw = bc_ref.shape[1]

    @pl.when(i == 0)
    def _():
        s_ref[...] = jnp.zeros_like(s_ref)
        ubuf[...] = jnp.zeros_like(ubuf)
        hst[...] = h0_ref[...]

    u = u_ref[...].reshape(nb * tt, u_ref.shape[-1])
    u16 = u.astype(BF16)
    for k in range(n_chunk):
        bu = jnp.dot(u16[:, k * cw:(k + 1) * cw], bc_ref[k], preferred_element_type=F32)
        for j in range(slab_per_chunk):
            for b in range(nb):
                s_ref[k * slab_per_chunk + j, b * pitch:b * pitch + tt, :] = (
                    bu[b * tt:(b + 1) * tt, j * LANES:(j + 1) * LANES])
    for b in range(nb):
        ubuf[b * pitch:b * pitch + tt, :] = u[b * tt:(b + 1) * tt, :]

    n_pair = n_slab // 2
    for pg in range(n_pair // SSM_PAIR_GROUP):
        qs = [pg * SSM_PAIR_GROUP + j for j in range(SSM_PAIR_GROUP)]
        ars = [ar_ref[q] for q in qs]
        ais = [ai_ref[q] for q in qs]

        def body(t, carry, qs=qs, ars=ars, ais=ais):
            hr, hi = carry
            nhr, nhi = [], []
            rows = pl.ds(t, nb, stride=pitch)
            for j, q in enumerate(qs):
                bur = s_ref[2 * q, rows, :]
                bui = s_ref[2 * q + 1, rows, :]
                r_ = ars[j] * hr[j] - ais[j] * hi[j] + bur
                i_ = ars[j] * hi[j] + ais[j] * hr[j] + bui
                s_ref[2 * q, rows, :] = r_
                s_ref[2 * q + 1, rows, :] = i_
                nhr.append(r_)
                nhi.append(i_)
            return tuple(nhr), tuple(nhi)

        init = (tuple(hst[2 * q] for q in qs), tuple(hst[2 * q + 1] for q in qs))
        hr, hi = lax.fori_loop(0, tt, body, init)
        for j, q in enumerate(qs):
            hst[2 * q] = hr[j]
            hst[2 * q + 1] = hi[j]

    ycs = []
    for n in range(n_chunk):
        hc = jnp.concatenate([s_ref[n * slab_per_chunk + j] for j in range(slab_per_chunk)], axis=1)
        ycs.append(jnp.dot(hc.astype(BF16), cc_ref[n], preferred_element_type=F32))
    y = jnp.concatenate(ycs, axis=1) + dsk_ref[...] * ubuf[...]
    y = jax.nn.gelu(y)
    y16 = y.astype(BF16)
    z = jnp.concatenate([jnp.dot(y16[:, k * cw:(k + 1) * cw], wg_ref[k], preferred_element_type=F32)
                         for k in range(n_chunk)], axis=1) + bgl_ref[...]
    out = y * jax.nn.sigmoid(z)
    for b in range(nb):
        y_ref[b] = out[b * pitch:b * pitch + tt].astype(y_ref.dtype)

    @pl.when(i == pl.num_programs(0) - 1)
    def _():
        hn_ref[...] = hst[...]


def _ssm(u_ssm, prm, h0):
    b, l, ds = u_ssm.shape
    tt = min(l, 64)
    pitch = tt + 4
    n_slab = h0.shape[0]
    n_chunk = prm["bc"].shape[0]
    kern = functools.partial(_ssm_kernel, tt=tt, pitch=pitch, nb=b)

    def const(a):
        nd = a.ndim
        return pl.BlockSpec(a.shape, lambda i, nd=nd: (0,) * nd)

    args = (u_ssm, prm["bc"], prm["cc"], prm["wg"], prm["ar"], prm["ai"], prm["dsk"], prm["bgl"], h0)
    return pl.pallas_call(
        kern,
        grid=(l // tt,),
        in_specs=[pl.BlockSpec((b, tt, ds), lambda i: (0, i, 0))] + [const(a) for a in args[1:]],
        out_specs=[pl.BlockSpec((b, tt, ds), lambda i: (0, i, 0)),
                   pl.BlockSpec((n_slab, b, LANES), lambda i: (0, 0, 0))],
        out_shape=[jax.ShapeDtypeStruct((b, l, ds), BF16), jax.ShapeDtypeStruct((n_slab, b, LANES), F32)],
        scratch_shapes=[pltpu.VMEM((n_slab, b * pitch, LANES), F32),
                        pltpu.VMEM((b * pitch, ds), F32),
                        pltpu.VMEM((n_slab, b, LANES), F32)],
        compiler_params=_params(("arbitrary",), (n_slab * LANES + ds) * b * pitch * 4 + 2 * b * tt * ds * 6
                                + 2 * sum(a.size * a.dtype.itemsize for a in args[1:])),
        name="ssm_mixer",
    )(*args)


def _ssm_params(lambda_re, lambda_im, log_dt, b_re, b_im, c_re, c_im, d_skip, w_glu, b_glu, nb):
    g, p = lambda_re.shape
    ch = b_re.shape[-1]
    gpc = MXU_DIM // ch
    n_chunk = g // gpc
    ppc = gpc // 2
    lr, li = lambda_re.astype(F32), lambda_im.astype(F32)
    dt = jnp.exp(log_dt.astype(F32))[:, None]
    mag = jnp.exp(lr * dt)
    abar_re, abar_im = mag * jnp.cos(li * dt), mag * jnp.sin(li * dt)
    nr, ni = abar_re - 1.0, abar_im
    den = lr * lr + li * li
    k_re = (nr * lr + ni * li) / den
    k_im = (ni * lr - nr * li) / den
    br, bi = b_re.astype(F32), b_im.astype(F32)
    bb_re = k_re[..., None] * br - k_im[..., None] * bi
    bb_im = k_re[..., None] * bi + k_im[..., None] * br
    eye_2 = jnp.eye(2, dtype=F32)

    def pair_block_diag(blocks):
        _, _, r, c = blocks.shape
        out = jnp.zeros((n_chunk, ppc * r, ppc * c), blocks.dtype)
        for q in range(ppc):
            out = out.at[:, q * r:(q + 1) * r, q * c:(q + 1) * c].set(blocks[:, q])
        return out

    bb = jnp.stack([bb_re, bb_im]).reshape(2, n_chunk, ppc, 2, p, ch)
    bc = pair_block_diag(jnp.einsum("rkqgpc,gy->kqgcryp", bb, eye_2).astype(BF16)
                         .reshape(n_chunk, ppc, 2 * ch, 2 * 2 * p))
    cm = jnp.stack([c_re.astype(F32), -c_im.astype(F32)]).reshape(2, n_chunk, ppc, 2, ch, p)
    cc = pair_block_diag(jnp.einsum("rkqgcp,gy->kqrgpyc", cm, eye_2).astype(BF16)
                         .reshape(n_chunk, ppc, 2 * 2 * p, 2 * ch))
    wg = jnp.einsum("kgce,gx->kgcxe", w_glu.astype(F32).reshape(n_chunk, gpc, ch, ch),
                    jnp.eye(gpc, dtype=F32)).reshape(n_chunk, gpc * ch, gpc * ch)
    n_pair = g // 2
    ar = jnp.broadcast_to(abar_re.reshape(n_pair, 1, 2 * p), (n_pair, nb, 2 * p))
    ai = jnp.broadcast_to(abar_im.reshape(n_pair, 1, 2 * p), (n_pair, nb, 2 * p))
    return dict(bc=bc.astype(BF16), cc=cc.astype(BF16), wg=wg.astype(BF16), ar=ar, ai=ai,
                dsk=d_skip.astype(F32).reshape(1, g * ch), bgl=b_glu.astype(F32).reshape(1, g * ch))


def _state_to_slabs(s_re, s_im):
    b, g, p = s_re.shape
    re = s_re.astype(F32).reshape(b, g // 2, 2 * p).transpose(1, 0, 2)
    im = s_im.astype(F32).reshape(b, g // 2, 2 * p).transpose(1, 0, 2)
    return jnp.stack([re, im], axis=1).reshape(g, b, 2 * p)


def _slabs_to_state(h, p):
    n_slab, b, _ = h.shape
    h = h.reshape(n_slab // 2, 2, b, 2 * p)
    re = h[:, 0].transpose(1, 0, 2).reshape(b, n_slab, p)
    im = h[:, 1].transpose(1, 0, 2).reshape(b, n_slab, p)
    return re, im


def _outproj_kernel(yp_ref, ys_ref, x_ref, g1_ref, sc2_ref, sh2_ref, w_ref, lng_ref, lnb_ref, wr_ref, br_ref,
                    x1_ref, hp_ref, idx_ref, gate_ref, cnt_ref, *, n_experts):
    first = jnp.logical_and(pl.program_id(0) == 0, pl.program_id(1) == 0)
    ymix = jnp.concatenate([yp_ref[...], ys_ref[...]], axis=1)
    mix = jnp.dot(ymix, w_ref[...], preferred_element_type=F32)
    x1 = _layer_norm(DN_ALPHA * x_ref[...] + g1_ref[...] * mix, lng_ref[...], lnb_ref[...])
    x1_ref[...] = x1
    h2 = x1 * (1.0 + sc2_ref[...]) + sh2_ref[...]
    hp_ref[...] = h2
    logits = jnp.dot(h2.astype(BF16), wr_ref[...], preferred_element_type=F32) + br_ref[...]
    tm = logits.shape[0]
    lane = lax.broadcasted_iota(I32, (tm, LANES), 1).astype(F32)
    cur = jnp.where(lane < n_experts, logits, -jnp.inf)
    vals, onehot = [], jnp.zeros((tm, LANES), F32)
    for k in range(TOP_K):
        m = jnp.max(cur, axis=1, keepdims=True)
        sel = jnp.min(jnp.where(cur == m, lane, float(LANES)), axis=1, keepdims=True)
        hit = lane == sel
        idx_ref[:, k:k + 1] = sel.astype(I32)
        vals.append(m)
        onehot = onehot + jnp.where(hit, 1.0, 0.0)
        cur = jnp.where(hit, -jnp.inf, cur)
    es = [jnp.exp(v - vals[0]) for v in vals]
    den = es[0] + es[1] + es[2] + es[3]
    for k in range(TOP_K):
        gate_ref[:, k:k + 1] = es[k] / den

    @pl.when(first)
    def _():
        cnt_ref[...] = jnp.zeros_like(cnt_ref)

    cnt_ref[0:1, :] += jnp.sum(onehot, axis=0, keepdims=True)


def _outproj(y_pool, y_ssm, x, g1, sc2, sh2, w_out_bf, ln_g, ln_b, wr_bf, br_pad, n_experts):
    b, l, d = x.shape
    dp = y_pool.shape[-1]
    tm = min(l, 256)
    nl = l // tm
    t = b * l
    row = pl.BlockSpec((None, tm, d), lambda bi, i: (bi, i, 0))
    halfrow = pl.BlockSpec((None, tm, dp), lambda bi, i: (bi, i, 0))
    mod = pl.BlockSpec((None, 1, d), lambda bi, i: (bi, 0, 0))
    vec = pl.BlockSpec((1, d), lambda bi, i: (0, 0))
    tok4 = pl.BlockSpec((tm, TOP_K), lambda bi, i: (bi * nl + i, 0))
    kern = functools.partial(_outproj_kernel, n_experts=n_experts)
    return pl.pallas_call(
        kern,
        grid=(b, nl),
        in_specs=[halfrow, halfrow, row, mod, mod, mod,
                  pl.BlockSpec((d, d), lambda bi, i: (0, 0)), vec, vec,
                  pl.BlockSpec((d, LANES), lambda bi, i: (0, 0)),
                  pl.BlockSpec((1, LANES), lambda bi, i: (0, 0))],
        out_specs=[row,
                   pl.BlockSpec((tm, d), lambda bi, i: (bi * nl + i, 0)),
                   tok4, tok4,
                   pl.BlockSpec((SUBLANES, LANES), lambda bi, i: (0, 0))],
        out_shape=[jax.ShapeDtypeStruct((b, l, d), F32),
                   jax.ShapeDtypeStruct((t, d), F32),
                   jax.ShapeDtypeStruct((t, TOP_K), I32),
                   jax.ShapeDtypeStruct((t, TOP_K), F32),
                   jax.ShapeDtypeStruct((SUBLANES, LANES), F32)],
        compiler_params=_params(("arbitrary", "arbitrary"), 2 * (d * d * 2 + d * LANES * 2 + tm * d * 14)),
        name="out_proj_router",
    )(y_pool, y_ssm, x, g1, sc2, sh2, w_out_bf, ln_g, ln_b, wr_bf, br_pad)


def _plan_kernel(idx_ref, start_ref, pos_ref, carry):
    i = pl.program_id(0)

    @pl.when(i == 0)
    def _():
        carry[...] = jnp.zeros_like(carry)

    idx = idx_ref[...]
    tp = idx.shape[0]
    lane = lax.broadcasted_iota(I32, (tp, LANES), 1)
    hits = [lane == idx[:, k:k + 1] for k in range(TOP_K)]
    onehot = jnp.zeros((tp, LANES), F32)
    for h in hits:
        onehot = onehot + jnp.where(h, 1.0, 0.0)
    below = jnp.where(lax.broadcasted_iota(I32, (tp, tp), 0) > lax.broadcasted_iota(I32, (tp, tp), 1),
                      1.0, 0.0).astype(BF16)
    excl = jnp.dot(below, onehot.astype(BF16), preferred_element_type=F32)
    slot = start_ref[0:1, :] + carry[0:1, :] + excl
    for k in range(TOP_K):
        pos_ref[:, k:k + 1] = jnp.sum(jnp.where(hits[k], slot, 0.0), axis=1, keepdims=True).astype(I32)
    carry[0:1, :] += jnp.sum(onehot, axis=0, keepdims=True)


def _plan(idx_all, start_pad):
    t = idx_all.shape[0]
    tp = 128
    return pl.pallas_call(
        _plan_kernel,
        grid=(t // tp,),
        in_specs=[pl.BlockSpec((tp, TOP_K), lambda i: (i, 0)),
                  pl.BlockSpec((SUBLANES, LANES), lambda i: (0, 0))],
        out_specs=pl.BlockSpec((tp, TOP_K), lambda i: (i, 0)),
        out_shape=jax.ShapeDtypeStruct((t, TOP_K), I32),
        scratch_shapes=[pltpu.VMEM((SUBLANES, LANES), F32)],
        compiler_params=_params(("arbitrary",), 4 * tp * LANES * 4),
        name="moe_plan",
    )(idx_all, start_pad)


DISPATCH_UNROLL = 4


def _dispatch_kernel(fill_ref, posp_ref, poss_ref, hp_ref, hs_ref, xs_ref, zbuf, sem, zsem, *, n_p_steps, n_blk):
    i = pl.program_id(0)

    def fill_copy(b):
        return pltpu.make_async_copy(zbuf, xs_ref.at[pl.ds(pl.multiple_of(b * ROW_BLK, ROW_BLK), ROW_BLK)], zsem)

    @pl.when(i == 0)
    def _():
        zbuf[...] = jnp.zeros_like(zbuf)

        def start(b, carry):
            @pl.when(fill_ref[b] != 0)
            def _():
                fill_copy(b).start()
            return carry

        def wait(b, carry):
            @pl.when(fill_ref[b] != 0)
            def _():
                fill_copy(b).wait()
            return carry

        lax.fori_loop(0, n_blk, start, 0)
        lax.fori_loop(0, n_blk, wait, 0)

    def scatter(src_ref, pos_ref):
        td = src_ref.shape[0]

        def body(t2, carry):
            for tt in range(DISPATCH_UNROLL):
                t = t2 * DISPATCH_UNROLL + tt
                for k in range(TOP_K):
                    p = pos_ref[0, 0, t * TOP_K + k]
                    pltpu.make_async_copy(src_ref.at[pl.ds(t, 1)], xs_ref.at[pl.ds(p, 1)], sem).start()
            return carry

        lax.fori_loop(0, td // DISPATCH_UNROLL, body, 0)
        for k in range(TOP_K):
            pltpu.make_async_copy(src_ref, xs_ref.at[pl.ds(0, td)], sem).wait()

    @pl.when(i < n_p_steps)
    def _():
        scatter(hp_ref, posp_ref)

    @pl.when(i >= n_p_steps)
    def _():
        scatter(hs_ref, poss_ref)


def _dispatch(pos_p, pos_s, h_prompt, h_sample, fill_flags, n_rows):
    tp, w = h_prompt.shape
    ts = h_sample.shape[0]
    tdp, tds = math.gcd(tp, 512), math.gcd(ts, 512)
    n_p_steps, n_s_steps = tp // tdp, ts // tds
    n_blk = fill_flags.shape[0]
    grid_spec = pltpu.PrefetchScalarGridSpec(
        num_scalar_prefetch=1,
        grid=(n_p_steps + n_s_steps,),
        in_specs=[pl.BlockSpec((1, 1, tdp * TOP_K), lambda i, f: (jnp.minimum(i, n_p_steps - 1), 0, 0),
                               memory_space=pltpu.SMEM),
                  pl.BlockSpec((1, 1, tds * TOP_K), lambda i, f: (jnp.maximum(i - n_p_steps, 0), 0, 0),
                               memory_space=pltpu.SMEM),
                  pl.BlockSpec((tdp, w), lambda i, f: (jnp.minimum(i, n_p_steps - 1), 0)),
                  pl.BlockSpec((tds, w), lambda i, f: (jnp.maximum(i - n_p_steps, 0), 0))],
        out_specs=pl.BlockSpec(memory_space=pl.ANY),
        scratch_shapes=[pltpu.VMEM((ROW_BLK, w), F32),
                        pltpu.SemaphoreType.DMA(()), pltpu.SemaphoreType.DMA(())],
    )
    need = 2 * (tdp + tds) * w * 4 + ROW_BLK * w * 4
    return pl.pallas_call(
        functools.partial(_dispatch_kernel, n_p_steps=n_p_steps, n_blk=n_blk),
        grid_spec=grid_spec,
        out_shape=jax.ShapeDtypeStruct((n_rows, w), F32),
        compiler_params=_params(("arbitrary",), need),
        name="moe_dispatch",
    )(fill_flags, pos_p.reshape(n_p_steps, 1, tdp * TOP_K), pos_s.reshape(n_s_steps, 1, tds * TOP_K),
      h_prompt, h_sample)


def _moe_kernel(ie_ref, ir_ref, inb_ref, meta_ref, xs_ref, wg_ref, wu_ref, bg_ref, bu_ref, wd_ref, bd_ref,
                ys_ref, xbuf, act, ystage, zbuf, yflag, xsem, ysem, zsem, *, nc, n_blk):
    i = pl.program_id(0)
    c = pl.program_id(1)
    n_items = meta_ref[0]
    used_blk = meta_ref[1]
    valid = i < n_items
    nb = inb_ref[i]
    row0 = pl.multiple_of(ir_ref[i], ROW_BLK)
    blk_per_pair = 2 * MOE_RB // ROW_BLK
    n_pair = nb // blk_per_pair
    rem = nb - n_pair * blk_per_pair
    has_rb = rem >= MOE_RB // ROW_BLK
    has_blk = (rem % (MOE_RB // ROW_BLK)) != 0
    base_rb = pl.multiple_of(n_pair * 2 * MOE_RB, ROW_BLK)
    base_blk = pl.multiple_of(base_rb + jnp.where(has_rb, MOE_RB, 0), ROW_BLK)
    first = jnp.logical_and(i == 0, c == 0)
    last = jnp.logical_and(i == pl.num_programs(0) - 1, c == pl.num_programs(1) - 1)

    def x_copy(it, j):
        src0 = pl.multiple_of(ir_ref[it] + j * ROW_BLK, ROW_BLK)
        return pltpu.make_async_copy(xs_ref.at[pl.ds(src0, ROW_BLK)],
                                     xbuf.at[pl.ds(pl.multiple_of(j * ROW_BLK, ROW_BLK), ROW_BLK)], xsem)

    def x_start(it):
        def start(j, carry):
            x_copy(it, j).start()
            return carry
        lax.fori_loop(0, inb_ref[it], start, 0)

    def x_wait(it):
        def wait(j, carry):
            x_copy(it, j).wait()
            return carry
        lax.fori_loop(0, inb_ref[it], wait, 0)

    def tail_copy(b, col):
        return pltpu.make_async_copy(
            zbuf, ys_ref.at[pl.ds(pl.multiple_of(b * ROW_BLK, ROW_BLK), ROW_BLK), pl.ds(col * MOE_CW, MOE_CW)], zsem)

    n_col = ys_ref.shape[1] // MOE_CW

    @pl.when(first)
    def _():
        x_start(0)
        zbuf[...] = jnp.zeros_like(zbuf)
        for slot in range(YS_SLOTS):
            yflag[slot] = 0

        def start(b, carry):
            for col in range(n_col):
                tail_copy(b, col).start()
            return carry
        lax.fori_loop(used_blk, n_blk, start, 0)

    @pl.when(jnp.logical_and(valid, c == 0))
    def _():
        x_wait(i)

    @pl.when(jnp.logical_and(c == nc, i + 1 < n_items))
    def _():
        x_start(i + 1)

    @pl.when(jnp.logical_and(valid, c < nc))
    def _():
        def piece(r0, rows):
            w, bias = [], []
            for s in range(MOE_CW // LANES):
                cols = slice(s * LANES, (s + 1) * LANES)
                w += [wg_ref[:, cols].astype(BF16), wu_ref[:, cols].astype(BF16)]
                bias += [bg_ref[:, cols], bu_ref[:, cols]]
            x = xbuf[pl.ds(r0, rows), :].astype(BF16)
            gu = (jnp.dot(x, jnp.concatenate(w, axis=1), preferred_element_type=F32)
                  + jnp.concatenate(bias, axis=1))
            for s in range(MOE_CW // LANES):
                gate = jnp.minimum(gu[:, 2 * s * LANES:(2 * s + 1) * LANES], SWIGLU_LIMIT)
                up = jnp.clip(gu[:, (2 * s + 1) * LANES:(2 * s + 2) * LANES], -SWIGLU_LIMIT, SWIGLU_LIMIT)
                a = (up + 1.0) * (gate * jax.nn.sigmoid(SWIGLU_ALPHA * gate))
                act[c, pl.ds(r0, rows), s * LANES:(s + 1) * LANES] = a.astype(BF16)

        def pair(p, carry):
            r0 = pl.multiple_of(p * 2 * MOE_RB, ROW_BLK)
            piece(r0, MOE_RB)
            piece(pl.multiple_of(r0 + MOE_RB, ROW_BLK), MOE_RB)
            return carry

        lax.fori_loop(0, n_pair, pair, 0)

        @pl.when(has_rb)
        def _():
            piece(base_rb, MOE_RB)

        @pl.when(has_blk)
        def _():
            piece(base_blk, ROW_BLK)

    def y_copy(slot, rows, r0, col0):
        dst_row = pl.multiple_of(row0 + r0, ROW_BLK)
        return pltpu.make_async_copy(ystage.at[slot, pl.ds(0, rows)],
                                     ys_ref.at[pl.ds(dst_row, rows), pl.ds(col0, MOE_CW)], ysem.at[slot])

    def y_drain(slot, rows):
        @pl.when(yflag[slot] != 0)
        def _():
            y_copy(slot, rows, 0, 0).wait()

    @pl.when(jnp.logical_and(valid, c >= nc))
    def _():
        col0 = pl.multiple_of((c - nc) * MOE_CW, MOE_CW)

        def piece(r0, rows, slot):
            y_drain(slot, rows)
            a = jnp.concatenate([act[cc, pl.ds(r0, rows), :] for cc in range(nc)], axis=1)
            ystage[slot, 0:rows, :] = (jnp.dot(a, wd_ref[...].astype(BF16), preferred_element_type=F32)
                                       + bd_ref[...])
            y_copy(slot, rows, r0, col0).start()
            yflag[slot] = 1

        def pair(p, carry):
            r0 = pl.multiple_of(p * 2 * MOE_RB, ROW_BLK)
            s0 = (p & 1) * 2
            piece(r0, MOE_RB, s0)
            piece(pl.multiple_of(r0 + MOE_RB, ROW_BLK), MOE_RB, s0 + 1)
            return carry

        lax.fori_loop(0, n_pair, pair, 0)

        @pl.when(has_rb)
        def _():
            piece(base_rb, MOE_RB, 4)

        @pl.when(has_blk)
        def _():
            piece(base_blk, ROW_BLK, 5)

    @pl.when(last)
    def _():
        def wait(b, carry):
            for col in range(n_col):
                tail_copy(b, col).wait()
            return carry
        lax.fori_loop(used_blk, n_blk, wait, 0)
        for slot in range(YS_SLOTS):
            y_drain(slot, ROW_BLK if slot == YS_SLOTS - 1 else MOE_RB)


def _moe(xs, item_e, item_row0, item_nblk, meta, w_gate_up, b_gate_up, w_down, b_down):
    n_rows, d = xs.shape
    n_exp, _, ff2 = w_gate_up.shape
    ff = ff2 // 2
    assert w_gate_up.shape[1] == d and w_down.shape[1:] == (ff, d) and ff == d
    nc = ff // MOE_CW
    ni = item_e.shape[0]
    rbuf = MOE_MAX_BLK * ROW_BLK

    def item(i, n):
        return jnp.minimum(i, n[0] - 1)

    def c1(i, c, n):
        return jnp.where(i < n[0], jnp.minimum(c, nc - 1), nc - 1)

    def c2(i, c, n):
        return jnp.where(i < n[0], jnp.maximum(c - nc, 0), nc - 1)

    bgu3 = b_gate_up.reshape(n_exp, 1, ff2)
    bd3 = b_down.reshape(n_exp, 1, d)
    in_specs = [
        pl.BlockSpec(memory_space=pl.ANY),
        pl.BlockSpec((None, d, MOE_CW), lambda i, c, e, r, b, n: (e[item(i, n)], 0, c1(i, c, n))),
        pl.BlockSpec((None, d, MOE_CW), lambda i, c, e, r, b, n: (e[item(i, n)], 0, nc + c1(i, c, n))),
        pl.BlockSpec((None, 1, MOE_CW), lambda i, c, e, r, b, n: (e[item(i, n)], 0, c1(i, c, n))),
        pl.BlockSpec((None, 1, MOE_CW), lambda i, c, e, r, b, n: (e[item(i, n)], 0, nc + c1(i, c, n))),
        pl.BlockSpec((None, ff, MOE_CW), lambda i, c, e, r, b, n: (e[item(i, n)], 0, c2(i, c, n))),
        pl.BlockSpec((None, 1, MOE_CW), lambda i, c, e, r, b, n: (e[item(i, n)], 0, c2(i, c, n))),
    ]
    grid_spec = pltpu.PrefetchScalarGridSpec(
        num_scalar_prefetch=4,
        grid=(ni, 2 * nc),
        in_specs=in_specs,
        out_specs=pl.BlockSpec(memory_space=pl.ANY),
        scratch_shapes=[pltpu.VMEM((rbuf, d), F32),
                        pltpu.VMEM((nc, rbuf, MOE_CW), BF16),
                        pltpu.VMEM((YS_SLOTS, MOE_RB, MOE_CW), F32),
                        pltpu.VMEM((ROW_BLK, MOE_CW), F32),
                        pltpu.SMEM((YS_SLOTS,), I32),
                        pltpu.SemaphoreType.DMA(()),
                        pltpu.SemaphoreType.DMA((YS_SLOTS,)),
                        pltpu.SemaphoreType.DMA(())],
    )
    need = (rbuf * d * 4 + rbuf * ff * 2 + YS_SLOTS * MOE_RB * MOE_CW * 4 + ROW_BLK * MOE_CW * 4
            + 2 * 3 * d * MOE_CW * 4)
    return pl.pallas_call(
        functools.partial(_moe_kernel, nc=nc, n_blk=n_rows // ROW_BLK),
        grid_spec=grid_spec,
        out_shape=jax.ShapeDtypeStruct((n_rows, d), F32),
        compiler_params=_params(("arbitrary", "arbitrary"), need),
        name="moe_experts",
    )(item_e, item_row0, item_nblk, meta, xs, w_gate_up, w_gate_up, bgu3, bgu3, w_down, bd3)


def _combine_kernel(pos_ref, posn_ref, ys_ref, gate_ref, x1_ref, g2_ref, lng_ref, lnb_ref, o_ref, gbuf, sem, *, tc):
    i = pl.program_id(0)
    slot = i % 2

    def issue(p_ref, s):
        def body(t2, carry):
            for tt in range(DISPATCH_UNROLL):
                t = t2 * DISPATCH_UNROLL + tt
                for k in range(TOP_K):
                    p = p_ref[0, 0, t * TOP_K + k]
                    pltpu.make_async_copy(ys_ref.at[pl.ds(p, 1)], gbuf.at[s, k, pl.ds(t, 1)], sem.at[s]).start()
            return carry
        lax.fori_loop(0, tc // DISPATCH_UNROLL, body, 0)

    @pl.when(i == 0)
    def _():
        issue(pos_ref, 0)

    @pl.when(i + 1 < pl.num_programs(0))
    def _():
        issue(posn_ref, 1 - slot)

    for k in range(TOP_K):
        pltpu.make_async_copy(ys_ref.at[pl.ds(0, tc)], gbuf.at[slot, k], sem.at[slot]).wait()
    gt = gate_ref[...]
    ffn = gt[:, 0:1] * gbuf[slot, 0]
    for k in range(1, TOP_K):
        ffn = ffn + gt[:, k:k + 1] * gbuf[slot, k]
    o_ref[...] = _layer_norm(DN_ALPHA * x1_ref[...] + g2_ref[...] * ffn, lng_ref[...], lnb_ref[...])


def _combine(pos, ys, gates, x1, g2, ln_g, ln_b):
    b, l, d = x1.shape
    t = b * l
    tc = min(l, 128)
    per_b = l // tc
    n_steps = t // tc
    pos3 = pos.reshape(n_steps, 1, tc * TOP_K)
    vec = pl.BlockSpec((1, d), lambda i: (0, 0))
    out = pl.pallas_call(
        functools.partial(_combine_kernel, tc=tc),
        grid=(n_steps,),
        in_specs=[pl.BlockSpec((1, 1, tc * TOP_K), lambda i: (i, 0, 0), memory_space=pltpu.SMEM),
                  pl.BlockSpec((1, 1, tc * TOP_K), lambda i: (jnp.minimum(i + 1, n_steps - 1), 0, 0),
                               memory_space=pltpu.SMEM),
                  pl.BlockSpec(memory_space=pl.ANY),
                  pl.BlockSpec((tc, TOP_K), lambda i: (i, 0)),
                  pl.BlockSpec((tc, d), lambda i: (i, 0)),
                  pl.BlockSpec((None, 1, d), lambda i: (i // per_b, 0, 0)),
                  vec, vec],
        out_specs=pl.BlockSpec((tc, d), lambda i: (i, 0)),
        out_shape=jax.ShapeDtypeStruct((t, d), F32),
        scratch_shapes=[pltpu.VMEM((2, TOP_K, tc, d), F32), pltpu.SemaphoreType.DMA((2,))],
        compiler_params=_params(("arbitrary",), (2 * TOP_K + 2 * 2) * tc * d * 4),
        name="moe_combine",
    )(pos3, pos3, ys, gates, x1.reshape(t, d), g2, ln_g, ln_b)
    return out.reshape(b, l, d)


def _moe_items(counts, n_blk_total):
    n_exp = counts.shape[0]
    nblk = (counts + ROW_BLK - 1) // ROW_BLK
    blk_end = jnp.cumsum(nblk)
    blk0 = blk_end - nblk
    n_it = (nblk + MOE_MAX_BLK - 1) // MOE_MAX_BLK
    it_end = jnp.cumsum(n_it)
    it0 = it_end - n_it
    ni = n_exp + n_blk_total // MOE_MAX_BLK
    j = jnp.arange(ni, dtype=I32)
    e = jnp.minimum(jnp.sum(j[:, None] >= it_end[None, :], axis=1), n_exp - 1).astype(I32)
    local = j - it0[e]
    item_nblk = jnp.clip(nblk[e] - local * MOE_MAX_BLK, 0, MOE_MAX_BLK).astype(I32)
    item_row0 = ((blk0[e] + local * MOE_MAX_BLK) * ROW_BLK).astype(I32)
    n_items = it_end[-1].astype(I32)
    item_nblk = jnp.where(j < n_items, item_nblk, 0)
    item_row0 = jnp.where(j < n_items, item_row0, 0)
    row_start = (blk0 * ROW_BLK).astype(F32)
    used_blk = blk_end[-1].astype(I32)
    meta = jnp.stack([n_items, used_blk])
    blk = jnp.arange(n_blk_total, dtype=I32)
    partial = jnp.any((blk[:, None] == (blk_end - 1)[None, :]) & ((counts % ROW_BLK) != 0)[None, :], axis=1)
    fill = jnp.logical_or(partial, blk >= used_blk).astype(I32)
    return e, item_row0, item_nblk, meta, row_start, fill


def _mixer_stream(x, mod, hist, h0, pos0, w):
    b, l, d = x.shape
    sh1, sc1, g1, sh2, sc2, g2 = [m.reshape(b, 1, d) for m in jnp.split(mod, 6, axis=-1)]
    u_pool, u_ssm = _inproj(x, sc1, sh1, w["w_in"])
    hist16 = jnp.concatenate([jnp.zeros((b, HIST_ROWS - hist.shape[1], hist.shape[2]), F32), hist.astype(F32)], axis=1)
    y_pool, new_hist = _pool(u_pool, hist16, w["w_pool"], w["pool_scale"], pos0)
    y_ssm, h_new = _ssm(u_ssm, w["ssm"], h0)
    x1, hp, idx, gates, cnt = _outproj(y_pool, y_ssm, x, g1, sc2, sh2, w["w_out"], w["ln1_g"], w["ln1_b"],
                                       w["w_router"], w["b_router"], w["n_experts"])
    s_re, s_im = _slabs_to_state(h_new, SSM_STATE)
    return dict(x1=x1, hp=hp, idx=idx, gates=gates, cnt=cnt, g2=g2,
                hist=new_hist[:, 1:, :], s_re=s_re, s_im=s_im)


def kernel(x_prompt, x_sample, cache_pool, state_ssm_re, state_ssm_im, c_prompt, c_sample, w_ada, b_ada, w_in, w_pool, pool_scale, lambda_re, lambda_im, log_dt, ssm_b_re, ssm_b_im, ssm_c_re, ssm_c_im, d_skip, w_glu, b_glu, w_out, ln1_g, ln1_b, w_router, b_router, w_gate_up, b_gate_up, w_down, b_down, ln2_g, ln2_b):
    assert w_ada.shape[0] == DEPTH
    bp, lp, d = x_prompt.shape
    bs, ls, _ = x_sample.shape
    assert bp == SUBLANES and bs == SUBLANES, "the S5 kernel puts the 8 streams on sublanes"
    n_exp = w_router.shape[-1]
    dp = w_pool.shape[1] * w_pool.shape[2]

    w = dict(
        w_in=w_in[0].astype(BF16),
        w_pool=w_pool[0].astype(BF16),
        pool_scale=pool_scale[0].astype(F32).reshape(1, dp),
        ssm=_ssm_params(lambda_re[0], lambda_im[0], log_dt[0], ssm_b_re[0], ssm_b_im[0], ssm_c_re[0], ssm_c_im[0],
                        d_skip[0], w_glu[0], b_glu[0], SUBLANES),
        w_out=w_out[0].astype(BF16),
        ln1_g=ln1_g[0].reshape(1, d), ln1_b=ln1_b[0].reshape(1, d),
        w_router=jnp.pad(w_router[0], ((0, 0), (0, LANES - n_exp))).astype(BF16),
        b_router=jnp.pad(b_router[0].astype(F32), (0, LANES - n_exp)).reshape(1, LANES),
        n_experts=n_exp,
    )

    mod = _ada(jnp.concatenate([c_prompt, c_sample], axis=0), w_ada[0], b_ada[0].reshape(1, -1))
    n_slab = lambda_re.shape[1]
    p = _mixer_stream(x_prompt, mod[:bp], jnp.zeros((bp, HIST_ROWS - 1, dp), F32),
                      jnp.zeros((n_slab, bp, LANES), F32), 0, w)
    s = _mixer_stream(x_sample, mod[bp:], cache_pool[0], _state_to_slabs(state_ssm_re[0], state_ssm_im[0]),
                      PAST_LEN, w)

    tp, ts = bp * lp, bs * ls
    n_assign = (tp + ts) * TOP_K
    n_blk_total = -(-n_assign // ROW_BLK) + n_exp
    n_rows = n_blk_total * ROW_BLK
    counts = (p["cnt"][0, :n_exp] + s["cnt"][0, :n_exp]).astype(I32)
    item_e, item_row0, item_nblk, meta, row_start, fill = _moe_items(counts, n_blk_total)
    start_pad = jnp.zeros((SUBLANES, LANES), F32).at[0, :n_exp].set(row_start)
    pos = _plan(jnp.concatenate([p["idx"], s["idx"]], axis=0), start_pad)
    pos_p, pos_s = pos[:tp], pos[tp:]
    xs = _dispatch(pos_p, pos_s, p["hp"], s["hp"], fill, n_rows)
    ys = _moe(xs, item_e, item_row0, item_nblk, meta, w_gate_up[0], b_gate_up[0], w_down[0], b_down[0])
    g2l, b2l = ln2_g[0].reshape(1, d), ln2_b[0].reshape(1, d)
    y_p = _combine(pos_p, ys, p["gates"], p["x1"], p["g2"], g2l, b2l)
    y_s = _combine(pos_s, ys, s["gates"], s["x1"], s["g2"], g2l, b2l)

    return (y_p, y_s, p["hist"][None], p["s_re"][None], p["s_im"][None],
            s["hist"][None], s["s_re"][None], s["s_im"][None])
```

```python
import functools
import math

import jax
import jax.numpy as jnp
from jax import lax
from jax.experimental import pallas as pl
from jax.experimental.pallas import tpu as pltpu

F32 = jnp.float32
BF16 = jnp.bfloat16
I32 = jnp.int32
U32 = jnp.uint32

POOL_WINDOWS = (2, 4, 8, 16)
HIST_ROWS = 16
SSM_CH = 16
SSM_STATE = 64
TOP_K = 4
SWIGLU_LIMIT = 7.0
SWIGLU_ALPHA = 1.702
LN_EPS = 1e-5
DEPTH = 1
DN_ALPHA = (2 * DEPTH) ** 0.25
PAST_LEN = 2048

LANES = 128
SUBLANES = 8
MXU_DIM = 256
V7X_VMEM_BYTES = 64 * 1024 * 1024
MIB = 1024 * 1024

ROW_BLK = 256
MOE_RB = 512
MOE_CW = 256
MOE_MAX_BLK = 10
YS_SLOTS = 8


def _params(sem, need_bytes):
    limit = min(need_bytes + 16 * MIB, V7X_VMEM_BYTES - 4 * MIB)
    return pltpu.CompilerParams(dimension_semantics=sem, vmem_limit_bytes=limit)


def _layer_norm(v, g, b):
    mu = jnp.mean(v, axis=-1, keepdims=True)
    vc = v - mu
    var = jnp.mean(vc * vc, axis=-1, keepdims=True)
    return vc * lax.rsqrt(var + LN_EPS) * g + b


def _ada_kernel(c_ref, w_ref, b_ref, o_ref):
    c = c_ref[...]
    s = c * jax.nn.sigmoid(c)
    o_ref[...] = jnp.dot(s.astype(BF16), w_ref[...].astype(BF16), preferred_element_type=F32) + b_ref[...]


def _ada(c_all, w_ada, b_ada):
    r, d = c_all.shape
    n = w_ada.shape[1]
    tn = 1536
    return pl.pallas_call(
        _ada_kernel,
        grid=(n // tn,),
        in_specs=[pl.BlockSpec((r, d), lambda j: (0, 0)),
                  pl.BlockSpec((d, tn), lambda j: (0, j)),
                  pl.BlockSpec((1, tn), lambda j: (0, j))],
        out_specs=pl.BlockSpec((r, tn), lambda j: (0, j)),
        out_shape=jax.ShapeDtypeStruct((r, n), F32),
        compiler_params=_params(("arbitrary",), 2 * (d * tn * 4 + r * (d + tn) * 4)),
        name="ada_mod",
    )(c_all, w_ada, b_ada)


def _inproj_kernel(x_ref, sc_ref, sh_ref, w_ref, up_ref, us_ref):
    h = x_ref[...] * (1.0 + sc_ref[...]) + sh_ref[...]
    u = jnp.dot(h.astype(BF16), w_ref[...], preferred_element_type=F32)
    dp = up_ref.shape[-1]
    up_ref[...] = u[:, :dp]
    us_ref[...] = u[:, dp:]


def _inproj(x, sc, sh, w_bf):
    b, l, d = x.shape
    dm = w_bf.shape[1]
    dp = dm // 2
    tm = min(l, 512)
    row = pl.BlockSpec((None, tm, d), lambda bi, i: (bi, i, 0))
    mod = pl.BlockSpec((None, 1, d), lambda bi, i: (bi, 0, 0))
    half = pl.BlockSpec((None, tm, dp), lambda bi, i: (bi, i, 0))
    return pl.pallas_call(
        _inproj_kernel,
        grid=(b, l // tm),
        in_specs=[row, mod, mod, pl.BlockSpec((d, dm), lambda bi, i: (0, 0))],
        out_specs=[half, half],
        out_shape=[jax.ShapeDtypeStruct((b, l, dp), F32), jax.ShapeDtypeStruct((b, l, dm - dp), F32)],
        compiler_params=_params(("arbitrary", "arbitrary"), 2 * (d * dm * 2 + tm * (d + dm) * 4)),
        name="in_proj",
    )(x, sc, sh, w_bf)


def _pool_kernel(u_ref, hist_ref, w_ref, scale_ref, y_ref, nh_ref, hbuf, *, tt, pos0):
    i = pl.program_id(1)

    @pl.when(i == 0)
    def _():
        hbuf[...] = hist_ref[...]

    u = u_ref[...]
    hb = hbuf[...]
    u16 = u.astype(BF16)
    h16 = hb.astype(BF16)
    pc = u.shape[1] // len(POOL_WINDOWS)
    d_main = lax.broadcasted_iota(I32, (tt, tt), 0) - lax.broadcasted_iota(I32, (tt, tt), 1)
    d_hist = (lax.broadcasted_iota(I32, (tt, HIST_ROWS), 0) + HIST_ROWS
              - lax.broadcasted_iota(I32, (tt, HIST_ROWS), 1))
    pos = (pos0 + i * tt + lax.broadcasted_iota(I32, (tt, 1), 0)).astype(F32)
    for g, w in enumerate(POOL_WINDOWS):
        sl = slice(g * pc, (g + 1) * pc)
        bm = jnp.where(d_main >= 0, jnp.where(d_main < w, 1.0, 0.0), 0.0).astype(BF16)
        bh = jnp.where(d_hist < w, 1.0, 0.0).astype(BF16)
        s = (jnp.dot(bm, u16[:, sl], preferred_element_type=F32)
             + jnp.dot(bh, h16[:, sl], preferred_element_type=F32))
        cnt = jnp.minimum(pos + 1.0, float(w))
        diff = s / cnt - u[:, sl]
        y = jnp.dot(diff.astype(BF16), w_ref[g], preferred_element_type=F32) * scale_ref[:, sl]
        y_ref[:, sl] = y.astype(y_ref.dtype)
    hbuf[...] = u[tt - HIST_ROWS:, :]

    @pl.when(i == pl.num_programs(1) - 1)
    def _():
        nh_ref[...] = hbuf[...]


def _pool(u_pool, hist16, w_pool_bf, pool_scale, pos0):
    b, l, dp = u_pool.shape
    tt = min(l, 256)
    g, pc, _ = w_pool_bf.shape
    kern = functools.partial(_pool_kernel, tt=tt, pos0=pos0)
    return pl.pallas_call(
        kern,
        grid=(b, l // tt),
        in_specs=[pl.BlockSpec((None, tt, dp), lambda bi, i: (bi, i, 0)),
                  pl.BlockSpec((None, HIST_ROWS, dp), lambda bi, i: (bi, 0, 0)),
                  pl.BlockSpec((g, pc, pc), lambda bi, i: (0, 0, 0)),
                  pl.BlockSpec((1, dp), lambda bi, i: (0, 0))],
        out_specs=[pl.BlockSpec((None, tt, dp), lambda bi, i: (bi, i, 0)),
                   pl.BlockSpec((None, HIST_ROWS, dp), lambda bi, i: (bi, 0, 0))],
        out_shape=[jax.ShapeDtypeStruct((b, l, dp), BF16), jax.ShapeDtypeStruct((b, HIST_ROWS, dp), F32)],
        scratch_shapes=[pltpu.VMEM((HIST_ROWS, dp), F32)],
        compiler_params=_params(("arbitrary", "arbitrary"), 2 * (tt * dp * 6 + g * pc * pc * 2 + 2 * HIST_ROWS * dp * 4)),
        name="pool_mixer",
    )(u_pool, hist16, w_pool_bf, pool_scale)


SSM_PAIR_GROUP = 8


def _ssm_kernel(u_ref, bc_ref, cc_ref, wg_ref, ar_ref, ai_ref, dsk_ref, bgl_ref, h0_ref,
                y_ref, hn_ref, s_ref, ubuf, hst, *, tt, pitch, nb):
    i = pl.program_id(0)
    n_slab = s_ref.shape[0]
    n_chunk = bc_ref.shape[0]
    slab_per_chunk = n_slab // n_chunk
    cw = bc_ref.shape[1]

    @pl.when(i == 0)
    def _():
        s_ref[...] = jnp.zeros_like(s_ref)
        ubuf[...] = jnp.zeros_like(ubuf)
        hst[...] = h0_ref[...]

    u = u_ref[...].reshape(nb * tt, u_ref.shape[-1])
    u16 = u.astype(BF16)
    for k in range(n_chunk):
        bu = jnp.dot(u16[:, k * cw:(k + 1) * cw], bc_ref[k], preferred_element_type=F32)
        for j in range(slab_per_chunk):
            for b in range(nb):
                s_ref[k * slab_per_chunk + j, b * pitch:b * pitch + tt, :] = (
                    bu[b * tt:(b + 1) * tt, j * LANES:(j + 1) * LANES])
    for b in range(nb):
        ubuf[b * pitch:b * pitch + tt, :] = u[b * tt:(b + 1) * tt, :]

    n_pair = n_slab // 2
    for pg in range(n_pair // SSM_PAIR_GROUP):
        qs = [pg * SSM_PAIR_GROUP + j for j in range(SSM_PAIR_GROUP)]
        ars = [ar_ref[q] for q in qs]
        ais = [ai_ref[q] for q in qs]

        def body(t, carry, qs=qs, ars=ars, ais=ais):
            hr, hi = carry
            nhr, nhi = [], []
            rows = pl.ds(t, nb, stride=pitch)
            for j, q in enumerate(qs):
                bur = s_ref[2 * q, rows, :]
                bui = s_ref[2 * q + 1, rows, :]
                r_ = ars[j] * hr[j] - ais[j] * hi[j] + bur
                i_ = ars[j] * hi[j] + ais[j] * hr[j] + bui
                s_ref[2 * q, rows, :] = r_
                s_ref[2 * q + 1, rows, :] = i_
                nhr.append(r_)
                nhi.append(i_)
            return tuple(nhr), tuple(nhi)

        init = (tuple(hst[2 * q] for q in qs), tuple(hst[2 * q + 1] for q in qs))
        hr, hi = lax.fori_loop(0, tt, body, init)
        for j, q in enumerate(qs):
            hst[2 * q] = hr[j]
            hst[2 * q + 1] = hi[j]

    ycs = []
    for n in range(n_chunk):
        hc = jnp.concatenate([s_ref[n * slab_per_chunk + j] for j in range(slab_per_chunk)], axis=1)
        ycs.append(jnp.dot(hc.astype(BF16), cc_ref[n], preferred_element_type=F32))
    y = jnp.concatenate(ycs, axis=1) + dsk_ref[...] * ubuf[...]
    y = jax.nn.gelu(y)
    y16 = y.astype(BF16)
    z = jnp.concatenate([jnp.dot(y16[:, k * cw:(k + 1) * cw], wg_ref[k], preferred_element_type=F32)
                         for k in range(n_chunk)], axis=1) + bgl_ref[...]
    out = y * jax.nn.sigmoid(z)
    for b in range(nb):
        y_ref[b] = out[b * pitch:b * pitch + tt].astype(y_ref.dtype)

    @pl.when(i == pl.num_programs(0) - 1)
    def _():
        hn_ref[...] = hst[...]


def _ssm(u_ssm, prm, h0):
    b, l, ds = u_ssm.shape
    tt = min(l, 64)
    pitch = tt + 4
    n_slab = h0.shape[0]
    n_chunk = prm["bc"].shape[0]
    kern = functools.partial(_ssm_kernel, tt=tt, pitch=pitch, nb=b)

    def const(a):
        nd = a.ndim
        return pl.BlockSpec(a.shape, lambda i, nd=nd: (0,) * nd)

    args = (u_ssm, prm["bc"], prm["cc"], prm["wg"], prm["ar"], prm["ai"], prm["dsk"], prm["bgl"], h0)
    return pl.pallas_call(
        kern,
        grid=(l // tt,),
        in_specs=[pl.BlockSpec((b, tt, ds), lambda i: (0, i, 0))] + [const(a) for a in args[1:]],
        out_specs=[pl.BlockSpec((b, tt, ds), lambda i: (0, i, 0)),
                   pl.BlockSpec((n_slab, b, LANES), lambda i: (0, 0, 0))],
        out_shape=[jax.ShapeDtypeStruct((b, l, ds), BF16), jax.ShapeDtypeStruct((n_slab, b, LANES), F32)],
        scratch_shapes=[pltpu.VMEM((n_slab, b * pitch, LANES), F32),
                        pltpu.VMEM((b * pitch, ds), F32),
                        pltpu.VMEM((n_slab, b, LANES), F32)],
        compiler_params=_params(("arbitrary",), (n_slab * LANES + ds) * b * pitch * 4 + 2 * b * tt * ds * 6
                                + 2 * sum(a.size * a.dtype.itemsize for a in args[1:])),
        name="ssm_mixer",
    )(*args)


def _ssm_params(lambda_re, lambda_im, log_dt, b_re, b_im, c_re, c_im, d_skip, w_glu, b_glu, nb):
    g, p = lambda_re.shape
    ch = b_re.shape[-1]
    gpc = MXU_DIM // ch
    n_chunk = g // gpc
    ppc = gpc // 2
    lr, li = lambda_re.astype(F32), lambda_im.astype(F32)
    dt = jnp.exp(log_dt.astype(F32))[:, None]
    mag = jnp.exp(lr * dt)
    abar_re, abar_im = mag * jnp.cos(li * dt), mag * jnp.sin(li * dt)
    nr, ni = abar_re - 1.0, abar_im
    den = lr * lr + li * li
    k_re = (nr * lr + ni * li) / den
    k_im = (ni * lr - nr * li) / den
    br, bi = b_re.astype(F32), b_im.astype(F32)
    bb_re = k_re[..., None] * br - k_im[..., None] * bi
    bb_im = k_re[..., None] * bi + k_im[..., None] * br
    eye_2 = jnp.eye(2, dtype=F32)

    def pair_block_diag(blocks):
        _, _, r, c = blocks.shape
        out = jnp.zeros((n_chunk, ppc * r, ppc * c), blocks.dtype)
        for q in range(ppc):
            out = out.at[:, q * r:(q + 1) * r, q * c:(q + 1) * c].set(blocks[:, q])
        return out

    bb = jnp.stack([bb_re, bb_im]).reshape(2, n_chunk, ppc, 2, p, ch)
    bc = pair_block_diag(jnp.einsum("rkqgpc,gy->kqgcryp", bb, eye_2).astype(BF16)
                         .reshape(n_chunk, ppc, 2 * ch, 2 * 2 * p))
    cm = jnp.stack([c_re.astype(F32), -c_im.astype(F32)]).reshape(2, n_chunk, ppc, 2, ch, p)
    cc = pair_block_diag(jnp.einsum("rkqgcp,gy->kqrgpyc", cm, eye_2).astype(BF16)
                         .reshape(n_chunk, ppc, 2 * 2 * p, 2 * ch))
    wg = jnp.einsum("kgce,gx->kgcxe", w_glu.astype(F32).reshape(n_chunk, gpc, ch, ch),
                    jnp.eye(gpc, dtype=F32)).reshape(n_chunk, gpc * ch, gpc * ch)
    n_pair = g // 2
    ar = jnp.broadcast_to(abar_re.reshape(n_pair, 1, 2 * p), (n_pair, nb, 2 * p))
    ai = jnp.broadcast_to(abar_im.reshape(n_pair, 1, 2 * p), (n_pair, nb, 2 * p))
    return dict(bc=bc.astype(BF16), cc=cc.astype(BF16), wg=wg.astype(BF16), ar=ar, ai=ai,
                dsk=d_skip.astype(F32).reshape(1, g * ch), bgl=b_glu.astype(F32).reshape(1, g * ch))


def _state_to_slabs(s_re, s_im):
    b, g, p = s_re.shape
    re = s_re.astype(F32).reshape(b, g // 2, 2 * p).transpose(1, 0, 2)
    im = s_im.astype(F32).reshape(b, g // 2, 2 * p).transpose(1, 0, 2)
    return jnp.stack([re, im], axis=1).reshape(g, b, 2 * p)


def _slabs_to_state(h, p):
    n_slab, b, _ = h.shape
    h = h.reshape(n_slab // 2, 2, b, 2 * p)
    re = h[:, 0].transpose(1, 0, 2).reshape(b, n_slab, p)
    im = h[:, 1].transpose(1, 0, 2).reshape(b, n_slab, p)
    return re, im


def _outproj_kernel(yp_ref, ys_ref, x_ref, g1_ref, sc2_ref, sh2_ref, w_ref, lng_ref, lnb_ref, wr_ref, br_ref,
                    x1_ref, hp_ref, idx_ref, gate_ref, cnt_ref, *, n_experts):
    first = jnp.logical_and(pl.program_id(0) == 0, pl.program_id(1) == 0)
    ymix = jnp.concatenate([yp_ref[...], ys_ref[...]], axis=1)
    mix = jnp.dot(ymix, w_ref[...], preferred_element_type=F32)
    x1 = _layer_norm(DN_ALPHA * x_ref[...] + g1_ref[...] * mix, lng_ref[...], lnb_ref[...])
    x1_ref[...] = x1
    h2 = x1 * (1.0 + sc2_ref[...]) + sh2_ref[...]
    hp_ref[...] = h2
    logits = jnp.dot(h2.astype(BF16), wr_ref[...], preferred_element_type=F32) + br_ref[...]
    tm = logits.shape[0]
    lane = lax.broadcasted_iota(I32, (tm, LANES), 1).astype(F32)
    cur = jnp.where(lane < n_experts, logits, -jnp.inf)
    vals, onehot = [], jnp.zeros((tm, LANES), F32)
    for k in range(TOP_K):
        m = jnp.max(cur, axis=1, keepdims=True)
        sel = jnp.min(jnp.where(cur == m, lane, float(LANES)), axis=1, keepdims=True)
        hit = lane == sel
        idx_ref[:, k:k + 1] = sel.astype(I32)
        vals.append(m)
        onehot = onehot + jnp.where(hit, 1.0, 0.0)
        cur = jnp.where(hit, -jnp.inf, cur)
    es = [jnp.exp(v - vals[0]) for v in vals]
    den = es[0] + es[1] + es[2] + es[3]
    for k in range(TOP_K):
        gate_ref[:, k:k + 1] = es[k] / den

    @pl.when(first)
    def _():
        cnt_ref[...] = jnp.zeros_like(cnt_ref)

    cnt_ref[0:1, :] += jnp.sum(onehot, axis=0, keepdims=True)


def _outproj(y_pool, y_ssm, x, g1, sc2, sh2, w_out_bf, ln_g, ln_b, wr_bf, br_pad, n_experts):
    b, l, d = x.shape
    dp = y_pool.shape[-1]
    tm = min(l, 256)
    nl = l // tm
    t = b * l
    row = pl.BlockSpec((None, tm, d), lambda bi, i: (bi, i, 0))
    halfrow = pl.BlockSpec((None, tm, dp), lambda bi, i: (bi, i, 0))
    mod = pl.BlockSpec((None, 1, d), lambda bi, i: (bi, 0, 0))
    vec = pl.BlockSpec((1, d), lambda bi, i: (0, 0))
    tok4 = pl.BlockSpec((tm, TOP_K), lambda bi, i: (bi * nl + i, 0))
    kern = functools.partial(_outproj_kernel, n_experts=n_experts)
    return pl.pallas_call(
        kern,
        grid=(b, nl),
        in_specs=[halfrow, halfrow, row, mod, mod, mod,
                  pl.BlockSpec((d, d), lambda bi, i: (0, 0)), vec, vec,
                  pl.BlockSpec((d, LANES), lambda bi, i: (0, 0)),
                  pl.BlockSpec((1, LANES), lambda bi, i: (0, 0))],
        out_specs=[row,
                   pl.BlockSpec((tm, d), lambda bi, i: (bi * nl + i, 0)),
                   tok4, tok4,
                   pl.BlockSpec((SUBLANES, LANES), lambda bi, i: (0, 0))],
        out_shape=[jax.ShapeDtypeStruct((b, l, d), F32),
                   jax.ShapeDtypeStruct((t, d), F32),
                   jax.ShapeDtypeStruct((t, TOP_K), I32),
                   jax.ShapeDtypeStruct((t, TOP_K), F32),
                   jax.ShapeDtypeStruct((SUBLANES, LANES), F32)],
        compiler_params=_params(("arbitrary", "arbitrary"), 2 * (d * d * 2 + d * LANES * 2 + tm * d * 14)),
        name="out_proj_router",
    )(y_pool, y_ssm, x, g1, sc2, sh2, w_out_bf, ln_g, ln_b, wr_bf, br_pad)


def _plan_kernel(idx_ref, start_ref, pos_ref, carry):
    i = pl.program_id(0)

    @pl.when(i == 0)
    def _():
        carry[...] = jnp.zeros_like(carry)

    idx = idx_ref[...]
    tp = idx.shape[0]
    lane = lax.broadcasted_iota(I32, (tp, LANES), 1)
    hits = [lane == idx[:, k:k + 1] for k in range(TOP_K)]
    onehot = jnp.zeros((tp, LANES), F32)
    for h in hits:
        onehot = onehot + jnp.where(h, 1.0, 0.0)
    below = jnp.where(lax.broadcasted_iota(I32, (tp, tp), 0) > lax.broadcasted_iota(I32, (tp, tp), 1),
                      1.0, 0.0).astype(BF16)
    excl = jnp.dot(below, onehot.astype(BF16), preferred_element_type=F32)
    slot = start_ref[0:1, :] + carry[0:1, :] + excl
    for k in range(TOP_K):
        pos_ref[:, k:k + 1] = jnp.sum(jnp.where(hits[k], slot, 0.0), axis=1, keepdims=True).astype(I32)
    carry[0:1, :] += jnp.sum(onehot, axis=0, keepdims=True)


def _plan(idx_all, start_pad):
    t = idx_all.shape[0]
    tp = 128
    return pl.pallas_call(
        _plan_kernel,
        grid=(t // tp,),
        in_specs=[pl.BlockSpec((tp, TOP_K), lambda i: (i, 0)),
                  pl.BlockSpec((SUBLANES, LANES), lambda i: (0, 0))],
        out_specs=pl.BlockSpec((tp, TOP_K), lambda i: (i, 0)),
        out_shape=jax.ShapeDtypeStruct((t, TOP_K), I32),
        scratch_shapes=[pltpu.VMEM((SUBLANES, LANES), F32)],
        compiler_params=_params(("arbitrary",), 4 * tp * LANES * 4),
        name="moe_plan",
    )(idx_all, start_pad)


DISPATCH_UNROLL = 4


def _dispatch_kernel(fill_ref, posp_ref, poss_ref, hp_ref, hs_ref, xs_ref, zbuf, sem, zsem, *, n_p_steps, n_blk):
    i = pl.program_id(0)

    def fill_copy(b):
        return pltpu.make_async_copy(zbuf, xs_ref.at[pl.ds(pl.multiple_of(b * ROW_BLK, ROW_BLK), ROW_BLK)], zsem)

    @pl.when(i == 0)
    def _():
        zbuf[...] = jnp.zeros_like(zbuf)

        def start(b, carry):
            @pl.when(fill_ref[b] != 0)
            def _():
                fill_copy(b).start()
            return carry

        def wait(b, carry):
            @pl.when(fill_ref[b] != 0)
            def _():
                fill_copy(b).wait()
            return carry

        lax.fori_loop(0, n_blk, start, 0)
        lax.fori_loop(0, n_blk, wait, 0)

    def scatter(src_ref, pos_ref):
        td = src_ref.shape[0]

        def body(t2, carry):
            for tt in range(DISPATCH_UNROLL):
                t = t2 * DISPATCH_UNROLL + tt
                for k in range(TOP_K):
                    p = pos_ref[0, 0, t * TOP_K + k]
                    pltpu.make_async_copy(src_ref.at[pl.ds(t, 1)], xs_ref.at[pl.ds(p, 1)], sem).start()
            return carry

        lax.fori_loop(0, td // DISPATCH_UNROLL, body, 0)
        for k in range(TOP_K):
            pltpu.make_async_copy(src_ref, xs_ref.at[pl.ds(0, td)], sem).wait()

    @pl.when(i < n_p_steps)
    def _():
        scatter(hp_ref, posp_ref)

    @pl.when(i >= n_p_steps)
    def _():
        scatter(hs_ref, poss_ref)


def _dispatch(pos_p, pos_s, h_prompt, h_sample, fill_flags, n_rows):
    tp, w = h_prompt.shape
    ts = h_sample.shape[0]
    tdp, tds = math.gcd(tp, 512), math.gcd(ts, 512)
    n_p_steps, n_s_steps = tp // tdp, ts // tds
    n_blk = fill_flags.shape[0]
    grid_spec = pltpu.PrefetchScalarGridSpec(
        num_scalar_prefetch=1,
        grid=(n_p_steps + n_s_steps,),
        in_specs=[pl.BlockSpec((1, 1, tdp * TOP_K), lambda i, f: (jnp.minimum(i, n_p_steps - 1), 0, 0),
                               memory_space=pltpu.SMEM),
                  pl.BlockSpec((1, 1, tds * TOP_K), lambda i, f: (jnp.maximum(i - n_p_steps, 0), 0, 0),
                               memory_space=pltpu.SMEM),
                  pl.BlockSpec((tdp, w), lambda i, f: (jnp.minimum(i, n_p_steps - 1), 0)),
                  pl.BlockSpec((tds, w), lambda i, f: (jnp.maximum(i - n_p_steps, 0), 0))],
        out_specs=pl.BlockSpec(memory_space=pl.ANY),
        scratch_shapes=[pltpu.VMEM((ROW_BLK, w), F32),
                        pltpu.SemaphoreType.DMA(()), pltpu.SemaphoreType.DMA(())],
    )
    need = 2 * (tdp + tds) * w * 4 + ROW_BLK * w * 4
    return pl.pallas_call(
        functools.partial(_dispatch_kernel, n_p_steps=n_p_steps, n_blk=n_blk),
        grid_spec=grid_spec,
        out_shape=jax.ShapeDtypeStruct((n_rows, w), F32),
        compiler_params=_params(("arbitrary",), need),
        name="moe_dispatch",
    )(fill_flags, pos_p.reshape(n_p_steps, 1, tdp * TOP_K), pos_s.reshape(n_s_steps, 1, tds * TOP_K),
      h_prompt, h_sample)


def _moe_kernel(ie_ref, ir_ref, inb_ref, meta_ref, xs_ref, wg_ref, wu_ref, bg_ref, bu_ref, wd_ref, bd_ref,
                ys_ref, xbuf, act, ystage, zbuf, yflag, xsem, ysem, zsem, *, nc, n_blk):
    i = pl.program_id(0)
    c = pl.program_id(1)
    n_items = meta_ref[0]
    used_blk = meta_ref[1]
    valid = i < n_items
    nb = inb_ref[i]
    row0 = pl.multiple_of(ir_ref[i], ROW_BLK)
    n_quad = lax.shift_right_logical(nb, 3)
    has_pair = (nb & 4) != 0
    has_rb = (nb & 2) != 0
    has_blk = (nb & 1) != 0
    base_pair = pl.multiple_of(n_quad * 4 * MOE_RB, ROW_BLK)
    base_rb = pl.multiple_of(base_pair + jnp.where(has_pair, 2 * MOE_RB, 0), ROW_BLK)
    base_blk = pl.multiple_of(base_rb + jnp.where(has_rb, MOE_RB, 0), ROW_BLK)
    first = jnp.logical_and(i == 0, c == 0)
    last = jnp.logical_and(i == pl.num_programs(0) - 1, c == pl.num_programs(1) - 1)

    def x_copy(it, j):
        src0 = pl.multiple_of(ir_ref[it] + j * ROW_BLK, ROW_BLK)
        return pltpu.make_async_copy(xs_ref.at[pl.ds(src0, ROW_BLK)],
                                     xbuf.at[pl.ds(pl.multiple_of(j * ROW_BLK, ROW_BLK), ROW_BLK)], xsem)

    def x_start(it):
        def start(j, carry):
            x_copy(it, j).start()
            return carry
        lax.fori_loop(0, inb_ref[it], start, 0)

    def x_wait(it):
        def wait(j, carry):
            x_copy(it, j).wait()
            return carry
        lax.fori_loop(0, inb_ref[it], wait, 0)

    def tail_copy(b, col):
        return pltpu.make_async_copy(
            zbuf, ys_ref.at[pl.ds(pl.multiple_of(b * ROW_BLK, ROW_BLK), ROW_BLK), pl.ds(col * MOE_CW, MOE_CW)], zsem)

    n_col = ys_ref.shape[1] // MOE_CW

    @pl.when(first)
    def _():
        x_start(0)
        zbuf[...] = jnp.zeros_like(zbuf)
        for slot in range(YS_SLOTS):
            yflag[slot] = 0

        def start(b, carry):
            for col in range(n_col):
                tail_copy(b, col).start()
            return carry
        lax.fori_loop(used_blk, n_blk, start, 0)

    @pl.when(jnp.logical_and(valid, c == 0))
    def _():
        x_wait(i)

    @pl.when(jnp.logical_and(c == nc, i + 1 < n_items))
    def _():
        x_start(i + 1)

    @pl.when(jnp.logical_and(valid, c < nc))
    def _():
        def piece(r0, rows):
            w, bias = [], []
            for s in range(MOE_CW // LANES):
                cols = slice(s * LANES, (s + 1) * LANES)
                w += [wg_ref[:, cols].astype(BF16), wu_ref[:, cols].astype(BF16)]
                bias += [bg_ref[:, cols], bu_ref[:, cols]]
            x = xbuf[pl.ds(r0, rows), :].astype(BF16)
            gu = (jnp.dot(x, jnp.concatenate(w, axis=1), preferred_element_type=F32)
                  + jnp.concatenate(bias, axis=1))
            for s in range(MOE_CW // LANES):
                gate = jnp.minimum(gu[:, 2 * s * LANES:(2 * s + 1) * LANES], SWIGLU_LIMIT)
                up = jnp.clip(gu[:, (2 * s + 1) * LANES:(2 * s + 2) * LANES], -SWIGLU_LIMIT, SWIGLU_LIMIT)
                a = (up + 1.0) * (gate * jax.nn.sigmoid(SWIGLU_ALPHA * gate))
                act[c, pl.ds(r0, rows), s * LANES:(s + 1) * LANES] = a.astype(BF16)

        def quad(q, carry):
            r0 = pl.multiple_of(q * 4 * MOE_RB, ROW_BLK)
            for j in range(4):
                piece(pl.multiple_of(r0 + j * MOE_RB, ROW_BLK), MOE_RB)
            return carry

        lax.fori_loop(0, n_quad, quad, 0)

        @pl.when(has_pair)
        def _():
            for j in range(2):
                piece(pl.multiple_of(base_pair + j * MOE_RB, ROW_BLK), MOE_RB)

        @pl.when(has_rb)
        def _():
            piece(base_rb, MOE_RB)

        @pl.when(has_blk)
        def _():
            piece(base_blk, ROW_BLK)

    slot_rows = [MOE_RB] * (YS_SLOTS - 1) + [ROW_BLK]

    def y_copy(slot, r0, col0):
        dst_row = pl.multiple_of(row0 + r0, ROW_BLK)
        return pltpu.make_async_copy(ystage.at[slot, pl.ds(0, slot_rows[slot])],
                                     ys_ref.at[pl.ds(dst_row, slot_rows[slot]), pl.ds(col0, MOE_CW)],
                                     ysem.at[slot])

    def y_drain_all():
        for slot in range(YS_SLOTS):
            @pl.when(yflag[slot] != 0)
            def _(slot=slot):
                y_copy(slot, 0, 0).wait()
                yflag[slot] = 0

    @pl.when(jnp.logical_and(valid, c >= nc))
    def _():
        col0 = pl.multiple_of((c - nc) * MOE_CW, MOE_CW)
        y_drain_all()

        def compute(r0, rows):
            a = jnp.concatenate([act[cc, pl.ds(r0, rows), :] for cc in range(nc)], axis=1)
            return jnp.dot(a, wd_ref[...].astype(BF16), preferred_element_type=F32) + bd_ref[...]

        def group(slots, base):
            for j, slot in enumerate(slots):
                ystage[slot, 0:slot_rows[slot], :] = compute(pl.multiple_of(base + j * MOE_RB, ROW_BLK),
                                                             slot_rows[slot])
            for j, slot in enumerate(slots):
                y_copy(slot, base + j * MOE_RB, col0).start()

        def quad(q, carry):
            @pl.when(q > 0)
            def _():
                for slot in (0, 1, 2, 3):
                    y_copy(slot, 0, 0).wait()

            group((0, 1, 2, 3), pl.multiple_of(q * 4 * MOE_RB, ROW_BLK))
            return carry

        lax.fori_loop(0, n_quad, quad, 0)

        @pl.when(n_quad > 0)
        def _():
            for slot in (0, 1, 2, 3):
                yflag[slot] = 1

        @pl.when(has_pair)
        def _():
            group((4, 5), base_pair)
            yflag[4] = 1
            yflag[5] = 1

        @pl.when(has_rb)
        def _():
            group((6,), base_rb)
            yflag[6] = 1

        @pl.when(has_blk)
        def _():
            group((7,), base_blk)
            yflag[7] = 1

    @pl.when(last)
    def _():
        def wait(b, carry):
            for col in range(n_col):
                tail_copy(b, col).wait()
            return carry
        lax.fori_loop(used_blk, n_blk, wait, 0)
        y_drain_all()


def _moe(xs, item_e, item_row0, item_nblk, meta, w_gate_up, b_gate_up, w_down, b_down):
    n_rows, d = xs.shape
    n_exp, _, ff2 = w_gate_up.shape
    ff = ff2 // 2
    assert w_gate_up.shape[1] == d and w_down.shape[1:] == (ff, d) and ff == d
    nc = ff // MOE_CW
    ni = item_e.shape[0]
    rbuf = MOE_MAX_BLK * ROW_BLK

    def item(i, n):
        return jnp.minimum(i, n[0] - 1)

    def c1(i, c, n):
        return jnp.where(i < n[0], jnp.minimum(c, nc - 1), nc - 1)

    def c2(i, c, n):
        return jnp.where(i < n[0], jnp.maximum(c - nc, 0), nc - 1)

    bgu3 = b_gate_up.reshape(n_exp, 1, ff2)
    bd3 = b_down.reshape(n_exp, 1, d)
    in_specs = [
        pl.BlockSpec(memory_space=pl.ANY),
        pl.BlockSpec((None, d, MOE_CW), lambda i, c, e, r, b, n: (e[item(i, n)], 0, c1(i, c, n))),
        pl.BlockSpec((None, d, MOE_CW), lambda i, c, e, r, b, n: (e[item(i, n)], 0, nc + c1(i, c, n))),
        pl.BlockSpec((None, 1, MOE_CW), lambda i, c, e, r, b, n: (e[item(i, n)], 0, c1(i, c, n))),
        pl.BlockSpec((None, 1, MOE_CW), lambda i, c, e, r, b, n: (e[item(i, n)], 0, nc + c1(i, c, n))),
        pl.BlockSpec((None, ff, MOE_CW), lambda i, c, e, r, b, n: (e[item(i, n)], 0, c2(i, c, n))),
        pl.BlockSpec((None, 1, MOE_CW), lambda i, c, e, r, b, n: (e[item(i, n)], 0, c2(i, c, n))),
    ]
    grid_spec = pltpu.PrefetchScalarGridSpec(
        num_scalar_prefetch=4,
        grid=(ni, 2 * nc),
        in_specs=in_specs,
        out_specs=pl.BlockSpec(memory_space=pl.ANY),
        scratch_shapes=[pltpu.VMEM((rbuf, d), F32),
                        pltpu.VMEM((nc, rbuf, MOE_CW), BF16),
                        pltpu.VMEM((YS_SLOTS, MOE_RB, MOE_CW), F32),
                        pltpu.VMEM((ROW_BLK, MOE_CW), F32),
                        pltpu.SMEM((YS_SLOTS,), I32),
                        pltpu.SemaphoreType.DMA(()),
                        pltpu.SemaphoreType.DMA((YS_SLOTS,)),
                        pltpu.SemaphoreType.DMA(())],
    )
    need = (rbuf * d * 4 + rbuf * ff * 2 + YS_SLOTS * MOE_RB * MOE_CW * 4 + ROW_BLK * MOE_CW * 4
            + 2 * 3 * d * MOE_CW * 4)
    return pl.pallas_call(
        functools.partial(_moe_kernel, nc=nc, n_blk=n_rows // ROW_BLK),
        grid_spec=grid_spec,
        out_shape=jax.ShapeDtypeStruct((n_rows, d), F32),
        compiler_params=_params(("arbitrary", "arbitrary"), need),
        name="moe_experts",
    )(item_e, item_row0, item_nblk, meta, xs, w_gate_up, w_gate_up, bgu3, bgu3, w_down, bd3)


def _combine_kernel(pos_ref, posn_ref, ys_ref, gate_ref, x1_ref, g2_ref, lng_ref, lnb_ref, o_ref, gbuf, sem, *, tc):
    i = pl.program_id(0)
    slot = i % 2

    def issue(p_ref, s):
        def body(t2, carry):
            for tt in range(DISPATCH_UNROLL):
                t = t2 * DISPATCH_UNROLL + tt
                for k in range(TOP_K):
                    p = p_ref[0, 0, t * TOP_K + k]
                    pltpu.make_async_copy(ys_ref.at[pl.ds(p, 1)], gbuf.at[s, k, pl.ds(t, 1)], sem.at[s]).start()
            return carry
        lax.fori_loop(0, tc // DISPATCH_UNROLL, body, 0)

    @pl.when(i == 0)
    def _():
        issue(pos_ref, 0)

    @pl.when(i + 1 < pl.num_programs(0))
    def _():
        issue(posn_ref, 1 - slot)

    for k in range(TOP_K):
        pltpu.make_async_copy(ys_ref.at[pl.ds(0, tc)], gbuf.at[slot, k], sem.at[slot]).wait()
    gt = gate_ref[...]
    ffn = gt[:, 0:1] * gbuf[slot, 0]
    for k in range(1, TOP_K):
        ffn = ffn + gt[:, k:k + 1] * gbuf[slot, k]
    o_ref[...] = _layer_norm(DN_ALPHA * x1_ref[...] + g2_ref[...] * ffn, lng_ref[...], lnb_ref[...])


def _combine(pos, ys, gates, x1, g2, ln_g, ln_b):
    b, l, d = x1.shape
    t = b * l
    tc = min(l, 128)
    per_b = l // tc
    n_steps = t // tc
    pos3 = pos.reshape(n_steps, 1, tc * TOP_K)
    vec = pl.BlockSpec((1, d), lambda i: (0, 0))
    out = pl.pallas_call(
        functools.partial(_combine_kernel, tc=tc),
        grid=(n_steps,),
        in_specs=[pl.BlockSpec((1, 1, tc * TOP_K), lambda i: (i, 0, 0), memory_space=pltpu.SMEM),
                  pl.BlockSpec((1, 1, tc * TOP_K), lambda i: (jnp.minimum(i + 1, n_steps - 1), 0, 0),
                               memory_space=pltpu.SMEM),
                  pl.BlockSpec(memory_space=pl.ANY),
                  pl.BlockSpec((tc, TOP_K), lambda i: (i, 0)),
                  pl.BlockSpec((tc, d), lambda i: (i, 0)),
                  pl.BlockSpec((None, 1, d), lambda i: (i // per_b, 0, 0)),
                  vec, vec],
        out_specs=pl.BlockSpec((tc, d), lambda i: (i, 0)),
        out_shape=jax.ShapeDtypeStruct((t, d), F32),
        scratch_shapes=[pltpu.VMEM((2, TOP_K, tc, d), F32), pltpu.SemaphoreType.DMA((2,))],
        compiler_params=_params(("arbitrary",), (2 * TOP_K + 2 * 2) * tc * d * 4),
        name="moe_combine",
    )(pos3, pos3, ys, gates, x1.reshape(t, d), g2, ln_g, ln_b)
    return out.reshape(b, l, d)


def _moe_items(counts, n_blk_total):
    n_exp = counts.shape[0]
    nblk = (counts + ROW_BLK - 1) // ROW_BLK
    blk_end = jnp.cumsum(nblk)
    blk0 = blk_end - nblk
    n_it = (nblk + MOE_MAX_BLK - 1) // MOE_MAX_BLK
    it_end = jnp.cumsum(n_it)
    it0 = it_end - n_it
    ni = n_exp + n_blk_total // MOE_MAX_BLK
    j = jnp.arange(ni, dtype=I32)
    e = jnp.minimum(jnp.sum(j[:, None] >= it_end[None, :], axis=1), n_exp - 1).astype(I32)
    local = j - it0[e]
    item_nblk = jnp.clip(nblk[e] - local * MOE_MAX_BLK, 0, MOE_MAX_BLK).astype(I32)
    item_row0 = ((blk0[e] + local * MOE_MAX_BLK) * ROW_BLK).astype(I32)
    n_items = it_end[-1].astype(I32)
    item_nblk = jnp.where(j < n_items, item_nblk, 0)
    item_row0 = jnp.where(j < n_items, item_row0, 0)
    row_start = (blk0 * ROW_BLK).astype(F32)
    used_blk = blk_end[-1].astype(I32)
    meta = jnp.stack([n_items, used_blk])
    blk = jnp.arange(n_blk_total, dtype=I32)
    partial = jnp.any((blk[:, None] == (blk_end - 1)[None, :]) & ((counts % ROW_BLK) != 0)[None, :], axis=1)
    fill = jnp.logical_or(partial, blk >= used_blk).astype(I32)
    return e, item_row0, item_nblk, meta, row_start, fill


def _mixer_stream(x, mod, hist, h0, pos0, w):
    b, l, d = x.shape
    sh1, sc1, g1, sh2, sc2, g2 = [m.reshape(b, 1, d) for m in jnp.split(mod, 6, axis=-1)]
    u_pool, u_ssm = _inproj(x, sc1, sh1, w["w_in"])
    hist16 = jnp.concatenate([jnp.zeros((b, HIST_ROWS - hist.shape[1], hist.shape[2]), F32), hist.astype(F32)], axis=1)
    y_pool, new_hist = _pool(u_pool, hist16, w["w_pool"], w["pool_scale"], pos0)
    y_ssm, h_new = _ssm(u_ssm, w["ssm"], h0)
    x1, hp, idx, gates, cnt = _outproj(y_pool, y_ssm, x, g1, sc2, sh2, w["w_out"], w["ln1_g"], w["ln1_b"],
                                       w["w_router"], w["b_router"], w["n_experts"])
    s_re, s_im = _slabs_to_state(h_new, SSM_STATE)
    return dict(x1=x1, hp=hp, idx=idx, gates=gates, cnt=cnt, g2=g2,
                hist=new_hist[:, 1:, :], s_re=s_re, s_im=s_im)


def kernel(x_prompt, x_sample, cache_pool, state_ssm_re, state_ssm_im, c_prompt, c_sample, w_ada, b_ada, w_in, w_pool, pool_scale, lambda_re, lambda_im, log_dt, ssm_b_re, ssm_b_im, ssm_c_re, ssm_c_im, d_skip, w_glu, b_glu, w_out, ln1_g, ln1_b, w_router, b_router, w_gate_up, b_gate_up, w_down, b_down, ln2_g, ln2_b):
    assert w_ada.shape[0] == DEPTH
    bp, lp, d = x_prompt.shape
    bs, ls, _ = x_sample.shape
    assert bp == SUBLANES and bs == SUBLANES, "the S5 kernel puts the 8 streams on sublanes"
    n_exp = w_router.shape[-1]
    dp = w_pool.shape[1] * w_pool.shape[2]

    w = dict(
        w_in=w_in[0].astype(BF16),
        w_pool=w_pool[0].astype(BF16),
        pool_scale=pool_scale[0].astype(F32).reshape(1, dp),
        ssm=_ssm_params(lambda_re[0], lambda_im[0], log_dt[0], ssm_b_re[0], ssm_b_im[0], ssm_c_re[0], ssm_c_im[0],
                        d_skip[0], w_glu[0], b_glu[0], SUBLANES),
        w_out=w_out[0].astype(BF16),
        ln1_g=ln1_g[0].reshape(1, d), ln1_b=ln1_b[0].reshape(1, d),
        w_router=jnp.pad(w_router[0], ((0, 0), (0, LANES - n_exp))).astype(BF16),
        b_router=jnp.pad(b_router[0].astype(F32), (0, LANES - n_exp)).reshape(1, LANES),
        n_experts=n_exp,
    )

    mod = _ada(jnp.concatenate([c_prompt, c_sample], axis=0), w_ada[0], b_ada[0].reshape(1, -1))
    n_slab = lambda_re.shape[1]
    p = _mixer_stream(x_prompt, mod[:bp], jnp.zeros((bp, HIST_ROWS - 1, dp), F32),
                      jnp.zeros((n_slab, bp, LANES), F32), 0, w)
    s = _mixer_stream(x_sample, mod[bp:], cache_pool[0], _state_to_slabs(state_ssm_re[0], state_ssm_im[0]),
                      PAST_LEN, w)

    tp, ts = bp * lp, bs * ls
    n_assign = (tp + ts) * TOP_K
    n_blk_total = -(-n_assign // ROW_BLK) + n_exp
    n_rows = n_blk_total * ROW_BLK
    counts = (p["cnt"][0, :n_exp] + s["cnt"][0, :n_exp]).astype(I32)
    item_e, item_row0, item_nblk, meta, row_start, fill = _moe_items(counts, n_blk_total)
    start_pad = jnp.zeros((SUBLANES, LANES), F32).at[0, :n_exp].set(row_start)
    pos = _plan(jnp.concatenate([p["idx"], s["idx"]], axis=0), start_pad)
    pos_p, pos_s = pos[:tp], pos[tp:]
    xs = _dispatch(pos_p, pos_s, p["hp"], s["hp"], fill, n_rows)
    ys = _moe(xs, item_e, item_row0, item_nblk, meta, w_gate_up[0], b_gate_up[0], w_down[0], b_down[0])
    g2l, b2l = ln2_g[0].reshape(1, d), ln2_b[0].reshape(1, d)
    y_p = _combine(pos_p, ys, p["gates"], p["x1"], p["g2"], g2l, b2l)
    y_s = _combine(pos_s, ys, s["gates"], s["x1"], s["g2"], g2l, b2l)

    return (y_p, y_s, p["hist"][None], p["s_re"][None], p["s_im"][None],
            s["hist"][None], s["s_re"][None], s["s_im"][None])
```

```python
import functools
import math

import jax
import jax.numpy as jnp
from jax import lax
from jax.experimental import pallas as pl
from jax.experimental.pallas import tpu as pltpu

F32 = jnp.float32
BF16 = jnp.bfloat16
I32 = jnp.int32
U32 = jnp.uint32

POOL_WINDOWS = (2, 4, 8, 16)
HIST_ROWS = 16
SSM_CH = 16
SSM_STATE = 64
TOP_K = 4
SWIGLU_LIMIT = 7.0
SWIGLU_ALPHA = 1.702
LN_EPS = 1e-5
DEPTH = 1
DN_ALPHA = (2 * DEPTH) ** 0.25
PAST_LEN = 2048

LANES = 128
SUBLANES = 8
MXU_DIM = 256
V7X_VMEM_BYTES = 64 * 1024 * 1024
MIB = 1024 * 1024

ROW_BLK = 256
MOE_RB = 512
MOE_CW = 256
MOE_MAX_BLK = 10
YS_SLOTS = 8


def _params(sem, need_bytes):
    limit = min(need_bytes + 16 * MIB, V7X_VMEM_BYTES - 4 * MIB)
    return pltpu.CompilerParams(dimension_semantics=sem, vmem_limit_bytes=limit)


def _layer_norm(v, g, b):
    mu = jnp.mean(v, axis=-1, keepdims=True)
    vc = v - mu
    var = jnp.mean(vc * vc, axis=-1, keepdims=True)
    return vc * lax.rsqrt(var + LN_EPS) * g + b


def _ada_kernel(c_ref, w_ref, b_ref, o_ref):
    c = c_ref[...]
    s = c * jax.nn.sigmoid(c)
    o_ref[...] = jnp.dot(s.astype(BF16), w_ref[...].astype(BF16), preferred_element_type=F32) + b_ref[...]


def _ada(c_all, w_ada, b_ada):
    r, d = c_all.shape
    n = w_ada.shape[1]
    tn = 1536
    return pl.pallas_call(
        _ada_kernel,
        grid=(n // tn,),
        in_specs=[pl.BlockSpec((r, d), lambda j: (0, 0)),
                  pl.BlockSpec((d, tn), lambda j: (0, j)),
                  pl.BlockSpec((1, tn), lambda j: (0, j))],
        out_specs=pl.BlockSpec((r, tn), lambda j: (0, j)),
        out_shape=jax.ShapeDtypeStruct((r, n), F32),
        compiler_params=_params(("arbitrary",), 2 * (d * tn * 4 + r * (d + tn) * 4)),
        name="ada_mod",
    )(c_all, w_ada, b_ada)


def _inproj_kernel(x_ref, sc_ref, sh_ref, w_ref, up_ref, us_ref):
    h = x_ref[...] * (1.0 + sc_ref[...]) + sh_ref[...]
    u = jnp.dot(h.astype(BF16), w_ref[...], preferred_element_type=F32)
    dp = up_ref.shape[-1]
    up_ref[...] = u[:, :dp]
    us_ref[...] = u[:, dp:]


def _inproj(x, sc, sh, w_bf):
    b, l, d = x.shape
    dm = w_bf.shape[1]
    dp = dm // 2
    tm = min(l, 512)
    row = pl.BlockSpec((None, tm, d), lambda bi, i: (bi, i, 0))
    mod = pl.BlockSpec((None, 1, d), lambda bi, i: (bi, 0, 0))
    half = pl.BlockSpec((None, tm, dp), lambda bi, i: (bi, i, 0))
    return pl.pallas_call(
        _inproj_kernel,
        grid=(b, l // tm),
        in_specs=[row, mod, mod, pl.BlockSpec((d, dm), lambda bi, i: (0, 0))],
        out_specs=[half, half],
        out_shape=[jax.ShapeDtypeStruct((b, l, dp), F32), jax.ShapeDtypeStruct((b, l, dm - dp), F32)],
        compiler_params=_params(("arbitrary", "arbitrary"), 2 * (d * dm * 2 + tm * (d + dm) * 4)),
        name="in_proj",
    )(x, sc, sh, w_bf)


def _pool_kernel(u_ref, hist_ref, w_ref, scale_ref, y_ref, nh_ref, hbuf, *, tt, pos0):
    i = pl.program_id(1)

    @pl.when(i == 0)
    def _():
        hbuf[...] = hist_ref[...]

    u = u_ref[...]
    hb = hbuf[...]
    u16 = u.astype(BF16)
    h16 = hb.astype(BF16)
    pc = u.shape[1] // len(POOL_WINDOWS)
    d_main = lax.broadcasted_iota(I32, (tt, tt), 0) - lax.broadcasted_iota(I32, (tt, tt), 1)
    d_hist = (lax.broadcasted_iota(I32, (tt, HIST_ROWS), 0) + HIST_ROWS
              - lax.broadcasted_iota(I32, (tt, HIST_ROWS), 1))
    pos = (pos0 + i * tt + lax.broadcasted_iota(I32, (tt, 1), 0)).astype(F32)
    for g, w in enumerate(POOL_WINDOWS):
        sl = slice(g * pc, (g + 1) * pc)
        bm = jnp.where(d_main >= 0, jnp.where(d_main < w, 1.0, 0.0), 0.0).astype(BF16)
        bh = jnp.where(d_hist < w, 1.0, 0.0).astype(BF16)
        s = (jnp.dot(bm, u16[:, sl], preferred_element_type=F32)
             + jnp.dot(bh, h16[:, sl], preferred_element_type=F32))
        cnt = jnp.minimum(pos + 1.0, float(w))
        diff = s / cnt - u[:, sl]
        y = jnp.dot(diff.astype(BF16), w_ref[g], preferred_element_type=F32) * scale_ref[:, sl]
        y_ref[:, sl] = y.astype(y_ref.dtype)
    hbuf[...] = u[tt - HIST_ROWS:, :]

    @pl.when(i == pl.num_programs(1) - 1)
    def _():
        nh_ref[...] = hbuf[...]


def _pool(u_pool, hist16, w_pool_bf, pool_scale, pos0):
    b, l, dp = u_pool.shape
    tt = min(l, 256)
    g, pc, _ = w_pool_bf.shape
    kern = functools.partial(_pool_kernel, tt=tt, pos0=pos0)
    return pl.pallas_call(
        kern,
        grid=(b, l // tt),
        in_specs=[pl.BlockSpec((None, tt, dp), lambda bi, i: (bi, i, 0)),
                  pl.BlockSpec((None, HIST_ROWS, dp), lambda bi, i: (bi, 0, 0)),
                  pl.BlockSpec((g, pc, pc), lambda bi, i: (0, 0, 0)),
                  pl.BlockSpec((1, dp), lambda bi, i: (0, 0))],
        out_specs=[pl.BlockSpec((None, tt, dp), lambda bi, i: (bi, i, 0)),
                   pl.BlockSpec((None, HIST_ROWS, dp), lambda bi, i: (bi, 0, 0))],
        out_shape=[jax.ShapeDtypeStruct((b, l, dp), BF16), jax.ShapeDtypeStruct((b, HIST_ROWS, dp), F32)],
        scratch_shapes=[pltpu.VMEM((HIST_ROWS, dp), F32)],
        compiler_params=_params(("arbitrary", "arbitrary"), 2 * (tt * dp * 6 + g * pc * pc * 2 + 2 * HIST_ROWS * dp * 4)),
        name="pool_mixer",
    )(u_pool, hist16, w_pool_bf, pool_scale)


SSM_PAIR_GROUP = 8
SSM_TIME_UNROLL = 2


def _ssm_kernel(u_ref, bq_ref, cq_ref, wg_ref, ar_ref, ai_ref, dsk_ref, bgl_ref, h0_ref,
                y_ref, hn_ref, s_ref, ubuf, hst, bc_ref, cc_ref, *, tt, pitch, nb):
    i = pl.program_id(0)
    n_slab = s_ref.shape[0]
    n_chunk, ppc, rb, cb = bq_ref.shape
    slab_per_chunk = n_slab // n_chunk
    cw = bc_ref.shape[1]

    @pl.when(i == 0)
    def _():
        s_ref[...] = jnp.zeros_like(s_ref)
        ubuf[...] = jnp.zeros_like(ubuf)
        hst[...] = h0_ref[...]
        bc_ref[...] = jnp.zeros_like(bc_ref)
        lane = lax.broadcasted_iota(I32, (rb, cw), 1) - lax.broadcasted_iota(I32, (rb, cw), 0)
        for k in range(n_chunk):
            for q in range(ppc):
                bc_ref[k, q * rb:(q + 1) * rb, q * cb:(q + 1) * cb] = bq_ref[k, q]
                place = jnp.where(lane == q * rb, 1.0, 0.0).astype(BF16)
                cc_ref[k, q * cb:(q + 1) * cb, :] = jnp.dot(cq_ref[k, q], place,
                                                             preferred_element_type=F32).astype(BF16)

    u = u_ref[...].reshape(nb * tt, u_ref.shape[-1])
    u16 = u.astype(BF16)
    for k in range(n_chunk):
        bu = jnp.dot(u16[:, k * cw:(k + 1) * cw], bc_ref[k], preferred_element_type=F32)
        for j in range(slab_per_chunk):
            for b in range(nb):
                s_ref[k * slab_per_chunk + j, b * pitch:b * pitch + tt, :] = (
                    bu[b * tt:(b + 1) * tt, j * LANES:(j + 1) * LANES])
    for b in range(nb):
        ubuf[b * pitch:b * pitch + tt, :] = u[b * tt:(b + 1) * tt, :]

    n_pair = n_slab // 2
    for pg in range(n_pair // SSM_PAIR_GROUP):
        qs = [pg * SSM_PAIR_GROUP + j for j in range(SSM_PAIR_GROUP)]
        ars = [ar_ref[q] for q in qs]
        ais = [ai_ref[q] for q in qs]

        def body(t2, carry, qs=qs, ars=ars, ais=ais):
            hr, hi = list(carry[0]), list(carry[1])
            for dt in range(SSM_TIME_UNROLL):
                rows = pl.ds(t2 * SSM_TIME_UNROLL + dt, nb, stride=pitch)
                for j, q in enumerate(qs):
                    bur = s_ref[2 * q, rows, :]
                    bui = s_ref[2 * q + 1, rows, :]
                    r_ = ars[j] * hr[j] - ais[j] * hi[j] + bur
                    i_ = ars[j] * hi[j] + ais[j] * hr[j] + bui
                    s_ref[2 * q, rows, :] = r_
                    s_ref[2 * q + 1, rows, :] = i_
                    hr[j], hi[j] = r_, i_
            return tuple(hr), tuple(hi)

        init = (tuple(hst[2 * q] for q in qs), tuple(hst[2 * q + 1] for q in qs))
        hr, hi = lax.fori_loop(0, tt // SSM_TIME_UNROLL, body, init)
        for j, q in enumerate(qs):
            hst[2 * q] = hr[j]
            hst[2 * q + 1] = hi[j]

    ycs = []
    for n in range(n_chunk):
        hc = jnp.concatenate([s_ref[n * slab_per_chunk + j] for j in range(slab_per_chunk)], axis=1)
        ycs.append(jnp.dot(hc.astype(BF16), cc_ref[n], preferred_element_type=F32))
    y = jnp.concatenate(ycs, axis=1) + dsk_ref[...] * ubuf[...]
    y = jax.nn.gelu(y)
    y16 = y.astype(BF16)
    z = jnp.concatenate([jnp.dot(y16[:, k * cw:(k + 1) * cw], wg_ref[k], preferred_element_type=F32)
                         for k in range(n_chunk)], axis=1) + bgl_ref[...]
    out = y * jax.nn.sigmoid(z)
    for b in range(nb):
        y_ref[b] = out[b * pitch:b * pitch + tt].astype(y_ref.dtype)

    @pl.when(i == pl.num_programs(0) - 1)
    def _():
        hn_ref[...] = hst[...]


def _ssm(u_ssm, prm, h0):
    b, l, ds = u_ssm.shape
    tt = min(l, 64)
    pitch = tt + 4
    n_slab = h0.shape[0]
    n_chunk, ppc, rb, cb = prm["bq"].shape
    kern = functools.partial(_ssm_kernel, tt=tt, pitch=pitch, nb=b)

    def const(a):
        nd = a.ndim
        return pl.BlockSpec(a.shape, lambda i, nd=nd: (0,) * nd)

    args = (u_ssm, prm["bq"], prm["cq"], prm["wg"], prm["ar"], prm["ai"], prm["dsk"], prm["bgl"], h0)
    dense = n_chunk * (ppc * rb) * (ppc * cb) * 2
    return pl.pallas_call(
        kern,
        grid=(l // tt,),
        in_specs=[pl.BlockSpec((b, tt, ds), lambda i: (0, i, 0))] + [const(a) for a in args[1:]],
        out_specs=[pl.BlockSpec((b, tt, ds), lambda i: (0, i, 0)),
                   pl.BlockSpec((n_slab, b, LANES), lambda i: (0, 0, 0))],
        out_shape=[jax.ShapeDtypeStruct((b, l, ds), BF16), jax.ShapeDtypeStruct((n_slab, b, LANES), F32)],
        scratch_shapes=[pltpu.VMEM((n_slab, b * pitch, LANES), F32),
                        pltpu.VMEM((b * pitch, ds), F32),
                        pltpu.VMEM((n_slab, b, LANES), F32),
                        pltpu.VMEM((n_chunk, ppc * rb, ppc * cb), BF16),
                        pltpu.VMEM((n_chunk, ppc * cb, ppc * rb), BF16)],
        compiler_params=_params(("arbitrary",), (n_slab * LANES + ds) * b * pitch * 4 + 2 * b * tt * ds * 6
                                + 2 * dense + 2 * sum(a.size * a.dtype.itemsize for a in args[1:])
                                + prm["cq"].size * 2 * (LANES // rb - 1) * 2),
        name="ssm_mixer",
    )(*args)


def _ssm_params(lambda_re, lambda_im, log_dt, b_re, b_im, c_re, c_im, d_skip, w_glu, b_glu, nb):
    g, p = lambda_re.shape
    ch = b_re.shape[-1]
    gpc = MXU_DIM // ch
    n_chunk = g // gpc
    ppc = gpc // 2
    lr, li = lambda_re.astype(F32), lambda_im.astype(F32)
    dt = jnp.exp(log_dt.astype(F32))[:, None]
    mag = jnp.exp(lr * dt)
    abar_re, abar_im = mag * jnp.cos(li * dt), mag * jnp.sin(li * dt)
    nr, ni = abar_re - 1.0, abar_im
    den = lr * lr + li * li
    k_re = (nr * lr + ni * li) / den
    k_im = (ni * lr - nr * li) / den
    br, bi = b_re.astype(F32), b_im.astype(F32)
    bb_re = k_re[..., None] * br - k_im[..., None] * bi
    bb_im = k_re[..., None] * bi + k_im[..., None] * br
    eye_2 = jnp.eye(2, dtype=F32)
    bb = jnp.stack([bb_re, bb_im]).reshape(2, n_chunk, ppc, 2, p, ch)
    bq = jnp.einsum("rkqgpc,gy->kqgcryp", bb, eye_2).astype(BF16).reshape(n_chunk, ppc, 2 * ch, 2 * 2 * p)
    cm = jnp.stack([c_re.astype(F32), -c_im.astype(F32)]).reshape(2, n_chunk, ppc, 2, ch, p)
    cq = jnp.einsum("rkqgcp,gy->kqrgpyc", cm, eye_2).astype(BF16).reshape(n_chunk, ppc, 2 * 2 * p, 2 * ch)
    wg = jnp.einsum("kgce,gx->kgcxe", w_glu.astype(F32).reshape(n_chunk, gpc, ch, ch),
                    jnp.eye(gpc, dtype=F32)).reshape(n_chunk, gpc * ch, gpc * ch)
    n_pair = g // 2
    ar = jnp.broadcast_to(abar_re.reshape(n_pair, 1, 2 * p), (n_pair, nb, 2 * p))
    ai = jnp.broadcast_to(abar_im.reshape(n_pair, 1, 2 * p), (n_pair, nb, 2 * p))
    return dict(bq=bq, cq=cq, wg=wg.astype(BF16), ar=ar, ai=ai,
                dsk=d_skip.astype(F32).reshape(1, g * ch), bgl=b_glu.astype(F32).reshape(1, g * ch))


def _state_to_slabs(s_re, s_im):
    b, g, p = s_re.shape
    re = s_re.astype(F32).reshape(b, g // 2, 2 * p).transpose(1, 0, 2)
    im = s_im.astype(F32).reshape(b, g // 2, 2 * p).transpose(1, 0, 2)
    return jnp.stack([re, im], axis=1).reshape(g, b, 2 * p)


def _slabs_to_state(h, p):
    n_slab, b, _ = h.shape
    h = h.reshape(n_slab // 2, 2, b, 2 * p)
    re = h[:, 0].transpose(1, 0, 2).reshape(b, n_slab, p)
    im = h[:, 1].transpose(1, 0, 2).reshape(b, n_slab, p)
    return re, im


def _outproj_kernel(yp_ref, ys_ref, x_ref, g1_ref, sc2_ref, sh2_ref, w_ref, lng_ref, lnb_ref, wr_ref, br_ref,
                    x1_ref, hp_ref, idx_ref, gate_ref, cnt_ref, *, n_experts):
    first = jnp.logical_and(pl.program_id(0) == 0, pl.program_id(1) == 0)
    ymix = jnp.concatenate([yp_ref[...], ys_ref[...]], axis=1)
    mix = jnp.dot(ymix, w_ref[...], preferred_element_type=F32)
    x1 = _layer_norm(DN_ALPHA * x_ref[...] + g1_ref[...] * mix, lng_ref[...], lnb_ref[...])
    x1_ref[...] = x1
    h2 = x1 * (1.0 + sc2_ref[...]) + sh2_ref[...]
    hp_ref[...] = h2
    logits = jnp.dot(h2.astype(BF16), wr_ref[...], preferred_element_type=F32) + br_ref[...]
    tm = logits.shape[0]
    lane = lax.broadcasted_iota(I32, (tm, LANES), 1).astype(F32)
    cur = jnp.where(lane < n_experts, logits, -jnp.inf)
    vals, onehot = [], jnp.zeros((tm, LANES), F32)
    for k in range(TOP_K):
        m = jnp.max(cur, axis=1, keepdims=True)
        sel = jnp.min(jnp.where(cur == m, lane, float(LANES)), axis=1, keepdims=True)
        hit = lane == sel
        idx_ref[:, k:k + 1] = sel.astype(I32)
        vals.append(m)
        onehot = onehot + jnp.where(hit, 1.0, 0.0)
        cur = jnp.where(hit, -jnp.inf, cur)
    es = [jnp.exp(v - vals[0]) for v in vals]
    den = es[0] + es[1] + es[2] + es[3]
    for k in range(TOP_K):
        gate_ref[:, k:k + 1] = es[k] / den

    @pl.when(first)
    def _():
        cnt_ref[...] = jnp.zeros_like(cnt_ref)

    cnt_ref[0:1, :] += jnp.sum(onehot, axis=0, keepdims=True)


def _outproj(y_pool, y_ssm, x, g1, sc2, sh2, w_out_bf, ln_g, ln_b, wr_bf, br_pad, n_experts):
    b, l, d = x.shape
    dp = y_pool.shape[-1]
    tm = min(l, 256)
    nl = l // tm
    t = b * l
    row = pl.BlockSpec((None, tm, d), lambda bi, i: (bi, i, 0))
    halfrow = pl.BlockSpec((None, tm, dp), lambda bi, i: (bi, i, 0))
    mod = pl.BlockSpec((None, 1, d), lambda bi, i: (bi, 0, 0))
    vec = pl.BlockSpec((1, d), lambda bi, i: (0, 0))
    tok4 = pl.BlockSpec((tm, TOP_K), lambda bi, i: (bi * nl + i, 0))
    kern = functools.partial(_outproj_kernel, n_experts=n_experts)
    return pl.pallas_call(
        kern,
        grid=(b, nl),
        in_specs=[halfrow, halfrow, row, mod, mod, mod,
                  pl.BlockSpec((d, d), lambda bi, i: (0, 0)), vec, vec,
                  pl.BlockSpec((d, LANES), lambda bi, i: (0, 0)),
                  pl.BlockSpec((1, LANES), lambda bi, i: (0, 0))],
        out_specs=[row,
                   pl.BlockSpec((tm, d), lambda bi, i: (bi * nl + i, 0)),
                   tok4, tok4,
                   pl.BlockSpec((SUBLANES, LANES), lambda bi, i: (0, 0))],
        out_shape=[jax.ShapeDtypeStruct((b, l, d), F32),
                   jax.ShapeDtypeStruct((t, d), F32),
                   jax.ShapeDtypeStruct((t, TOP_K), I32),
                   jax.ShapeDtypeStruct((t, TOP_K), F32),
                   jax.ShapeDtypeStruct((SUBLANES, LANES), F32)],
        compiler_params=_params(("arbitrary", "arbitrary"), 2 * (d * d * 2 + d * LANES * 2 + tm * d * 14)),
        name="out_proj_router",
    )(y_pool, y_ssm, x, g1, sc2, sh2, w_out_bf, ln_g, ln_b, wr_bf, br_pad)


def _plan_kernel(idx_ref, start_ref, pos_ref, carry):
    i = pl.program_id(0)

    @pl.when(i == 0)
    def _():
        carry[...] = jnp.zeros_like(carry)

    idx = idx_ref[...]
    tp = idx.shape[0]
    lane = lax.broadcasted_iota(I32, (tp, LANES), 1)
    hits = [lane == idx[:, k:k + 1] for k in range(TOP_K)]
    onehot = jnp.zeros((tp, LANES), F32)
    for h in hits:
        onehot = onehot + jnp.where(h, 1.0, 0.0)
    below = jnp.where(lax.broadcasted_iota(I32, (tp, tp), 0) > lax.broadcasted_iota(I32, (tp, tp), 1),
                      1.0, 0.0).astype(BF16)
    excl = jnp.dot(below, onehot.astype(BF16), preferred_element_type=F32)
    slot = start_ref[0:1, :] + carry[0:1, :] + excl
    for k in range(TOP_K):
        pos_ref[:, k:k + 1] = jnp.sum(jnp.where(hits[k], slot, 0.0), axis=1, keepdims=True).astype(I32)
    carry[0:1, :] += jnp.sum(onehot, axis=0, keepdims=True)


def _plan(idx_all, start_pad):
    t = idx_all.shape[0]
    tp = next(c for c in (512, 384, 256, 128, 64, 32, 16, 8) if t % c == 0)
    return pl.pallas_call(
        _plan_kernel,
        grid=(t // tp,),
        in_specs=[pl.BlockSpec((tp, TOP_K), lambda i: (i, 0)),
                  pl.BlockSpec((SUBLANES, LANES), lambda i: (0, 0))],
        out_specs=pl.BlockSpec((tp, TOP_K), lambda i: (i, 0)),
        out_shape=jax.ShapeDtypeStruct((t, TOP_K), I32),
        scratch_shapes=[pltpu.VMEM((SUBLANES, LANES), F32)],
        compiler_params=_params(("arbitrary",), 4 * tp * LANES * 4),
        name="moe_plan",
    )(idx_all, start_pad)


DISPATCH_UNROLL = 4


def _dispatch_kernel(fill_ref, posp_ref, poss_ref, hp_ref, hs_ref, xs_ref, zbuf, sem, zsem, *, n_p_steps, n_blk):
    i = pl.program_id(0)

    def fill_copy(b):
        return pltpu.make_async_copy(zbuf, xs_ref.at[pl.ds(pl.multiple_of(b * ROW_BLK, ROW_BLK), ROW_BLK)], zsem)

    @pl.when(i == 0)
    def _():
        zbuf[...] = jnp.zeros_like(zbuf)

        def start(b, carry):
            @pl.when(fill_ref[b] != 0)
            def _():
                fill_copy(b).start()
            return carry

        def wait(b, carry):
            @pl.when(fill_ref[b] != 0)
            def _():
                fill_copy(b).wait()
            return carry

        lax.fori_loop(0, n_blk, start, 0)
        lax.fori_loop(0, n_blk, wait, 0)

    def scatter(src_ref, pos_ref):
        td = src_ref.shape[0]

        def body(t2, carry):
            for tt in range(DISPATCH_UNROLL):
                t = t2 * DISPATCH_UNROLL + tt
                for k in range(TOP_K):
                    p = pos_ref[0, 0, t * TOP_K + k]
                    pltpu.make_async_copy(src_ref.at[pl.ds(t, 1)], xs_ref.at[pl.ds(p, 1)], sem).start()
            return carry

        lax.fori_loop(0, td // DISPATCH_UNROLL, body, 0)
        for k in range(TOP_K):
            pltpu.make_async_copy(src_ref, xs_ref.at[pl.ds(0, td)], sem).wait()

    @pl.when(i < n_p_steps)
    def _():
        scatter(hp_ref, posp_ref)

    @pl.when(i >= n_p_steps)
    def _():
        scatter(hs_ref, poss_ref)


def _dispatch(pos_p, pos_s, h_prompt, h_sample, fill_flags, n_rows):
    tp, w = h_prompt.shape
    ts = h_sample.shape[0]
    tdp, tds = math.gcd(tp, 512), math.gcd(ts, 512)
    n_p_steps, n_s_steps = tp // tdp, ts // tds
    n_blk = fill_flags.shape[0]
    grid_spec = pltpu.PrefetchScalarGridSpec(
        num_scalar_prefetch=1,
        grid=(n_p_steps + n_s_steps,),
        in_specs=[pl.BlockSpec((1, 1, tdp * TOP_K), lambda i, f: (jnp.minimum(i, n_p_steps - 1), 0, 0),
                               memory_space=pltpu.SMEM),
                  pl.BlockSpec((1, 1, tds * TOP_K), lambda i, f: (jnp.maximum(i - n_p_steps, 0), 0, 0),
                               memory_space=pltpu.SMEM),
                  pl.BlockSpec((tdp, w), lambda i, f: (jnp.minimum(i, n_p_steps - 1), 0)),
                  pl.BlockSpec((tds, w), lambda i, f: (jnp.maximum(i - n_p_steps, 0), 0))],
        out_specs=pl.BlockSpec(memory_space=pl.ANY),
        scratch_shapes=[pltpu.VMEM((ROW_BLK, w), F32),
                        pltpu.SemaphoreType.DMA(()), pltpu.SemaphoreType.DMA(())],
    )
    need = 2 * (tdp + tds) * w * 4 + ROW_BLK * w * 4
    return pl.pallas_call(
        functools.partial(_dispatch_kernel, n_p_steps=n_p_steps, n_blk=n_blk),
        grid_spec=grid_spec,
        out_shape=jax.ShapeDtypeStruct((n_rows, w), F32),
        compiler_params=_params(("arbitrary",), need),
        name="moe_dispatch",
    )(fill_flags, pos_p.reshape(n_p_steps, 1, tdp * TOP_K), pos_s.reshape(n_s_steps, 1, tds * TOP_K),
      h_prompt, h_sample)


def _moe_kernel(ie_ref, ir_ref, inb_ref, meta_ref, xs_ref, wg_ref, wu_ref, bg_ref, bu_ref, wd_ref, bd_ref,
                ys_ref, xbuf, act, ystage, zbuf, yflag, xsem, ysem, zsem, *, nc, n_blk):
    i = pl.program_id(0)
    c = pl.program_id(1)
    n_items = meta_ref[0]
    used_blk = meta_ref[1]
    valid = i < n_items
    nb = inb_ref[i]
    row0 = pl.multiple_of(ir_ref[i], ROW_BLK)
    n_quad = lax.shift_right_logical(nb, 3)
    has_pair = (nb & 4) != 0
    has_rb = (nb & 2) != 0
    has_blk = (nb & 1) != 0
    base_pair = pl.multiple_of(n_quad * 4 * MOE_RB, ROW_BLK)
    base_rb = pl.multiple_of(base_pair + jnp.where(has_pair, 2 * MOE_RB, 0), ROW_BLK)
    base_blk = pl.multiple_of(base_rb + jnp.where(has_rb, MOE_RB, 0), ROW_BLK)
    first = jnp.logical_and(i == 0, c == 0)
    last = jnp.logical_and(i == pl.num_programs(0) - 1, c == pl.num_programs(1) - 1)

    def x_copy(it, j):
        src0 = pl.multiple_of(ir_ref[it] + j * ROW_BLK, ROW_BLK)
        return pltpu.make_async_copy(xs_ref.at[pl.ds(src0, ROW_BLK)],
                                     xbuf.at[pl.ds(pl.multiple_of(j * ROW_BLK, ROW_BLK), ROW_BLK)], xsem)

    def x_start(it):
        def start(j, carry):
            x_copy(it, j).start()
            return carry
        lax.fori_loop(0, inb_ref[it], start, 0)

    def x_wait(it):
        def wait(j, carry):
            x_copy(it, j).wait()
            return carry
        lax.fori_loop(0, inb_ref[it], wait, 0)

    def tail_copy(b, col):
        return pltpu.make_async_copy(
            zbuf, ys_ref.at[pl.ds(pl.multiple_of(b * ROW_BLK, ROW_BLK), ROW_BLK), pl.ds(col * MOE_CW, MOE_CW)], zsem)

    n_col = ys_ref.shape[1] // MOE_CW

    @pl.when(first)
    def _():
        x_start(0)
        zbuf[...] = jnp.zeros_like(zbuf)
        for slot in range(YS_SLOTS):
            yflag[slot] = 0

        def start(b, carry):
            for col in range(n_col):
                tail_copy(b, col).start()
            return carry
        lax.fori_loop(used_blk, n_blk, start, 0)

    @pl.when(jnp.logical_and(valid, c == 0))
    def _():
        x_wait(i)

    @pl.when(jnp.logical_and(c == nc, i + 1 < n_items))
    def _():
        x_start(i + 1)

    @pl.when(jnp.logical_and(valid, c < nc))
    def _():
        def piece(r0, rows):
            w, bias = [], []
            for s in range(MOE_CW // LANES):
                cols = slice(s * LANES, (s + 1) * LANES)
                w += [wg_ref[:, cols].astype(BF16), wu_ref[:, cols].astype(BF16)]
                bias += [bg_ref[:, cols], bu_ref[:, cols]]
            x = xbuf[pl.ds(r0, rows), :].astype(BF16)
            gu = (jnp.dot(x, jnp.concatenate(w, axis=1), preferred_element_type=F32)
                  + jnp.concatenate(bias, axis=1))
            for s in range(MOE_CW // LANES):
                gate = jnp.minimum(gu[:, 2 * s * LANES:(2 * s + 1) * LANES], SWIGLU_LIMIT)
                up = jnp.clip(gu[:, (2 * s + 1) * LANES:(2 * s + 2) * LANES], -SWIGLU_LIMIT, SWIGLU_LIMIT)
                a = (up + 1.0) * (gate * jax.nn.sigmoid(SWIGLU_ALPHA * gate))
                act[c, pl.ds(r0, rows), s * LANES:(s + 1) * LANES] = a.astype(BF16)

        def quad(q, carry):
            r0 = pl.multiple_of(q * 4 * MOE_RB, ROW_BLK)
            for j in range(4):
                piece(pl.multiple_of(r0 + j * MOE_RB, ROW_BLK), MOE_RB)
            return carry

        lax.fori_loop(0, n_quad, quad, 0)

        @pl.when(has_pair)
        def _():
            for j in range(2):
                piece(pl.multiple_of(base_pair + j * MOE_RB, ROW_BLK), MOE_RB)

        @pl.when(has_rb)
        def _():
            piece(base_rb, MOE_RB)

        @pl.when(has_blk)
        def _():
            piece(base_blk, ROW_BLK)

    slot_rows = [MOE_RB] * (YS_SLOTS - 1) + [ROW_BLK]

    def y_copy(slot, r0, col0):
        dst_row = pl.multiple_of(row0 + r0, ROW_BLK)
        return pltpu.make_async_copy(ystage.at[slot, pl.ds(0, slot_rows[slot])],
                                     ys_ref.at[pl.ds(dst_row, slot_rows[slot]), pl.ds(col0, MOE_CW)],
                                     ysem.at[slot])

    def y_drain_all():
        for slot in range(YS_SLOTS):
            @pl.when(yflag[slot] != 0)
            def _(slot=slot):
                y_copy(slot, 0, 0).wait()
                yflag[slot] = 0

    @pl.when(jnp.logical_and(valid, c >= nc))
    def _():
        col0 = pl.multiple_of((c - nc) * MOE_CW, MOE_CW)
        y_drain_all()

        def compute(r0, rows):
            a = jnp.concatenate([act[cc, pl.ds(r0, rows), :] for cc in range(nc)], axis=1)
            return jnp.dot(a, wd_ref[...].astype(BF16), preferred_element_type=F32) + bd_ref[...]

        def group(slots, base):
            for j, slot in enumerate(slots):
                ystage[slot, 0:slot_rows[slot], :] = compute(pl.multiple_of(base + j * MOE_RB, ROW_BLK),
                                                             slot_rows[slot])
            for j, slot in enumerate(slots):
                y_copy(slot, base + j * MOE_RB, col0).start()

        def quad(q, carry):
            @pl.when(q > 0)
            def _():
                for slot in (0, 1, 2, 3):
                    y_copy(slot, 0, 0).wait()

            group((0, 1, 2, 3), pl.multiple_of(q * 4 * MOE_RB, ROW_BLK))
            return carry

        lax.fori_loop(0, n_quad, quad, 0)

        @pl.when(n_quad > 0)
        def _():
            for slot in (0, 1, 2, 3):
                yflag[slot] = 1

        @pl.when(has_pair)
        def _():
            group((4, 5), base_pair)
            yflag[4] = 1
            yflag[5] = 1

        @pl.when(has_rb)
        def _():
            group((6,), base_rb)
            yflag[6] = 1

        @pl.when(has_blk)
        def _():
            group((7,), base_blk)
            yflag[7] = 1

    @pl.when(last)
    def _():
        def wait(b, carry):
            for col in range(n_col):
                tail_copy(b, col).wait()
            return carry
        lax.fori_loop(used_blk, n_blk, wait, 0)
        y_drain_all()


def _moe(xs, item_e, item_row0, item_nblk, meta, w_gate_up, b_gate_up, w_down, b_down):
    n_rows, d = xs.shape
    n_exp, _, ff2 = w_gate_up.shape
    ff = ff2 // 2
    assert w_gate_up.shape[1] == d and w_down.shape[1:] == (ff, d) and ff == d
    nc = ff // MOE_CW
    ni = item_e.shape[0]
    rbuf = MOE_MAX_BLK * ROW_BLK

    def item(i, n):
        return jnp.minimum(i, n[0] - 1)

    def c1(i, c, n):
        return jnp.where(i < n[0], jnp.minimum(c, nc - 1), nc - 1)

    def c2(i, c, n):
        return jnp.where(i < n[0], jnp.maximum(c - nc, 0), nc - 1)

    bgu3 = b_gate_up.reshape(n_exp, 1, ff2)
    bd3 = b_down.reshape(n_exp, 1, d)
    in_specs = [
        pl.BlockSpec(memory_space=pl.ANY),
        pl.BlockSpec((None, d, MOE_CW), lambda i, c, e, r, b, n: (e[item(i, n)], 0, c1(i, c, n))),
        pl.BlockSpec((None, d, MOE_CW), lambda i, c, e, r, b, n: (e[item(i, n)], 0, nc + c1(i, c, n))),
        pl.BlockSpec((None, 1, MOE_CW), lambda i, c, e, r, b, n: (e[item(i, n)], 0, c1(i, c, n))),
        pl.BlockSpec((None, 1, MOE_CW), lambda i, c, e, r, b, n: (e[item(i, n)], 0, nc + c1(i, c, n))),
        pl.BlockSpec((None, ff, MOE_CW), lambda i, c, e, r, b, n: (e[item(i, n)], 0, c2(i, c, n))),
        pl.BlockSpec((None, 1, MOE_CW), lambda i, c, e, r, b, n: (e[item(i, n)], 0, c2(i, c, n))),
    ]
    grid_spec = pltpu.PrefetchScalarGridSpec(
        num_scalar_prefetch=4,
        grid=(ni, 2 * nc),
        in_specs=in_specs,
        out_specs=pl.BlockSpec(memory_space=pl.ANY),
        scratch_shapes=[pltpu.VMEM((rbuf, d), F32),
                        pltpu.VMEM((nc, rbuf, MOE_CW), BF16),
                        pltpu.VMEM((YS_SLOTS, MOE_RB, MOE_CW), F32),
                        pltpu.VMEM((ROW_BLK, MOE_CW), F32),
                        pltpu.SMEM((YS_SLOTS,), I32),
                        pltpu.SemaphoreType.DMA(()),
                        pltpu.SemaphoreType.DMA((YS_SLOTS,)),
                        pltpu.SemaphoreType.DMA(())],
    )
    need = (rbuf * d * 4 + rbuf * ff * 2 + YS_SLOTS * MOE_RB * MOE_CW * 4 + ROW_BLK * MOE_CW * 4
            + 2 * 3 * d * MOE_CW * 4)
    return pl.pallas_call(
        functools.partial(_moe_kernel, nc=nc, n_blk=n_rows // ROW_BLK),
        grid_spec=grid_spec,
        out_shape=jax.ShapeDtypeStruct((n_rows, d), F32),
        compiler_params=_params(("arbitrary", "arbitrary"), need),
        name="moe_experts",
    )(item_e, item_row0, item_nblk, meta, xs, w_gate_up, w_gate_up, bgu3, bgu3, w_down, bd3)


def _combine_kernel(pos_ref, posn_ref, ys_ref, gate_ref, x1_ref, g2_ref, lng_ref, lnb_ref, o_ref, gbuf, sem, *, tc):
    i = pl.program_id(0)
    slot = i % 2

    def issue(p_ref, s):
        def body(t2, carry):
            for tt in range(DISPATCH_UNROLL):
                t = t2 * DISPATCH_UNROLL + tt
                for k in range(TOP_K):
                    p = p_ref[0, 0, t * TOP_K + k]
                    pltpu.make_async_copy(ys_ref.at[pl.ds(p, 1)], gbuf.at[s, k, pl.ds(t, 1)], sem.at[s]).start()
            return carry
        lax.fori_loop(0, tc // DISPATCH_UNROLL, body, 0)

    @pl.when(i == 0)
    def _():
        issue(pos_ref, 0)

    @pl.when(i + 1 < pl.num_programs(0))
    def _():
        issue(posn_ref, 1 - slot)

    for k in range(TOP_K):
        pltpu.make_async_copy(ys_ref.at[pl.ds(0, tc)], gbuf.at[slot, k], sem.at[slot]).wait()
    gt = gate_ref[...]
    ffn = gt[:, 0:1] * gbuf[slot, 0]
    for k in range(1, TOP_K):
        ffn = ffn + gt[:, k:k + 1] * gbuf[slot, k]
    o_ref[...] = _layer_norm(DN_ALPHA * x1_ref[...] + g2_ref[...] * ffn, lng_ref[...], lnb_ref[...])


def _combine(pos, ys, gates, x1, g2, ln_g, ln_b):
    b, l, d = x1.shape
    t = b * l
    tc = min(l, 128)
    per_b = l // tc
    n_steps = t // tc
    pos3 = pos.reshape(n_steps, 1, tc * TOP_K)
    vec = pl.BlockSpec((1, d), lambda i: (0, 0))
    out = pl.pallas_call(
        functools.partial(_combine_kernel, tc=tc),
        grid=(n_steps,),
        in_specs=[pl.BlockSpec((1, 1, tc * TOP_K), lambda i: (i, 0, 0), memory_space=pltpu.SMEM),
                  pl.BlockSpec((1, 1, tc * TOP_K), lambda i: (jnp.minimum(i + 1, n_steps - 1), 0, 0),
                               memory_space=pltpu.SMEM),
                  pl.BlockSpec(memory_space=pl.ANY),
                  pl.BlockSpec((tc, TOP_K), lambda i: (i, 0)),
                  pl.BlockSpec((tc, d), lambda i: (i, 0)),
                  pl.BlockSpec((None, 1, d), lambda i: (i // per_b, 0, 0)),
                  vec, vec],
        out_specs=pl.BlockSpec((tc, d), lambda i: (i, 0)),
        out_shape=jax.ShapeDtypeStruct((t, d), F32),
        scratch_shapes=[pltpu.VMEM((2, TOP_K, tc, d), F32), pltpu.SemaphoreType.DMA((2,))],
        compiler_params=_params(("arbitrary",), (2 * TOP_K + 2 * 2) * tc * d * 4),
        name="moe_combine",
    )(pos3, pos3, ys, gates, x1.reshape(t, d), g2, ln_g, ln_b)
    return out.reshape(b, l, d)


def _moe_items(counts, n_blk_total):
    n_exp = counts.shape[0]
    nblk = (counts + ROW_BLK - 1) // ROW_BLK
    blk_end = jnp.cumsum(nblk)
    blk0 = blk_end - nblk
    n_it = (nblk + MOE_MAX_BLK - 1) // MOE_MAX_BLK
    it_end = jnp.cumsum(n_it)
    it0 = it_end - n_it
    ni = n_exp + n_blk_total // MOE_MAX_BLK
    j = jnp.arange(ni, dtype=I32)
    e = jnp.minimum(jnp.sum(j[:, None] >= it_end[None, :], axis=1), n_exp - 1).astype(I32)
    local = j - it0[e]
    item_nblk = jnp.clip(nblk[e] - local * MOE_MAX_BLK, 0, MOE_MAX_BLK).astype(I32)
    item_row0 = ((blk0[e] + local * MOE_MAX_BLK) * ROW_BLK).astype(I32)
    n_items = it_end[-1].astype(I32)
    item_nblk = jnp.where(j < n_items, item_nblk, 0)
    item_row0 = jnp.where(j < n_items, item_row0, 0)
    row_start = (blk0 * ROW_BLK).astype(F32)
    used_blk = blk_end[-1].astype(I32)
    meta = jnp.stack([n_items, used_blk])
    blk = jnp.arange(n_blk_total, dtype=I32)
    partial = jnp.any((blk[:, None] == (blk_end - 1)[None, :]) & ((counts % ROW_BLK) != 0)[None, :], axis=1)
    fill = jnp.logical_or(partial, blk >= used_blk).astype(I32)
    return e, item_row0, item_nblk, meta, row_start, fill


def _mixer_stream(x, mod, hist, h0, pos0, w):
    b, l, d = x.shape
    sh1, sc1, g1, sh2, sc2, g2 = [m.reshape(b, 1, d) for m in jnp.split(mod, 6, axis=-1)]
    u_pool, u_ssm = _inproj(x, sc1, sh1, w["w_in"])
    hist16 = jnp.concatenate([jnp.zeros((b, HIST_ROWS - hist.shape[1], hist.shape[2]), F32), hist.astype(F32)], axis=1)
    y_pool, new_hist = _pool(u_pool, hist16, w["w_pool"], w["pool_scale"], pos0)
    y_ssm, h_new = _ssm(u_ssm, w["ssm"], h0)
    x1, hp, idx, gates, cnt = _outproj(y_pool, y_ssm, x, g1, sc2, sh2, w["w_out"], w["ln1_g"], w["ln1_b"],
                                       w["w_router"], w["b_router"], w["n_experts"])
    s_re, s_im = _slabs_to_state(h_new, SSM_STATE)
    return dict(x1=x1, hp=hp, idx=idx, gates=gates, cnt=cnt, g2=g2,
                hist=new_hist[:, 1:, :], s_re=s_re, s_im=s_im)


def kernel(x_prompt, x_sample, cache_pool, state_ssm_re, state_ssm_im, c_prompt, c_sample, w_ada, b_ada, w_in, w_pool, pool_scale, lambda_re, lambda_im, log_dt, ssm_b_re, ssm_b_im, ssm_c_re, ssm_c_im, d_skip, w_glu, b_glu, w_out, ln1_g, ln1_b, w_router, b_router, w_gate_up, b_gate_up, w_down, b_down, ln2_g, ln2_b):
    assert w_ada.shape[0] == DEPTH
    bp, lp, d = x_prompt.shape
    bs, ls, _ = x_sample.shape
    assert bp == SUBLANES and bs == SUBLANES, "the S5 kernel puts the 8 streams on sublanes"
    n_exp = w_router.shape[-1]
    dp = w_pool.shape[1] * w_pool.shape[2]

    w = dict(
        w_in=w_in[0].astype(BF16),
        w_pool=w_pool[0].astype(BF16),
        pool_scale=pool_scale[0].astype(F32).reshape(1, dp),
        ssm=_ssm_params(lambda_re[0], lambda_im[0], log_dt[0], ssm_b_re[0], ssm_b_im[0], ssm_c_re[0], ssm_c_im[0],
                        d_skip[0], w_glu[0], b_glu[0], SUBLANES),
        w_out=w_out[0].astype(BF16),
        ln1_g=ln1_g[0].reshape(1, d), ln1_b=ln1_b[0].reshape(1, d),
        w_router=jnp.pad(w_router[0], ((0, 0), (0, LANES - n_exp))).astype(BF16),
        b_router=jnp.pad(b_router[0].astype(F32), (0, LANES - n_exp)).reshape(1, LANES),
        n_experts=n_exp,
    )

    mod = _ada(jnp.concatenate([c_prompt, c_sample], axis=0), w_ada[0], b_ada[0].reshape(1, -1))
    n_slab = lambda_re.shape[1]
    p = _mixer_stream(x_prompt, mod[:bp], jnp.zeros((bp, HIST_ROWS - 1, dp), F32),
                      jnp.zeros((n_slab, bp, LANES), F32), 0, w)
    s = _mixer_stream(x_sample, mod[bp:], cache_pool[0], _state_to_slabs(state_ssm_re[0], state_ssm_im[0]),
                      PAST_LEN, w)

    tp, ts = bp * lp, bs * ls
    n_assign = (tp + ts) * TOP_K
    n_blk_total = -(-n_assign // ROW_BLK) + n_exp
    n_rows = n_blk_total * ROW_BLK
    counts = (p["cnt"][0, :n_exp] + s["cnt"][0, :n_exp]).astype(I32)
    item_e, item_row0, item_nblk, meta, row_start, fill = _moe_items(counts, n_blk_total)
    start_pad = jnp.zeros((SUBLANES, LANES), F32).at[0, :n_exp].set(row_start)
    pos = _plan(jnp.concatenate([p["idx"], s["idx"]], axis=0), start_pad)
    pos_p, pos_s = pos[:tp], pos[tp:]
    xs = _dispatch(pos_p, pos_s, p["hp"], s["hp"], fill, n_rows)
    ys = _moe(xs, item_e, item_row0, item_nblk, meta, w_gate_up[0], b_gate_up[0], w_down[0], b_down[0])
    g2l, b2l = ln2_g[0].reshape(1, d), ln2_b[0].reshape(1, d)
    y_p = _combine(pos_p, ys, p["gates"], p["x1"], p["g2"], g2l, b2l)
    y_s = _combine(pos_s, ys, s["gates"], s["x1"], s["g2"], g2l, b2l)

    return (y_p, y_s, p["hist"][None], p["s_re"][None], p["s_im"][None],
            s["hist"][None], s["s_re"][None], s["s_im"][None])
```

```python
import functools
import math

import jax
import jax.numpy as jnp
from jax import lax
from jax.experimental import pallas as pl
from jax.experimental.pallas import tpu as pltpu

F32 = jnp.float32
BF16 = jnp.bfloat16
I32 = jnp.int32
U32 = jnp.uint32

POOL_WINDOWS = (2, 4, 8, 16)
HIST_ROWS = 16
SSM_CH = 16
SSM_STATE = 64
TOP_K = 4
SWIGLU_LIMIT = 7.0
SWIGLU_ALPHA = 1.702
LN_EPS = 1e-5
DEPTH = 1
DN_ALPHA = (2 * DEPTH) ** 0.25
PAST_LEN = 2048

LANES = 128
SUBLANES = 8
MXU_DIM = 256
V7X_VMEM_BYTES = 64 * 1024 * 1024
MIB = 1024 * 1024

ROW_BLK = 256
MOE_RB = 512
MOE_CW = 256
MOE_MAX_BLK = 10
YS_SLOTS = 8


def _params(sem, need_bytes):
    limit = min(need_bytes + 16 * MIB, V7X_VMEM_BYTES - 4 * MIB)
    return pltpu.CompilerParams(dimension_semantics=sem, vmem_limit_bytes=limit)


def _layer_norm(v, g, b):
    mu = jnp.mean(v, axis=-1, keepdims=True)
    vc = v - mu
    var = jnp.mean(vc * vc, axis=-1, keepdims=True)
    return vc * lax.rsqrt(var + LN_EPS) * g + b


def _ada_kernel(c_ref, w_ref, b_ref, o_ref):
    c = c_ref[...]
    s = c * jax.nn.sigmoid(c)
    o_ref[...] = jnp.dot(s.astype(BF16), w_ref[...].astype(BF16), preferred_element_type=F32) + b_ref[...]


def _ada(c_all, w_ada, b_ada):
    r, d = c_all.shape
    n = w_ada.shape[1]
    tn = 1536
    return pl.pallas_call(
        _ada_kernel,
        grid=(n // tn,),
        in_specs=[pl.BlockSpec((r, d), lambda j: (0, 0)),
                  pl.BlockSpec((d, tn), lambda j: (0, j)),
                  pl.BlockSpec((1, tn), lambda j: (0, j))],
        out_specs=pl.BlockSpec((r, tn), lambda j: (0, j)),
        out_shape=jax.ShapeDtypeStruct((r, n), F32),
        compiler_params=_params(("arbitrary",), 2 * (d * tn * 4 + r * (d + tn) * 4)),
        name="ada_mod",
    )(c_all, w_ada, b_ada)


def _inproj_kernel(x_ref, sc_ref, sh_ref, w_ref, up_ref, us_ref):
    h = x_ref[...] * (1.0 + sc_ref[...]) + sh_ref[...]
    u = jnp.dot(h.astype(BF16), w_ref[...], preferred_element_type=F32)
    dp = up_ref.shape[-1]
    up_ref[...] = u[:, :dp]
    us_ref[...] = u[:, dp:]


def _inproj(x, sc, sh, w_bf):
    b, l, d = x.shape
    dm = w_bf.shape[1]
    dp = dm // 2
    tm = min(l, 512)
    row = pl.BlockSpec((None, tm, d), lambda bi, i: (bi, i, 0))
    mod = pl.BlockSpec((None, 1, d), lambda bi, i: (bi, 0, 0))
    half = pl.BlockSpec((None, tm, dp), lambda bi, i: (bi, i, 0))
    return pl.pallas_call(
        _inproj_kernel,
        grid=(b, l // tm),
        in_specs=[row, mod, mod, pl.BlockSpec((d, dm), lambda bi, i: (0, 0))],
        out_specs=[half, half],
        out_shape=[jax.ShapeDtypeStruct((b, l, dp), F32), jax.ShapeDtypeStruct((b, l, dm - dp), F32)],
        compiler_params=_params(("arbitrary", "arbitrary"), 2 * (d * dm * 2 + tm * (d + dm) * 4)),
        name="in_proj",
    )(x, sc, sh, w_bf)


def _pool_kernel(u_ref, hist_ref, w_ref, scale_ref, y_ref, nh_ref, hbuf, *, tt, pos0):
    i = pl.program_id(1)

    @pl.when(i == 0)
    def _():
        hbuf[...] = hist_ref[...]

    u = u_ref[...]
    hb = hbuf[...]
    u16 = u.astype(BF16)
    h16 = hb.astype(BF16)
    pc = u.shape[1] // len(POOL_WINDOWS)
    d_main = lax.broadcasted_iota(I32, (tt, tt), 0) - lax.broadcasted_iota(I32, (tt, tt), 1)
    d_hist = (lax.broadcasted_iota(I32, (tt, HIST_ROWS), 0) + HIST_ROWS
              - lax.broadcasted_iota(I32, (tt, HIST_ROWS), 1))
    pos = (pos0 + i * tt + lax.broadcasted_iota(I32, (tt, 1), 0)).astype(F32)
    for g, w in enumerate(POOL_WINDOWS):
        sl = slice(g * pc, (g + 1) * pc)
        bm = jnp.where(d_main >= 0, jnp.where(d_main < w, 1.0, 0.0), 0.0).astype(BF16)
        bh = jnp.where(d_hist < w, 1.0, 0.0).astype(BF16)
        s = (jnp.dot(bm, u16[:, sl], preferred_element_type=F32)
             + jnp.dot(bh, h16[:, sl], preferred_element_type=F32))
        cnt = jnp.minimum(pos + 1.0, float(w))
        diff = s / cnt - u[:, sl]
        y = jnp.dot(diff.astype(BF16), w_ref[g], preferred_element_type=F32) * scale_ref[:, sl]
        y_ref[:, sl] = y.astype(y_ref.dtype)
    hbuf[...] = u[tt - HIST_ROWS:, :]

    @pl.when(i == pl.num_programs(1) - 1)
    def _():
        nh_ref[...] = hbuf[...]


def _pool(u_pool, hist16, w_pool_bf, pool_scale, pos0):
    b, l, dp = u_pool.shape
    tt = min(l, 256)
    g, pc, _ = w_pool_bf.shape
    kern = functools.partial(_pool_kernel, tt=tt, pos0=pos0)
    return pl.pallas_call(
        kern,
        grid=(b, l // tt),
        in_specs=[pl.BlockSpec((None, tt, dp), lambda bi, i: (bi, i, 0)),
                  pl.BlockSpec((None, HIST_ROWS, dp), lambda bi, i: (bi, 0, 0)),
                  pl.BlockSpec((g, pc, pc), lambda bi, i: (0, 0, 0)),
                  pl.BlockSpec((1, dp), lambda bi, i: (0, 0))],
        out_specs=[pl.BlockSpec((None, tt, dp), lambda bi, i: (bi, i, 0)),
                   pl.BlockSpec((None, HIST_ROWS, dp), lambda bi, i: (bi, 0, 0))],
        out_shape=[jax.ShapeDtypeStruct((b, l, dp), BF16), jax.ShapeDtypeStruct((b, HIST_ROWS, dp), F32)],
        scratch_shapes=[pltpu.VMEM((HIST_ROWS, dp), F32)],
        compiler_params=_params(("arbitrary", "arbitrary"), 2 * (tt * dp * 6 + g * pc * pc * 2 + 2 * HIST_ROWS * dp * 4)),
        name="pool_mixer",
    )(u_pool, hist16, w_pool_bf, pool_scale)


SSM_PAIR_GROUP = 8
SSM_TIME_UNROLL = 2


def _ssm_kernel(u_ref, bq_ref, cq_ref, wg_ref, ar_ref, ai_ref, dsk_ref, bgl_ref, h0_ref,
                y_ref, hn_ref, s_ref, ubuf, hst, bc_ref, cc_ref, *, tt, pitch, nb):
    i = pl.program_id(0)
    n_slab = s_ref.shape[0]
    n_chunk, ppc, rb, cb = bq_ref.shape
    slab_per_chunk = n_slab // n_chunk
    cw = bc_ref.shape[1]

    @pl.when(i == 0)
    def _():
        s_ref[...] = jnp.zeros_like(s_ref)
        ubuf[...] = jnp.zeros_like(ubuf)
        hst[...] = h0_ref[...]
        bc_ref[...] = jnp.zeros_like(bc_ref)
        lane = lax.broadcasted_iota(I32, (rb, cw), 1) - lax.broadcasted_iota(I32, (rb, cw), 0)
        for k in range(n_chunk):
            for q in range(ppc):
                bc_ref[k, q * rb:(q + 1) * rb, q * cb:(q + 1) * cb] = bq_ref[k, q]
                place = jnp.where(lane == q * rb, 1.0, 0.0).astype(BF16)
                cc_ref[k, q * cb:(q + 1) * cb, :] = jnp.dot(cq_ref[k, q], place,
                                                             preferred_element_type=F32).astype(BF16)

    u = u_ref[...].reshape(nb * tt, u_ref.shape[-1])
    u16 = u.astype(BF16)
    for k in range(n_chunk):
        bu = jnp.dot(u16[:, k * cw:(k + 1) * cw], bc_ref[k], preferred_element_type=F32)
        for j in range(slab_per_chunk):
            for b in range(nb):
                s_ref[k * slab_per_chunk + j, b * pitch:b * pitch + tt, :] = (
                    bu[b * tt:(b + 1) * tt, j * LANES:(j + 1) * LANES])
    for b in range(nb):
        ubuf[b * pitch:b * pitch + tt, :] = u[b * tt:(b + 1) * tt, :]

    n_pair = n_slab // 2
    for pg in range(n_pair // SSM_PAIR_GROUP):
        qs = [pg * SSM_PAIR_GROUP + j for j in range(SSM_PAIR_GROUP)]
        ars = [ar_ref[q] for q in qs]
        ais = [ai_ref[q] for q in qs]

        def body(t2, carry, qs=qs, ars=ars, ais=ais):
            hr, hi = list(carry[0]), list(carry[1])
            for dt in range(SSM_TIME_UNROLL):
                rows = pl.ds(t2 * SSM_TIME_UNROLL + dt, nb, stride=pitch)
                for j, q in enumerate(qs):
                    bur = s_ref[2 * q, rows, :]
                    bui = s_ref[2 * q + 1, rows, :]
                    r_ = ars[j] * hr[j] - ais[j] * hi[j] + bur
                    i_ = ars[j] * hi[j] + ais[j] * hr[j] + bui
                    s_ref[2 * q, rows, :] = r_
                    s_ref[2 * q + 1, rows, :] = i_
                    hr[j], hi[j] = r_, i_
            return tuple(hr), tuple(hi)

        init = (tuple(hst[2 * q] for q in qs), tuple(hst[2 * q + 1] for q in qs))
        hr, hi = lax.fori_loop(0, tt // SSM_TIME_UNROLL, body, init)
        for j, q in enumerate(qs):
            hst[2 * q] = hr[j]
            hst[2 * q + 1] = hi[j]

    ycs = []
    for n in range(n_chunk):
        hc = jnp.concatenate([s_ref[n * slab_per_chunk + j] for j in range(slab_per_chunk)], axis=1)
        ycs.append(jnp.dot(hc.astype(BF16), cc_ref[n], preferred_element_type=F32))
    y = jnp.concatenate(ycs, axis=1) + dsk_ref[...] * ubuf[...]
    y = jax.nn.gelu(y)
    y16 = y.astype(BF16)
    z = jnp.concatenate([jnp.dot(y16[:, k * cw:(k + 1) * cw], wg_ref[k], preferred_element_type=F32)
                         for k in range(n_chunk)], axis=1) + bgl_ref[...]
    out = y * jax.nn.sigmoid(z)
    for b in range(nb):
        y_ref[b] = out[b * pitch:b * pitch + tt].astype(y_ref.dtype)

    @pl.when(i == pl.num_programs(0) - 1)
    def _():
        hn_ref[...] = hst[...]


def _ssm(u_ssm, prm, h0):
    b, l, ds = u_ssm.shape
    tt = min(l, 64)
    pitch = tt + 4
    n_slab = h0.shape[0]
    n_chunk, ppc, rb, cb = prm["bq"].shape
    kern = functools.partial(_ssm_kernel, tt=tt, pitch=pitch, nb=b)

    def const(a):
        nd = a.ndim
        return pl.BlockSpec(a.shape, lambda i, nd=nd: (0,) * nd)

    args = (u_ssm, prm["bq"], prm["cq"], prm["wg"], prm["ar"], prm["ai"], prm["dsk"], prm["bgl"], h0)
    dense = n_chunk * (ppc * rb) * (ppc * cb) * 2
    return pl.pallas_call(
        kern,
        grid=(l // tt,),
        in_specs=[pl.BlockSpec((b, tt, ds), lambda i: (0, i, 0))] + [const(a) for a in args[1:]],
        out_specs=[pl.BlockSpec((b, tt, ds), lambda i: (0, i, 0)),
                   pl.BlockSpec((n_slab, b, LANES), lambda i: (0, 0, 0))],
        out_shape=[jax.ShapeDtypeStruct((b, l, ds), BF16), jax.ShapeDtypeStruct((n_slab, b, LANES), F32)],
        scratch_shapes=[pltpu.VMEM((n_slab, b * pitch, LANES), F32),
                        pltpu.VMEM((b * pitch, ds), F32),
                        pltpu.VMEM((n_slab, b, LANES), F32),
                        pltpu.VMEM((n_chunk, ppc * rb, ppc * cb), BF16),
                        pltpu.VMEM((n_chunk, ppc * cb, ppc * rb), BF16)],
        compiler_params=_params(("arbitrary",), (n_slab * LANES + ds) * b * pitch * 4 + 2 * b * tt * ds * 6
                                + 2 * dense + 2 * sum(a.size * a.dtype.itemsize for a in args[1:])
                                + prm["cq"].size * 2 * (LANES // rb - 1) * 2),
        name="ssm_mixer",
    )(*args)


def _ssm_params(lambda_re, lambda_im, log_dt, b_re, b_im, c_re, c_im, d_skip, w_glu, b_glu, nb):
    g, p = lambda_re.shape
    ch = b_re.shape[-1]
    gpc = MXU_DIM // ch
    n_chunk = g // gpc
    ppc = gpc // 2
    lr, li = lambda_re.astype(F32), lambda_im.astype(F32)
    dt = jnp.exp(log_dt.astype(F32))[:, None]
    mag = jnp.exp(lr * dt)
    abar_re, abar_im = mag * jnp.cos(li * dt), mag * jnp.sin(li * dt)
    nr, ni = abar_re - 1.0, abar_im
    den = lr * lr + li * li
    k_re = (nr * lr + ni * li) / den
    k_im = (ni * lr - nr * li) / den
    br, bi = b_re.astype(F32), b_im.astype(F32)
    bb_re = k_re[..., None] * br - k_im[..., None] * bi
    bb_im = k_re[..., None] * bi + k_im[..., None] * br
    eye_2 = jnp.eye(2, dtype=F32)
    bb = jnp.stack([bb_re, bb_im]).reshape(2, n_chunk, ppc, 2, p, ch)
    bq = jnp.einsum("rkqgpc,gy->kqgcryp", bb, eye_2).astype(BF16).reshape(n_chunk, ppc, 2 * ch, 2 * 2 * p)
    cm = jnp.stack([c_re.astype(F32), -c_im.astype(F32)]).reshape(2, n_chunk, ppc, 2, ch, p)
    cq = jnp.einsum("rkqgcp,gy->kqrgpyc", cm, eye_2).astype(BF16).reshape(n_chunk, ppc, 2 * 2 * p, 2 * ch)
    wg = jnp.einsum("kgce,gx->kgcxe", w_glu.astype(F32).reshape(n_chunk, gpc, ch, ch),
                    jnp.eye(gpc, dtype=F32)).reshape(n_chunk, gpc * ch, gpc * ch)
    n_pair = g // 2
    ar = jnp.broadcast_to(abar_re.reshape(n_pair, 1, 2 * p), (n_pair, nb, 2 * p))
    ai = jnp.broadcast_to(abar_im.reshape(n_pair, 1, 2 * p), (n_pair, nb, 2 * p))
    return dict(bq=bq, cq=cq, wg=wg.astype(BF16), ar=ar, ai=ai,
                dsk=d_skip.astype(F32).reshape(1, g * ch), bgl=b_glu.astype(F32).reshape(1, g * ch))


def _state_to_slabs(s_re, s_im):
    b, g, p = s_re.shape
    re = s_re.astype(F32).reshape(b, g // 2, 2 * p).transpose(1, 0, 2)
    im = s_im.astype(F32).reshape(b, g // 2, 2 * p).transpose(1, 0, 2)
    return jnp.stack([re, im], axis=1).reshape(g, b, 2 * p)


def _slabs_to_state(h, p):
    n_slab, b, _ = h.shape
    h = h.reshape(n_slab // 2, 2, b, 2 * p)
    re = h[:, 0].transpose(1, 0, 2).reshape(b, n_slab, p)
    im = h[:, 1].transpose(1, 0, 2).reshape(b, n_slab, p)
    return re, im


def _outproj_kernel(yp_ref, ys_ref, x_ref, g1_ref, sc2_ref, sh2_ref, w_ref, lng_ref, lnb_ref, wr_ref, br_ref,
                    x1_ref, hp_ref, idx_ref, gate_ref, cnt_ref, *, n_experts, n_split):
    first = jnp.logical_and(pl.program_id(0) == 0, pl.program_id(1) == 0)

    @pl.when(first)
    def _():
        cnt_ref[...] = jnp.zeros_like(cnt_ref)

    tm = x_ref.shape[0]
    th = tm // n_split
    for h in range(n_split):
        rows = slice(h * th, (h + 1) * th)
        ymix = jnp.concatenate([yp_ref[rows, :], ys_ref[rows, :]], axis=1)
        mix = jnp.dot(ymix, w_ref[...], preferred_element_type=F32)
        x1 = _layer_norm(DN_ALPHA * x_ref[rows, :] + g1_ref[...] * mix, lng_ref[...], lnb_ref[...])
        x1_ref[rows, :] = x1
        h2 = x1 * (1.0 + sc2_ref[...]) + sh2_ref[...]
        hp_ref[rows, :] = h2
        logits = jnp.dot(h2.astype(BF16), wr_ref[...], preferred_element_type=F32) + br_ref[...]
        lane = lax.broadcasted_iota(I32, (th, LANES), 1).astype(F32)
        cur = jnp.where(lane < n_experts, logits, -jnp.inf)
        vals, onehot = [], jnp.zeros((th, LANES), F32)
        for k in range(TOP_K):
            m = jnp.max(cur, axis=1, keepdims=True)
            sel = jnp.min(jnp.where(cur == m, lane, float(LANES)), axis=1, keepdims=True)
            hit = lane == sel
            idx_ref[rows, k:k + 1] = sel.astype(I32)
            vals.append(m)
            onehot = onehot + jnp.where(hit, 1.0, 0.0)
            cur = jnp.where(hit, -jnp.inf, cur)
        es = [jnp.exp(v - vals[0]) for v in vals]
        den = es[0] + es[1] + es[2] + es[3]
        for k in range(TOP_K):
            gate_ref[rows, k:k + 1] = es[k] / den
        cnt_ref[0:1, :] += jnp.sum(onehot, axis=0, keepdims=True)


def _outproj(y_pool, y_ssm, x, g1, sc2, sh2, w_out_bf, ln_g, ln_b, wr_bf, br_pad, n_experts):
    b, l, d = x.shape
    dp = y_pool.shape[-1]
    tm = min(l, 512)
    n_split = 2 if tm % 512 == 0 else 1
    nl = l // tm
    t = b * l
    row = pl.BlockSpec((None, tm, d), lambda bi, i: (bi, i, 0))
    halfrow = pl.BlockSpec((None, tm, dp), lambda bi, i: (bi, i, 0))
    mod = pl.BlockSpec((None, 1, d), lambda bi, i: (bi, 0, 0))
    vec = pl.BlockSpec((1, d), lambda bi, i: (0, 0))
    tok4 = pl.BlockSpec((tm, TOP_K), lambda bi, i: (bi * nl + i, 0))
    kern = functools.partial(_outproj_kernel, n_experts=n_experts, n_split=n_split)
    return pl.pallas_call(
        kern,
        grid=(b, nl),
        in_specs=[halfrow, halfrow, row, mod, mod, mod,
                  pl.BlockSpec((d, d), lambda bi, i: (0, 0)), vec, vec,
                  pl.BlockSpec((d, LANES), lambda bi, i: (0, 0)),
                  pl.BlockSpec((1, LANES), lambda bi, i: (0, 0))],
        out_specs=[row,
                   pl.BlockSpec((tm, d), lambda bi, i: (bi * nl + i, 0)),
                   tok4, tok4,
                   pl.BlockSpec((SUBLANES, LANES), lambda bi, i: (0, 0))],
        out_shape=[jax.ShapeDtypeStruct((b, l, d), F32),
                   jax.ShapeDtypeStruct((t, d), F32),
                   jax.ShapeDtypeStruct((t, TOP_K), I32),
                   jax.ShapeDtypeStruct((t, TOP_K), F32),
                   jax.ShapeDtypeStruct((SUBLANES, LANES), F32)],
        compiler_params=_params(("arbitrary", "arbitrary"), 2 * (d * d * 2 + d * LANES * 2 + tm * d * 14)),
        name="out_proj_router",
    )(y_pool, y_ssm, x, g1, sc2, sh2, w_out_bf, ln_g, ln_b, wr_bf, br_pad)


def _plan_kernel(idx_ref, start_ref, pos_ref, carry):
    i = pl.program_id(0)

    @pl.when(i == 0)
    def _():
        carry[...] = jnp.zeros_like(carry)

    idx = idx_ref[...]
    tp = idx.shape[0]
    lane = lax.broadcasted_iota(I32, (tp, LANES), 1)
    hits = [lane == idx[:, k:k + 1] for k in range(TOP_K)]
    onehot = jnp.zeros((tp, LANES), F32)
    for h in hits:
        onehot = onehot + jnp.where(h, 1.0, 0.0)
    below = jnp.where(lax.broadcasted_iota(I32, (tp, tp), 0) > lax.broadcasted_iota(I32, (tp, tp), 1),
                      1.0, 0.0).astype(BF16)
    excl = jnp.dot(below, onehot.astype(BF16), preferred_element_type=F32)
    slot = start_ref[0:1, :] + carry[0:1, :] + excl
    for k in range(TOP_K):
        pos_ref[:, k:k + 1] = jnp.sum(jnp.where(hits[k], slot, 0.0), axis=1, keepdims=True).astype(I32)
    carry[0:1, :] += jnp.sum(onehot, axis=0, keepdims=True)


def _plan(idx_all, start_pad):
    t = idx_all.shape[0]
    tp = next(c for c in (512, 384, 256, 128, 64, 32, 16, 8) if t % c == 0)
    return pl.pallas_call(
        _plan_kernel,
        grid=(t // tp,),
        in_specs=[pl.BlockSpec((tp, TOP_K), lambda i: (i, 0)),
                  pl.BlockSpec((SUBLANES, LANES), lambda i: (0, 0))],
        out_specs=pl.BlockSpec((tp, TOP_K), lambda i: (i, 0)),
        out_shape=jax.ShapeDtypeStruct((t, TOP_K), I32),
        scratch_shapes=[pltpu.VMEM((SUBLANES, LANES), F32)],
        compiler_params=_params(("arbitrary",), 4 * tp * LANES * 4),
        name="moe_plan",
    )(idx_all, start_pad)


def _dispatch_kernel(fill_ref, posp_ref, poss_ref, hp_ref, hs_ref, xs_ref, zbuf, sem, zsem, *, n_p_steps, n_blk):
    i = pl.program_id(0)

    def fill_copy(b):
        return pltpu.make_async_copy(zbuf, xs_ref.at[pl.ds(pl.multiple_of(b * ROW_BLK, ROW_BLK), ROW_BLK)], zsem)

    @pl.when(i == 0)
    def _():
        zbuf[...] = jnp.zeros_like(zbuf)

        def start(b, carry):
            @pl.when(fill_ref[b] != 0)
            def _():
                fill_copy(b).start()
            return carry

        def wait(b, carry):
            @pl.when(fill_ref[b] != 0)
            def _():
                fill_copy(b).wait()
            return carry

        lax.fori_loop(0, n_blk, start, 0)
        lax.fori_loop(0, n_blk, wait, 0)

    def scatter(src_ref, pos_ref):
        n_grp = src_ref.shape[0]

        def body(g, carry):
            for tt in range(SUBLANES):
                for k in range(TOP_K):
                    p = pos_ref[0, 0, (g * SUBLANES + tt) * TOP_K + k]
                    pltpu.make_async_copy(src_ref.at[g, pl.ds(tt, 1)], xs_ref.at[pl.ds(p, 1)], sem).start()
            return carry

        lax.fori_loop(0, n_grp, body, 0)
        rows = xs_ref.at[pl.ds(0, n_grp * SUBLANES)]
        for k in range(TOP_K):
            pltpu.make_async_copy(rows, rows, sem).wait()

    @pl.when(i < n_p_steps)
    def _():
        scatter(hp_ref, posp_ref)

    @pl.when(i >= n_p_steps)
    def _():
        scatter(hs_ref, poss_ref)


def _dispatch(pos_p, pos_s, h_prompt, h_sample, fill_flags, n_rows):
    tp, w = h_prompt.shape
    ts = h_sample.shape[0]
    tdp, tds = math.gcd(tp, 512), math.gcd(ts, 512)
    n_p_steps, n_s_steps = tp // tdp, ts // tds
    n_blk = fill_flags.shape[0]
    grid_spec = pltpu.PrefetchScalarGridSpec(
        num_scalar_prefetch=1,
        grid=(n_p_steps + n_s_steps,),
        in_specs=[pl.BlockSpec((1, 1, tdp * TOP_K), lambda i, f: (jnp.minimum(i, n_p_steps - 1), 0, 0),
                               memory_space=pltpu.SMEM),
                  pl.BlockSpec((1, 1, tds * TOP_K), lambda i, f: (jnp.maximum(i - n_p_steps, 0), 0, 0),
                               memory_space=pltpu.SMEM),
                  pl.BlockSpec((tdp // SUBLANES, SUBLANES, w), lambda i, f: (jnp.minimum(i, n_p_steps - 1), 0, 0)),
                  pl.BlockSpec((tds // SUBLANES, SUBLANES, w), lambda i, f: (jnp.maximum(i - n_p_steps, 0), 0, 0))],
        out_specs=pl.BlockSpec(memory_space=pl.ANY),
        scratch_shapes=[pltpu.VMEM((ROW_BLK, w), F32),
                        pltpu.SemaphoreType.DMA(()), pltpu.SemaphoreType.DMA(())],
    )
    need = 2 * (tdp + tds) * w * 4 + ROW_BLK * w * 4
    return pl.pallas_call(
        functools.partial(_dispatch_kernel, n_p_steps=n_p_steps, n_blk=n_blk),
        grid_spec=grid_spec,
        out_shape=jax.ShapeDtypeStruct((n_rows, w), F32),
        compiler_params=_params(("arbitrary",), need),
        name="moe_dispatch",
    )(fill_flags, pos_p.reshape(n_p_steps, 1, tdp * TOP_K), pos_s.reshape(n_s_steps, 1, tds * TOP_K),
      h_prompt.reshape(tp // SUBLANES, SUBLANES, w), h_sample.reshape(ts // SUBLANES, SUBLANES, w))


def _moe_kernel(ie_ref, ir_ref, inb_ref, meta_ref, xs_ref, wg_ref, wu_ref, bgu_ref, wd_ref, bd_ref,
                ys_ref, xbuf, act, ystage, zbuf, yflag, xsem, ysem, zsem, *, nc, n_blk):
    i = pl.program_id(0)
    c = pl.program_id(1)
    n_items = meta_ref[0]
    used_blk = meta_ref[1]
    valid = i < n_items
    nb = inb_ref[i]
    row0 = pl.multiple_of(ir_ref[i], ROW_BLK)
    n_quad = lax.shift_right_logical(nb, 3)
    has_pair = (nb & 4) != 0
    has_rb = (nb & 2) != 0
    has_blk = (nb & 1) != 0
    base_pair = pl.multiple_of(n_quad * 4 * MOE_RB, ROW_BLK)
    base_rb = pl.multiple_of(base_pair + jnp.where(has_pair, 2 * MOE_RB, 0), ROW_BLK)
    base_blk = pl.multiple_of(base_rb + jnp.where(has_rb, MOE_RB, 0), ROW_BLK)
    first = jnp.logical_and(i == 0, c == 0)
    last = jnp.logical_and(i == pl.num_programs(0) - 1, c == pl.num_programs(1) - 1)

    def x_copy(it, j):
        src0 = pl.multiple_of(ir_ref[it] + j * ROW_BLK, ROW_BLK)
        return pltpu.make_async_copy(xs_ref.at[pl.ds(src0, ROW_BLK)],
                                     xbuf.at[pl.ds(pl.multiple_of(j * ROW_BLK, ROW_BLK), ROW_BLK)], xsem)

    def x_start(it):
        def start(j, carry):
            x_copy(it, j).start()
            return carry
        lax.fori_loop(0, inb_ref[it], start, 0)

    def x_wait(it):
        def wait(j, carry):
            x_copy(it, j).wait()
            return carry
        lax.fori_loop(0, inb_ref[it], wait, 0)

    def tail_copy(b, col):
        return pltpu.make_async_copy(
            zbuf, ys_ref.at[pl.ds(pl.multiple_of(b * ROW_BLK, ROW_BLK), ROW_BLK), pl.ds(col * MOE_CW, MOE_CW)], zsem)

    n_col = ys_ref.shape[1] // MOE_CW

    @pl.when(first)
    def _():
        x_start(0)
        zbuf[...] = jnp.zeros_like(zbuf)
        for slot in range(YS_SLOTS):
            yflag[slot] = 0

        def start(b, carry):
            for col in range(n_col):
                tail_copy(b, col).start()
            return carry
        lax.fori_loop(used_blk, n_blk, start, 0)

    @pl.when(jnp.logical_and(valid, c == 0))
    def _():
        x_wait(i)

    @pl.when(jnp.logical_and(c == nc, i + 1 < n_items))
    def _():
        x_start(i + 1)

    @pl.when(jnp.logical_and(valid, c < nc))
    def _():
        def piece(r0, rows):
            w, bias = [], []
            bg, bu = bgu_ref[pl.ds(c, 1), :], bgu_ref[pl.ds(nc + c, 1), :]
            for s in range(MOE_CW // LANES):
                cols = slice(s * LANES, (s + 1) * LANES)
                w += [wg_ref[:, cols].astype(BF16), wu_ref[:, cols].astype(BF16)]
                bias += [bg[:, cols], bu[:, cols]]
            x = xbuf[pl.ds(r0, rows), :].astype(BF16)
            gu = (jnp.dot(x, jnp.concatenate(w, axis=1), preferred_element_type=F32)
                  + jnp.concatenate(bias, axis=1))
            for s in range(MOE_CW // LANES):
                gate = jnp.minimum(gu[:, 2 * s * LANES:(2 * s + 1) * LANES], SWIGLU_LIMIT)
                up = jnp.clip(gu[:, (2 * s + 1) * LANES:(2 * s + 2) * LANES], -SWIGLU_LIMIT, SWIGLU_LIMIT)
                a = (up + 1.0) * (gate * jax.nn.sigmoid(SWIGLU_ALPHA * gate))
                act[c, pl.ds(r0, rows), s * LANES:(s + 1) * LANES] = a.astype(BF16)

        def quad(q, carry):
            r0 = pl.multiple_of(q * 4 * MOE_RB, ROW_BLK)
            for j in range(4):
                piece(pl.multiple_of(r0 + j * MOE_RB, ROW_BLK), MOE_RB)
            return carry

        lax.fori_loop(0, n_quad, quad, 0)

        @pl.when(has_pair)
        def _():
            for j in range(2):
                piece(pl.multiple_of(base_pair + j * MOE_RB, ROW_BLK), MOE_RB)

        @pl.when(has_rb)
        def _():
            piece(base_rb, MOE_RB)

        @pl.when(has_blk)
        def _():
            piece(base_blk, ROW_BLK)

    slot_rows = [MOE_RB] * (YS_SLOTS - 1) + [ROW_BLK]

    def y_copy(slot, r0, col0):
        dst_row = pl.multiple_of(row0 + r0, ROW_BLK)
        return pltpu.make_async_copy(ystage.at[slot, pl.ds(0, slot_rows[slot])],
                                     ys_ref.at[pl.ds(dst_row, slot_rows[slot]), pl.ds(col0, MOE_CW)],
                                     ysem.at[slot])

    def y_drain_all():
        for slot in range(YS_SLOTS):
            @pl.when(yflag[slot] != 0)
            def _(slot=slot):
                y_copy(slot, 0, 0).wait()
                yflag[slot] = 0

    @pl.when(jnp.logical_and(valid, c >= nc))
    def _():
        col0 = pl.multiple_of((c - nc) * MOE_CW, MOE_CW)
        y_drain_all()

        def compute(r0, rows):
            a = jnp.concatenate([act[cc, pl.ds(r0, rows), :] for cc in range(nc)], axis=1)
            return jnp.dot(a, wd_ref[...].astype(BF16), preferred_element_type=F32) + bd_ref[pl.ds(c - nc, 1), :]

        def group(slots, base):
            for j, slot in enumerate(slots):
                ystage[slot, 0:slot_rows[slot], :] = compute(pl.multiple_of(base + j * MOE_RB, ROW_BLK),
                                                             slot_rows[slot])
            for j, slot in enumerate(slots):
                y_copy(slot, base + j * MOE_RB, col0).start()

        def quad(q, carry):
            @pl.when(q > 0)
            def _():
                for slot in (0, 1, 2, 3):
                    y_copy(slot, 0, 0).wait()

            group((0, 1, 2, 3), pl.multiple_of(q * 4 * MOE_RB, ROW_BLK))
            return carry

        lax.fori_loop(0, n_quad, quad, 0)

        @pl.when(n_quad > 0)
        def _():
            for slot in (0, 1, 2, 3):
                yflag[slot] = 1

        @pl.when(has_pair)
        def _():
            group((4, 5), base_pair)
            yflag[4] = 1
            yflag[5] = 1

        @pl.when(has_rb)
        def _():
            group((6,), base_rb)
            yflag[6] = 1

        @pl.when(has_blk)
        def _():
            group((7,), base_blk)
            yflag[7] = 1

    @pl.when(last)
    def _():
        def wait(b, carry):
            for col in range(n_col):
                tail_copy(b, col).wait()
            return carry
        lax.fori_loop(used_blk, n_blk, wait, 0)
        y_drain_all()


def _moe(xs, item_e, item_row0, item_nblk, meta, w_gate_up, b_gate_up, w_down, b_down):
    n_rows, d = xs.shape
    n_exp, _, ff2 = w_gate_up.shape
    ff = ff2 // 2
    assert w_gate_up.shape[1] == d and w_down.shape[1:] == (ff, d) and ff == d
    nc = ff // MOE_CW
    ni = item_e.shape[0]
    rbuf = MOE_MAX_BLK * ROW_BLK

    def item(i, n):
        return jnp.minimum(i, n[0] - 1)

    def c1(i, c, n):
        return jnp.where(i < n[0], jnp.minimum(c, nc - 1), nc - 1)

    def c2(i, c, n):
        return jnp.where(i < n[0], jnp.maximum(c - nc, 0), nc - 1)

    bgu3 = b_gate_up.reshape(n_exp, 2 * nc, MOE_CW)
    bd3 = b_down.reshape(n_exp, nc, MOE_CW)
    in_specs = [
        pl.BlockSpec(memory_space=pl.ANY),
        pl.BlockSpec((None, d, MOE_CW), lambda i, c, e, r, b, n: (e[item(i, n)], 0, c1(i, c, n))),
        pl.BlockSpec((None, d, MOE_CW), lambda i, c, e, r, b, n: (e[item(i, n)], 0, nc + c1(i, c, n))),
        pl.BlockSpec((None, 2 * nc, MOE_CW), lambda i, c, e, r, b, n: (e[item(i, n)], 0, 0)),
        pl.BlockSpec((None, ff, MOE_CW), lambda i, c, e, r, b, n: (e[item(i, n)], 0, c2(i, c, n))),
        pl.BlockSpec((None, nc, MOE_CW), lambda i, c, e, r, b, n: (e[item(i, n)], 0, 0)),
    ]
    grid_spec = pltpu.PrefetchScalarGridSpec(
        num_scalar_prefetch=4,
        grid=(ni, 2 * nc),
        in_specs=in_specs,
        out_specs=pl.BlockSpec(memory_space=pl.ANY),
        scratch_shapes=[pltpu.VMEM((rbuf, d), F32),
                        pltpu.VMEM((nc, rbuf, MOE_CW), BF16),
                        pltpu.VMEM((YS_SLOTS, MOE_RB, MOE_CW), F32),
                        pltpu.VMEM((ROW_BLK, MOE_CW), F32),
                        pltpu.SMEM((YS_SLOTS,), I32),
                        pltpu.SemaphoreType.DMA(()),
                        pltpu.SemaphoreType.DMA((YS_SLOTS,)),
                        pltpu.SemaphoreType.DMA(())],
    )
    need = (rbuf * d * 4 + rbuf * ff * 2 + YS_SLOTS * MOE_RB * MOE_CW * 4 + ROW_BLK * MOE_CW * 4
            + 2 * 3 * d * MOE_CW * 4)
    return pl.pallas_call(
        functools.partial(_moe_kernel, nc=nc, n_blk=n_rows // ROW_BLK),
        grid_spec=grid_spec,
        out_shape=jax.ShapeDtypeStruct((n_rows, d), F32),
        compiler_params=_params(("arbitrary", "arbitrary"), need),
        name="moe_experts",
    )(item_e, item_row0, item_nblk, meta, xs, w_gate_up, w_gate_up, bgu3, w_down, bd3)


def _combine_kernel(pos_ref, posn_ref, ys_ref, gate_ref, x1_ref, g2_ref, lng_ref, lnb_ref, o_ref, gbuf, sem, *, tc):
    i = pl.program_id(0)
    slot = i % 2

    def issue(p_ref, s):
        def body(g, carry):
            for tt in range(SUBLANES):
                for k in range(TOP_K):
                    p = p_ref[0, 0, (g * SUBLANES + tt) * TOP_K + k]
                    pltpu.make_async_copy(ys_ref.at[pl.ds(p, 1)], gbuf.at[s, k, g, pl.ds(tt, 1)],
                                          sem.at[s]).start()
            return carry
        lax.fori_loop(0, tc // SUBLANES, body, 0)

    @pl.when(i == 0)
    def _():
        issue(pos_ref, 0)

    @pl.when(i + 1 < pl.num_programs(0))
    def _():
        issue(posn_ref, 1 - slot)

    rows = ys_ref.at[pl.ds(0, tc)]
    for k in range(TOP_K):
        pltpu.make_async_copy(rows, rows, sem.at[slot]).wait()
    gt = gate_ref[...]
    d = x1_ref.shape[-1]
    ffn = gt[:, 0:1] * gbuf[slot, 0].reshape(tc, d)
    for k in range(1, TOP_K):
        ffn = ffn + gt[:, k:k + 1] * gbuf[slot, k].reshape(tc, d)
    o_ref[...] = _layer_norm(DN_ALPHA * x1_ref[...] + g2_ref[...] * ffn, lng_ref[...], lnb_ref[...])


def _combine(pos, ys, gates, x1, g2, ln_g, ln_b):
    b, l, d = x1.shape
    t = b * l
    tc = min(l, 128)
    per_b = l // tc
    n_steps = t // tc
    pos3 = pos.reshape(n_steps, 1, tc * TOP_K)
    vec = pl.BlockSpec((1, d), lambda i: (0, 0))
    out = pl.pallas_call(
        functools.partial(_combine_kernel, tc=tc),
        grid=(n_steps,),
        in_specs=[pl.BlockSpec((1, 1, tc * TOP_K), lambda i: (i, 0, 0), memory_space=pltpu.SMEM),
                  pl.BlockSpec((1, 1, tc * TOP_K), lambda i: (jnp.minimum(i + 1, n_steps - 1), 0, 0),
                               memory_space=pltpu.SMEM),
                  pl.BlockSpec(memory_space=pl.ANY),
                  pl.BlockSpec((tc, TOP_K), lambda i: (i, 0)),
                  pl.BlockSpec((tc, d), lambda i: (i, 0)),
                  pl.BlockSpec((None, 1, d), lambda i: (i // per_b, 0, 0)),
                  vec, vec],
        out_specs=pl.BlockSpec((tc, d), lambda i: (i, 0)),
        out_shape=jax.ShapeDtypeStruct((t, d), F32),
        scratch_shapes=[pltpu.VMEM((2, TOP_K, tc // SUBLANES, SUBLANES, d), F32), pltpu.SemaphoreType.DMA((2,))],
        compiler_params=_params(("arbitrary",), (2 * TOP_K + 2 * 2) * tc * d * 4),
        name="moe_combine",
    )(pos3, pos3, ys, gates, x1.reshape(t, d), g2, ln_g, ln_b)
    return out.reshape(b, l, d)


def _moe_items(counts, n_blk_total):
    n_exp = counts.shape[0]
    nblk = (counts + ROW_BLK - 1) // ROW_BLK
    blk_end = jnp.cumsum(nblk)
    blk0 = blk_end - nblk
    n_it = (nblk + MOE_MAX_BLK - 1) // MOE_MAX_BLK
    it_end = jnp.cumsum(n_it)
    it0 = it_end - n_it
    ni = n_exp + n_blk_total // MOE_MAX_BLK
    j = jnp.arange(ni, dtype=I32)
    e = jnp.minimum(jnp.sum(j[:, None] >= it_end[None, :], axis=1), n_exp - 1).astype(I32)
    local = j - it0[e]
    item_nblk = jnp.clip(nblk[e] - local * MOE_MAX_BLK, 0, MOE_MAX_BLK).astype(I32)
    item_row0 = ((blk0[e] + local * MOE_MAX_BLK) * ROW_BLK).astype(I32)
    n_items = it_end[-1].astype(I32)
    item_nblk = jnp.where(j < n_items, item_nblk, 0)
    item_row0 = jnp.where(j < n_items, item_row0, 0)
    row_start = (blk0 * ROW_BLK).astype(F32)
    used_blk = blk_end[-1].astype(I32)
    meta = jnp.stack([n_items, used_blk])
    blk = jnp.arange(n_blk_total, dtype=I32)
    partial = jnp.any((blk[:, None] == (blk_end - 1)[None, :]) & ((counts % ROW_BLK) != 0)[None, :], axis=1)
    fill = jnp.logical_or(partial, blk >= used_blk).astype(I32)
    return e, item_row0, item_nblk, meta, row_start, fill


def _mixer_stream(x, mod, hist, h0, pos0, w):
    b, l, d = x.shape
    sh1, sc1, g1, sh2, sc2, g2 = [m.reshape(b, 1, d) for m in jnp.split(mod, 6, axis=-1)]
    u_pool, u_ssm = _inproj(x, sc1, sh1, w["w_in"])
    hist16 = jnp.concatenate([jnp.zeros((b, HIST_ROWS - hist.shape[1], hist.shape[2]), F32), hist.astype(F32)], axis=1)
    y_pool, new_hist = _pool(u_pool, hist16, w["w_pool"], w["pool_scale"], pos0)
    y_ssm, h_new = _ssm(u_ssm, w["ssm"], h0)
    x1, hp, idx, gates, cnt = _outproj(y_pool, y_ssm, x, g1, sc2, sh2, w["w_out"], w["ln1_g"], w["ln1_b"],
                                       w["w_router"], w["b_router"], w["n_experts"])
    s_re, s_im = _slabs_to_state(h_new, SSM_STATE)
    return dict(x1=x1, hp=hp, idx=idx, gates=gates, cnt=cnt, g2=g2,
                hist=new_hist[:, 1:, :], s_re=s_re, s_im=s_im)


def kernel(x_prompt, x_sample, cache_pool, state_ssm_re, state_ssm_im, c_prompt, c_sample, w_ada, b_ada, w_in, w_pool, pool_scale, lambda_re, lambda_im, log_dt, ssm_b_re, ssm_b_im, ssm_c_re, ssm_c_im, d_skip, w_glu, b_glu, w_out, ln1_g, ln1_b, w_router, b_router, w_gate_up, b_gate_up, w_down, b_down, ln2_g, ln2_b):
    assert w_ada.shape[0] == DEPTH
    bp, lp, d = x_prompt.shape
    bs, ls, _ = x_sample.shape
    assert bp == SUBLANES and bs == SUBLANES, "the S5 kernel puts the 8 streams on sublanes"
    n_exp = w_router.shape[-1]
    dp = w_pool.shape[1] * w_pool.shape[2]

    w = dict(
        w_in=w_in[0].astype(BF16),
        w_pool=w_pool[0].astype(BF16),
        pool_scale=pool_scale[0].astype(F32).reshape(1, dp),
        ssm=_ssm_params(lambda_re[0], lambda_im[0], log_dt[0], ssm_b_re[0], ssm_b_im[0], ssm_c_re[0], ssm_c_im[0],
                        d_skip[0], w_glu[0], b_glu[0], SUBLANES),
        w_out=w_out[0].astype(BF16),
        ln1_g=ln1_g[0].reshape(1, d), ln1_b=ln1_b[0].reshape(1, d),
        w_router=jnp.pad(w_router[0], ((0, 0), (0, LANES - n_exp))).astype(BF16),
        b_router=jnp.pad(b_router[0].astype(F32), (0, LANES - n_exp)).reshape(1, LANES),
        n_experts=n_exp,
    )

    mod = _ada(jnp.concatenate([c_prompt, c_sample], axis=0), w_ada[0], b_ada[0].reshape(1, -1))
    n_slab = lambda_re.shape[1]
    p = _mixer_stream(x_prompt, mod[:bp], jnp.zeros((bp, HIST_ROWS - 1, dp), F32),
                      jnp.zeros((n_slab, bp, LANES), F32), 0, w)
    s = _mixer_stream(x_sample, mod[bp:], cache_pool[0], _state_to_slabs(state_ssm_re[0], state_ssm_im[0]),
                      PAST_LEN, w)

    tp, ts = bp * lp, bs * ls
    n_assign = (tp + ts) * TOP_K
    n_blk_total = -(-n_assign // ROW_BLK) + n_exp
    n_rows = n_blk_total * ROW_BLK
    counts = (p["cnt"][0, :n_exp] + s["cnt"][0, :n_exp]).astype(I32)
    item_e, item_row0, item_nblk, meta, row_start, fill = _moe_items(counts, n_blk_total)
    start_pad = jnp.zeros((SUBLANES, LANES), F32).at[0, :n_exp].set(row_start)
    pos = _plan(jnp.concatenate([p["idx"], s["idx"]], axis=0), start_pad)
    pos_p, pos_s = pos[:tp], pos[tp:]
    xs = _dispatch(pos_p, pos_s, p["hp"], s["hp"], fill, n_rows)
    ys = _moe(xs, item_e, item_row0, item_nblk, meta, w_gate_up[0], b_gate_up[0], w_down[0], b_down[0])
    g2l, b2l = ln2_g[0].reshape(1, d), ln2_b[0].reshape(1, d)
    y_p = _combine(pos_p, ys, p["gates"], p["x1"], p["g2"], g2l, b2l)
    y_s = _combine(pos_s, ys, s["gates"], s["x1"], s["g2"], g2l, b2l)

    return (y_p, y_s, p["hist"][None], p["s_re"][None], p["s_im"][None],
            s["hist"][None], s["s_re"][None], s["s_im"][None])
```

```python
import functools
import math

import jax
import jax.numpy as jnp
from jax import lax
from jax.experimental import pallas as pl
from jax.experimental.pallas import tpu as pltpu

F32 = jnp.float32
BF16 = jnp.bfloat16
I32 = jnp.int32
U32 = jnp.uint32

POOL_WINDOWS = (2, 4, 8, 16)
HIST_ROWS = 16
SSM_CH = 16
SSM_STATE = 64
TOP_K = 4
SWIGLU_LIMIT = 7.0
SWIGLU_ALPHA = 1.702
LN_EPS = 1e-5
DEPTH = 1
DN_ALPHA = (2 * DEPTH) ** 0.25
PAST_LEN = 2048

LANES = 128
SUBLANES = 8
MXU_DIM = 256
V7X_VMEM_BYTES = 64 * 1024 * 1024
MIB = 1024 * 1024

ROW_BLK = 256
MOE_RB = 512
MOE_CW = 256
MOE_MAX_BLK = 10
YS_SLOTS = 8


def _params(sem, need_bytes):
    limit = min(need_bytes + 16 * MIB, V7X_VMEM_BYTES - 4 * MIB)
    return pltpu.CompilerParams(dimension_semantics=sem, vmem_limit_bytes=limit)


def _layer_norm(v, g, b):
    mu = jnp.mean(v, axis=-1, keepdims=True)
    vc = v - mu
    var = jnp.mean(vc * vc, axis=-1, keepdims=True)
    return vc * lax.rsqrt(var + LN_EPS) * g + b


def _ada_kernel(c_ref, w_ref, b_ref, o_ref):
    c = c_ref[...]
    s = c * jax.nn.sigmoid(c)
    o_ref[...] = jnp.dot(s.astype(BF16), w_ref[...].astype(BF16), preferred_element_type=F32) + b_ref[...]


def _ada(c_all, w_ada, b_ada):
    r, d = c_all.shape
    n = w_ada.shape[1]
    tn = 1536
    return pl.pallas_call(
        _ada_kernel,
        grid=(n // tn,),
        in_specs=[pl.BlockSpec((r, d), lambda j: (0, 0)),
                  pl.BlockSpec((d, tn), lambda j: (0, j)),
                  pl.BlockSpec((1, tn), lambda j: (0, j))],
        out_specs=pl.BlockSpec((r, tn), lambda j: (0, j)),
        out_shape=jax.ShapeDtypeStruct((r, n), F32),
        compiler_params=_params(("arbitrary",), 2 * (d * tn * 4 + r * (d + tn) * 4)),
        name="ada_mod",
    )(c_all, w_ada, b_ada)


def _inproj_kernel(x_ref, sc_ref, sh_ref, w_ref, up_ref, us_ref):
    h = x_ref[...] * (1.0 + sc_ref[...]) + sh_ref[...]
    u = jnp.dot(h.astype(BF16), w_ref[...], preferred_element_type=F32)
    dp = up_ref.shape[-1]
    up_ref[...] = u[:, :dp]
    us_ref[...] = u[:, dp:]


def _inproj(x, sc, sh, w_bf):
    b, l, d = x.shape
    dm = w_bf.shape[1]
    dp = dm // 2
    tm = min(l, 512)
    row = pl.BlockSpec((None, tm, d), lambda bi, i: (bi, i, 0))
    mod = pl.BlockSpec((None, 1, d), lambda bi, i: (bi, 0, 0))
    half = pl.BlockSpec((None, tm, dp), lambda bi, i: (bi, i, 0))
    return pl.pallas_call(
        _inproj_kernel,
        grid=(b, l // tm),
        in_specs=[row, mod, mod, pl.BlockSpec((d, dm), lambda bi, i: (0, 0))],
        out_specs=[half, half],
        out_shape=[jax.ShapeDtypeStruct((b, l, dp), F32), jax.ShapeDtypeStruct((b, l, dm - dp), F32)],
        compiler_params=_params(("arbitrary", "arbitrary"), 2 * (d * dm * 2 + tm * (d + dm) * 4)),
        name="in_proj",
    )(x, sc, sh, w_bf)


def _pool_kernel(u_ref, hist_ref, w_ref, scale_ref, y_ref, nh_ref, hbuf, *, tt, pos0):
    i = pl.program_id(1)

    @pl.when(i == 0)
    def _():
        hbuf[...] = hist_ref[...]

    u = u_ref[...]
    hb = hbuf[...]
    u16 = u.astype(BF16)
    h16 = hb.astype(BF16)
    pc = u.shape[1] // len(POOL_WINDOWS)
    d_main = lax.broadcasted_iota(I32, (tt, tt), 0) - lax.broadcasted_iota(I32, (tt, tt), 1)
    d_hist = (lax.broadcasted_iota(I32, (tt, HIST_ROWS), 0) + HIST_ROWS
              - lax.broadcasted_iota(I32, (tt, HIST_ROWS), 1))
    pos = (pos0 + i * tt + lax.broadcasted_iota(I32, (tt, 1), 0)).astype(F32)
    for g, w in enumerate(POOL_WINDOWS):
        sl = slice(g * pc, (g + 1) * pc)
        bm = jnp.where(d_main >= 0, jnp.where(d_main < w, 1.0, 0.0), 0.0).astype(BF16)
        bh = jnp.where(d_hist < w, 1.0, 0.0).astype(BF16)
        s = (jnp.dot(bm, u16[:, sl], preferred_element_type=F32)
             + jnp.dot(bh, h16[:, sl], preferred_element_type=F32))
        cnt = jnp.minimum(pos + 1.0, float(w))
        diff = s / cnt - u[:, sl]
        y = jnp.dot(diff.astype(BF16), w_ref[g], preferred_element_type=F32) * scale_ref[:, sl]
        y_ref[:, sl] = y.astype(y_ref.dtype)
    hbuf[...] = u[tt - HIST_ROWS:, :]

    @pl.when(i == pl.num_programs(1) - 1)
    def _():
        nh_ref[...] = hbuf[...]


def _pool(u_pool, hist16, w_pool_bf, pool_scale, pos0):
    b, l, dp = u_pool.shape
    tt = min(l, 256)
    g, pc, _ = w_pool_bf.shape
    kern = functools.partial(_pool_kernel, tt=tt, pos0=pos0)
    return pl.pallas_call(
        kern,
        grid=(b, l // tt),
        in_specs=[pl.BlockSpec((None, tt, dp), lambda bi, i: (bi, i, 0)),
                  pl.BlockSpec((None, HIST_ROWS, dp), lambda bi, i: (bi, 0, 0)),
                  pl.BlockSpec((g, pc, pc), lambda bi, i: (0, 0, 0)),
                  pl.BlockSpec((1, dp), lambda bi, i: (0, 0))],
        out_specs=[pl.BlockSpec((None, tt, dp), lambda bi, i: (bi, i, 0)),
                   pl.BlockSpec((None, HIST_ROWS, dp), lambda bi, i: (bi, 0, 0))],
        out_shape=[jax.ShapeDtypeStruct((b, l, dp), BF16), jax.ShapeDtypeStruct((b, HIST_ROWS, dp), F32)],
        scratch_shapes=[pltpu.VMEM((HIST_ROWS, dp), F32)],
        compiler_params=_params(("arbitrary", "arbitrary"), 2 * (tt * dp * 6 + g * pc * pc * 2 + 2 * HIST_ROWS * dp * 4)),
        name="pool_mixer",
    )(u_pool, hist16, w_pool_bf, pool_scale)


SSM_PAIR_GROUP = 8
SSM_TIME_UNROLL = 2


def _ssm_kernel(u_ref, bq_ref, cq_ref, wg_ref, ar_ref, ai_ref, dsk_ref, bgl_ref, h0_ref,
                y_ref, hn_ref, s_ref, ubuf, hst, bc_ref, cc_ref, *, tt, pitch, nb):
    i = pl.program_id(0)
    n_slab = s_ref.shape[0]
    n_chunk, ppc, rb, cb = bq_ref.shape
    slab_per_chunk = n_slab // n_chunk
    cw = bc_ref.shape[1]

    @pl.when(i == 0)
    def _():
        s_ref[...] = jnp.zeros_like(s_ref)
        ubuf[...] = jnp.zeros_like(ubuf)
        hst[...] = h0_ref[...]
        bc_ref[...] = jnp.zeros_like(bc_ref)
        lane = lax.broadcasted_iota(I32, (rb, cw), 1) - lax.broadcasted_iota(I32, (rb, cw), 0)
        for k in range(n_chunk):
            for q in range(ppc):
                bc_ref[k, q * rb:(q + 1) * rb, q * cb:(q + 1) * cb] = bq_ref[k, q]
                place = jnp.where(lane == q * rb, 1.0, 0.0).astype(BF16)
                cc_ref[k, q * cb:(q + 1) * cb, :] = jnp.dot(cq_ref[k, q], place,
                                                             preferred_element_type=F32).astype(BF16)

    u = u_ref[...].reshape(nb * tt, u_ref.shape[-1])
    u16 = u.astype(BF16)
    for k in range(n_chunk):
        bu = jnp.dot(u16[:, k * cw:(k + 1) * cw], bc_ref[k], preferred_element_type=F32)
        for j in range(slab_per_chunk):
            for b in range(nb):
                s_ref[k * slab_per_chunk + j, b * pitch:b * pitch + tt, :] = (
                    bu[b * tt:(b + 1) * tt, j * LANES:(j + 1) * LANES])
    for b in range(nb):
        ubuf[b * pitch:b * pitch + tt, :] = u[b * tt:(b + 1) * tt, :]

    n_pair = n_slab // 2
    for pg in range(n_pair // SSM_PAIR_GROUP):
        qs = [pg * SSM_PAIR_GROUP + j for j in range(SSM_PAIR_GROUP)]
        ars = [ar_ref[q] for q in qs]
        ais = [ai_ref[q] for q in qs]

        def body(t2, carry, qs=qs, ars=ars, ais=ais):
            hr, hi = list(carry[0]), list(carry[1])
            for dt in range(SSM_TIME_UNROLL):
                rows = pl.ds(t2 * SSM_TIME_UNROLL + dt, nb, stride=pitch)
                for j, q in enumerate(qs):
                    bur = s_ref[2 * q, rows, :]
                    bui = s_ref[2 * q + 1, rows, :]
                    r_ = ars[j] * hr[j] - ais[j] * hi[j] + bur
                    i_ = ars[j] * hi[j] + ais[j] * hr[j] + bui
                    s_ref[2 * q, rows, :] = r_
                    s_ref[2 * q + 1, rows, :] = i_
                    hr[j], hi[j] = r_, i_
            return tuple(hr), tuple(hi)

        init = (tuple(hst[2 * q] for q in qs), tuple(hst[2 * q + 1] for q in qs))
        hr, hi = lax.fori_loop(0, tt // SSM_TIME_UNROLL, body, init)
        for j, q in enumerate(qs):
            hst[2 * q] = hr[j]
            hst[2 * q + 1] = hi[j]

    ycs = []
    for n in range(n_chunk):
        hc = jnp.concatenate([s_ref[n * slab_per_chunk + j] for j in range(slab_per_chunk)], axis=1)
        ycs.append(jnp.dot(hc.astype(BF16), cc_ref[n], preferred_element_type=F32))
    y = jnp.concatenate(ycs, axis=1) + dsk_ref[...] * ubuf[...]
    y = jax.nn.gelu(y)
    y16 = y.astype(BF16)
    z = jnp.concatenate([jnp.dot(y16[:, k * cw:(k + 1) * cw], wg_ref[k], preferred_element_type=F32)
                         for k in range(n_chunk)], axis=1) + bgl_ref[...]
    out = y * jax.nn.sigmoid(z)
    for b in range(nb):
        y_ref[b] = out[b * pitch:b * pitch + tt].astype(y_ref.dtype)

    @pl.when(i == pl.num_programs(0) - 1)
    def _():
        hn_ref[...] = hst[...]


def _ssm(u_ssm, prm, h0):
    b, l, ds = u_ssm.shape
    tt = min(l, 64)
    pitch = tt + 4
    n_slab = h0.shape[0]
    n_chunk, ppc, rb, cb = prm["bq"].shape
    kern = functools.partial(_ssm_kernel, tt=tt, pitch=pitch, nb=b)

    def const(a):
        nd = a.ndim
        return pl.BlockSpec(a.shape, lambda i, nd=nd: (0,) * nd)

    args = (u_ssm, prm["bq"], prm["cq"], prm["wg"], prm["ar"], prm["ai"], prm["dsk"], prm["bgl"], h0)
    dense = n_chunk * (ppc * rb) * (ppc * cb) * 2
    return pl.pallas_call(
        kern,
        grid=(l // tt,),
        in_specs=[pl.BlockSpec((b, tt, ds), lambda i: (0, i, 0))] + [const(a) for a in args[1:]],
        out_specs=[pl.BlockSpec((b, tt, ds), lambda i: (0, i, 0)),
                   pl.BlockSpec((n_slab, b, LANES), lambda i: (0, 0, 0))],
        out_shape=[jax.ShapeDtypeStruct((b, l, ds), BF16), jax.ShapeDtypeStruct((n_slab, b, LANES), F32)],
        scratch_shapes=[pltpu.VMEM((n_slab, b * pitch, LANES), F32),
                        pltpu.VMEM((b * pitch, ds), F32),
                        pltpu.VMEM((n_slab, b, LANES), F32),
                        pltpu.VMEM((n_chunk, ppc * rb, ppc * cb), BF16),
                        pltpu.VMEM((n_chunk, ppc * cb, ppc * rb), BF16)],
        compiler_params=_params(("arbitrary",), (n_slab * LANES + ds) * b * pitch * 4 + 2 * b * tt * ds * 6
                                + 2 * dense + 2 * sum(a.size * a.dtype.itemsize for a in args[1:])
                                + prm["cq"].size * 2 * (LANES // rb - 1) * 2),
        name="ssm_mixer",
    )(*args)


def _ssm_params(lambda_re, lambda_im, log_dt, b_re, b_im, c_re, c_im, d_skip, w_glu, b_glu, nb):
    g, p = lambda_re.shape
    ch = b_re.shape[-1]
    gpc = MXU_DIM // ch
    n_chunk = g // gpc
    ppc = gpc // 2
    lr, li = lambda_re.astype(F32), lambda_im.astype(F32)
    dt = jnp.exp(log_dt.astype(F32))[:, None]
    mag = jnp.exp(lr * dt)
    abar_re, abar_im = mag * jnp.cos(li * dt), mag * jnp.sin(li * dt)
    nr, ni = abar_re - 1.0, abar_im
    den = lr * lr + li * li
    k_re = (nr * lr + ni * li) / den
    k_im = (ni * lr - nr * li) / den
    br, bi = b_re.astype(F32), b_im.astype(F32)
    bb_re = k_re[..., None] * br - k_im[..., None] * bi
    bb_im = k_re[..., None] * bi + k_im[..., None] * br
    eye_2 = jnp.eye(2, dtype=F32)
    bb = jnp.stack([bb_re, bb_im]).reshape(2, n_chunk, ppc, 2, p, ch)
    bq = jnp.einsum("rkqgpc,gy->kqgcryp", bb, eye_2).astype(BF16).reshape(n_chunk, ppc, 2 * ch, 2 * 2 * p)
    cm = jnp.stack([c_re.astype(F32), -c_im.astype(F32)]).reshape(2, n_chunk, ppc, 2, ch, p)
    cq = jnp.einsum("rkqgcp,gy->kqrgpyc", cm, eye_2).astype(BF16).reshape(n_chunk, ppc, 2 * 2 * p, 2 * ch)
    wg = jnp.einsum("kgce,gx->kgcxe", w_glu.astype(F32).reshape(n_chunk, gpc, ch, ch),
                    jnp.eye(gpc, dtype=F32)).reshape(n_chunk, gpc * ch, gpc * ch)
    n_pair = g // 2
    ar = jnp.broadcast_to(abar_re.reshape(n_pair, 1, 2 * p), (n_pair, nb, 2 * p))
    ai = jnp.broadcast_to(abar_im.reshape(n_pair, 1, 2 * p), (n_pair, nb, 2 * p))
    return dict(bq=bq, cq=cq, wg=wg.astype(BF16), ar=ar, ai=ai,
                dsk=d_skip.astype(F32).reshape(1, g * ch), bgl=b_glu.astype(F32).reshape(1, g * ch))


def _state_to_slabs(s_re, s_im):
    b, g, p = s_re.shape
    re = s_re.astype(F32).reshape(b, g // 2, 2 * p).transpose(1, 0, 2)
    im = s_im.astype(F32).reshape(b, g // 2, 2 * p).transpose(1, 0, 2)
    return jnp.stack([re, im], axis=1).reshape(g, b, 2 * p)


def _slabs_to_state(h, p):
    n_slab, b, _ = h.shape
    h = h.reshape(n_slab // 2, 2, b, 2 * p)
    re = h[:, 0].transpose(1, 0, 2).reshape(b, n_slab, p)
    im = h[:, 1].transpose(1, 0, 2).reshape(b, n_slab, p)
    return re, im


def _outproj_kernel(yp_ref, ys_ref, x_ref, g1_ref, sc2_ref, sh2_ref, w_ref, lng_ref, lnb_ref, wr_ref, br_ref,
                    x1_ref, hp_ref, idx_ref, gate_ref, cnt_ref, *, n_experts, n_split):
    first = jnp.logical_and(pl.program_id(0) == 0, pl.program_id(1) == 0)

    @pl.when(first)
    def _():
        cnt_ref[...] = jnp.zeros_like(cnt_ref)

    tm = x_ref.shape[0]
    th = tm // n_split
    for h in range(n_split):
        rows = slice(h * th, (h + 1) * th)
        ymix = jnp.concatenate([yp_ref[rows, :], ys_ref[rows, :]], axis=1)
        mix = jnp.dot(ymix, w_ref[...], preferred_element_type=F32)
        x1 = _layer_norm(DN_ALPHA * x_ref[rows, :] + g1_ref[...] * mix, lng_ref[...], lnb_ref[...])
        x1_ref[rows, :] = x1
        h2 = x1 * (1.0 + sc2_ref[...]) + sh2_ref[...]
        hp_ref[rows, :] = h2
        logits = jnp.dot(h2.astype(BF16), wr_ref[...], preferred_element_type=F32) + br_ref[...]
        lane = lax.broadcasted_iota(I32, (th, LANES), 1).astype(F32)
        cur = jnp.where(lane < n_experts, logits, -jnp.inf)
        vals, onehot = [], jnp.zeros((th, LANES), F32)
        for k in range(TOP_K):
            m = jnp.max(cur, axis=1, keepdims=True)
            sel = jnp.min(jnp.where(cur == m, lane, float(LANES)), axis=1, keepdims=True)
            hit = lane == sel
            idx_ref[rows, k:k + 1] = sel.astype(I32)
            vals.append(m)
            onehot = onehot + jnp.where(hit, 1.0, 0.0)
            cur = jnp.where(hit, -jnp.inf, cur)
        es = [jnp.exp(v - vals[0]) for v in vals]
        den = es[0] + es[1] + es[2] + es[3]
        for k in range(TOP_K):
            gate_ref[rows, k:k + 1] = es[k] / den
        cnt_ref[0:1, :] += jnp.sum(onehot, axis=0, keepdims=True)


def _outproj(y_pool, y_ssm, x, g1, sc2, sh2, w_out_bf, ln_g, ln_b, wr_bf, br_pad, n_experts):
    b, l, d = x.shape
    dp = y_pool.shape[-1]
    tm = min(l, 512)
    n_split = 2 if tm % 512 == 0 else 1
    nl = l // tm
    t = b * l
    row = pl.BlockSpec((None, tm, d), lambda bi, i: (bi, i, 0))
    halfrow = pl.BlockSpec((None, tm, dp), lambda bi, i: (bi, i, 0))
    mod = pl.BlockSpec((None, 1, d), lambda bi, i: (bi, 0, 0))
    vec = pl.BlockSpec((1, d), lambda bi, i: (0, 0))
    tok4 = pl.BlockSpec((tm, TOP_K), lambda bi, i: (bi * nl + i, 0))
    kern = functools.partial(_outproj_kernel, n_experts=n_experts, n_split=n_split)
    return pl.pallas_call(
        kern,
        grid=(b, nl),
        in_specs=[halfrow, halfrow, row, mod, mod, mod,
                  pl.BlockSpec((d, d), lambda bi, i: (0, 0)), vec, vec,
                  pl.BlockSpec((d, LANES), lambda bi, i: (0, 0)),
                  pl.BlockSpec((1, LANES), lambda bi, i: (0, 0))],
        out_specs=[row,
                   pl.BlockSpec((tm, d), lambda bi, i: (bi * nl + i, 0)),
                   tok4, tok4,
                   pl.BlockSpec((SUBLANES, LANES), lambda bi, i: (0, 0))],
        out_shape=[jax.ShapeDtypeStruct((b, l, d), F32),
                   jax.ShapeDtypeStruct((t, d), F32),
                   jax.ShapeDtypeStruct((t, TOP_K), I32),
                   jax.ShapeDtypeStruct((t, TOP_K), F32),
                   jax.ShapeDtypeStruct((SUBLANES, LANES), F32)],
        compiler_params=_params(("arbitrary", "arbitrary"), 2 * (d * d * 2 + d * LANES * 2 + tm * d * 14)),
        name="out_proj_router",
    )(y_pool, y_ssm, x, g1, sc2, sh2, w_out_bf, ln_g, ln_b, wr_bf, br_pad)


def _plan_kernel(idx_ref, start_ref, pos_ref, carry):
    i = pl.program_id(0)

    @pl.when(i == 0)
    def _():
        carry[...] = jnp.zeros_like(carry)

    idx = idx_ref[...]
    tp = idx.shape[0]
    lane = lax.broadcasted_iota(I32, (tp, LANES), 1)
    hits = [lane == idx[:, k:k + 1] for k in range(TOP_K)]
    onehot = jnp.zeros((tp, LANES), F32)
    for h in hits:
        onehot = onehot + jnp.where(h, 1.0, 0.0)
    below = jnp.where(lax.broadcasted_iota(I32, (tp, tp), 0) > lax.broadcasted_iota(I32, (tp, tp), 1),
                      1.0, 0.0).astype(BF16)
    excl = jnp.dot(below, onehot.astype(BF16), preferred_element_type=F32)
    slot = start_ref[0:1, :] + carry[0:1, :] + excl
    for k in range(TOP_K):
        pos_ref[:, k:k + 1] = jnp.sum(jnp.where(hits[k], slot, 0.0), axis=1, keepdims=True).astype(I32)
    carry[0:1, :] += jnp.sum(onehot, axis=0, keepdims=True)


def _plan(idx_all, start_pad):
    t = idx_all.shape[0]
    tp = next(c for c in (512, 384, 256, 128, 64, 32, 16, 8) if t % c == 0)
    return pl.pallas_call(
        _plan_kernel,
        grid=(t // tp,),
        in_specs=[pl.BlockSpec((tp, TOP_K), lambda i: (i, 0)),
                  pl.BlockSpec((SUBLANES, LANES), lambda i: (0, 0))],
        out_specs=pl.BlockSpec((tp, TOP_K), lambda i: (i, 0)),
        out_shape=jax.ShapeDtypeStruct((t, TOP_K), I32),
        scratch_shapes=[pltpu.VMEM((SUBLANES, LANES), F32)],
        compiler_params=_params(("arbitrary",), 4 * tp * LANES * 4),
        name="moe_plan",
    )(idx_all, start_pad)


def _dispatch_kernel(fill_ref, posp_ref, poss_ref, hp_ref, hs_ref, xs_ref, zbuf, sem, zsem, *, n_p_steps, n_blk):
    i = pl.program_id(0)

    def fill_copy(b):
        return pltpu.make_async_copy(zbuf, xs_ref.at[pl.ds(pl.multiple_of(b * ROW_BLK, ROW_BLK), ROW_BLK)], zsem)

    @pl.when(i == 0)
    def _():
        zbuf[...] = jnp.zeros_like(zbuf)

        def start(b, carry):
            @pl.when(fill_ref[b] != 0)
            def _():
                fill_copy(b).start()
            return carry

        def wait(b, carry):
            @pl.when(fill_ref[b] != 0)
            def _():
                fill_copy(b).wait()
            return carry

        lax.fori_loop(0, n_blk, start, 0)
        lax.fori_loop(0, n_blk, wait, 0)

    def scatter(src_ref, pos_ref):
        n_grp = src_ref.shape[0]

        def body(g, carry):
            for tt in range(SUBLANES):
                for k in range(TOP_K):
                    p = pos_ref[0, 0, (g * SUBLANES + tt) * TOP_K + k]
                    pltpu.make_async_copy(src_ref.at[g, pl.ds(tt, 1)], xs_ref.at[pl.ds(p, 1)],
                                          sem).start(priority=k % 2)
            return carry

        lax.fori_loop(0, n_grp, body, 0)
        rows = xs_ref.at[pl.ds(0, n_grp * SUBLANES)]
        for k in range(TOP_K):
            pltpu.make_async_copy(rows, rows, sem).wait()

    @pl.when(i < n_p_steps)
    def _():
        scatter(hp_ref, posp_ref)

    @pl.when(i >= n_p_steps)
    def _():
        scatter(hs_ref, poss_ref)


def _dispatch(pos_p, pos_s, h_prompt, h_sample, fill_flags, n_rows):
    tp, w = h_prompt.shape
    ts = h_sample.shape[0]
    tdp, tds = math.gcd(tp, 512), math.gcd(ts, 512)
    n_p_steps, n_s_steps = tp // tdp, ts // tds
    n_blk = fill_flags.shape[0]
    grid_spec = pltpu.PrefetchScalarGridSpec(
        num_scalar_prefetch=1,
        grid=(n_p_steps + n_s_steps,),
        in_specs=[pl.BlockSpec((1, 1, tdp * TOP_K), lambda i, f: (jnp.minimum(i, n_p_steps - 1), 0, 0),
                               memory_space=pltpu.SMEM),
                  pl.BlockSpec((1, 1, tds * TOP_K), lambda i, f: (jnp.maximum(i - n_p_steps, 0), 0, 0),
                               memory_space=pltpu.SMEM),
                  pl.BlockSpec((tdp // SUBLANES, SUBLANES, w), lambda i, f: (jnp.minimum(i, n_p_steps - 1), 0, 0)),
                  pl.BlockSpec((tds // SUBLANES, SUBLANES, w), lambda i, f: (jnp.maximum(i - n_p_steps, 0), 0, 0))],
        out_specs=pl.BlockSpec(memory_space=pl.ANY),
        scratch_shapes=[pltpu.VMEM((ROW_BLK, w), F32),
                        pltpu.SemaphoreType.DMA(()), pltpu.SemaphoreType.DMA(())],
    )
    need = 2 * (tdp + tds) * w * 4 + ROW_BLK * w * 4
    return pl.pallas_call(
        functools.partial(_dispatch_kernel, n_p_steps=n_p_steps, n_blk=n_blk),
        grid_spec=grid_spec,
        out_shape=jax.ShapeDtypeStruct((n_rows, w), F32),
        compiler_params=_params(("arbitrary",), need),
        name="moe_dispatch",
    )(fill_flags, pos_p.reshape(n_p_steps, 1, tdp * TOP_K), pos_s.reshape(n_s_steps, 1, tds * TOP_K),
      h_prompt.reshape(tp // SUBLANES, SUBLANES, w), h_sample.reshape(ts // SUBLANES, SUBLANES, w))


def _moe_kernel(ie_ref, ir_ref, inb_ref, meta_ref, xs_ref, wg_ref, wu_ref, bgu_ref, wd_ref, bd_ref,
                ys_ref, xbuf, act, ystage, zbuf, yflag, xsem, ysem, zsem, *, nc, n_blk):
    i = pl.program_id(0)
    c = pl.program_id(1)
    n_items = meta_ref[0]
    used_blk = meta_ref[1]
    valid = i < n_items
    nb = inb_ref[i]
    row0 = pl.multiple_of(ir_ref[i], ROW_BLK)
    n_quad = lax.shift_right_logical(nb, 3)
    has_pair = (nb & 4) != 0
    has_rb = (nb & 2) != 0
    has_blk = (nb & 1) != 0
    base_pair = pl.multiple_of(n_quad * 4 * MOE_RB, ROW_BLK)
    base_rb = pl.multiple_of(base_pair + jnp.where(has_pair, 2 * MOE_RB, 0), ROW_BLK)
    base_blk = pl.multiple_of(base_rb + jnp.where(has_rb, MOE_RB, 0), ROW_BLK)
    first = jnp.logical_and(i == 0, c == 0)
    last = jnp.logical_and(i == pl.num_programs(0) - 1, c == pl.num_programs(1) - 1)

    def x_copy(it, j):
        src0 = pl.multiple_of(ir_ref[it] + j * ROW_BLK, ROW_BLK)
        return pltpu.make_async_copy(xs_ref.at[pl.ds(src0, ROW_BLK)],
                                     xbuf.at[pl.ds(pl.multiple_of(j * ROW_BLK, ROW_BLK), ROW_BLK)], xsem)

    def x_start(it):
        def start(j, carry):
            x_copy(it, j).start()
            return carry
        lax.fori_loop(0, inb_ref[it], start, 0)

    def x_wait(it):
        def wait(j, carry):
            x_copy(it, j).wait()
            return carry
        lax.fori_loop(0, inb_ref[it], wait, 0)

    def tail_copy(b, col):
        return pltpu.make_async_copy(
            zbuf, ys_ref.at[pl.ds(pl.multiple_of(b * ROW_BLK, ROW_BLK), ROW_BLK), pl.ds(col * MOE_CW, MOE_CW)], zsem)

    n_col = ys_ref.shape[1] // MOE_CW

    @pl.when(first)
    def _():
        x_start(0)
        zbuf[...] = jnp.zeros_like(zbuf)
        for slot in range(YS_SLOTS):
            yflag[slot] = 0

        def start(b, carry):
            for col in range(n_col):
                tail_copy(b, col).start()
            return carry
        lax.fori_loop(used_blk, n_blk, start, 0)

    @pl.when(jnp.logical_and(valid, c == 0))
    def _():
        x_wait(i)

    @pl.when(jnp.logical_and(c == nc, i + 1 < n_items))
    def _():
        x_start(i + 1)

    @pl.when(jnp.logical_and(valid, c < nc))
    def _():
        def piece(r0, rows):
            w, bias = [], []
            bg, bu = bgu_ref[pl.ds(c, 1), :], bgu_ref[pl.ds(nc + c, 1), :]
            for s in range(MOE_CW // LANES):
                cols = slice(s * LANES, (s + 1) * LANES)
                w += [wg_ref[:, cols].astype(BF16), wu_ref[:, cols].astype(BF16)]
                bias += [bg[:, cols], bu[:, cols]]
            x = xbuf[pl.ds(r0, rows), :].astype(BF16)
            gu = (jnp.dot(x, jnp.concatenate(w, axis=1), preferred_element_type=F32)
                  + jnp.concatenate(bias, axis=1))
            for s in range(MOE_CW // LANES):
                gate = jnp.minimum(gu[:, 2 * s * LANES:(2 * s + 1) * LANES], SWIGLU_LIMIT)
                up = jnp.clip(gu[:, (2 * s + 1) * LANES:(2 * s + 2) * LANES], -SWIGLU_LIMIT, SWIGLU_LIMIT)
                a = (up + 1.0) * (gate * jax.nn.sigmoid(SWIGLU_ALPHA * gate))
                act[c, pl.ds(r0, rows), s * LANES:(s + 1) * LANES] = a.astype(BF16)

        def quad(q, carry):
            r0 = pl.multiple_of(q * 4 * MOE_RB, ROW_BLK)
            for j in range(4):
                piece(pl.multiple_of(r0 + j * MOE_RB, ROW_BLK), MOE_RB)
            return carry

        lax.fori_loop(0, n_quad, quad, 0)

        @pl.when(has_pair)
        def _():
            for j in range(2):
                piece(pl.multiple_of(base_pair + j * MOE_RB, ROW_BLK), MOE_RB)

        @pl.when(has_rb)
        def _():
            piece(base_rb, MOE_RB)

        @pl.when(has_blk)
        def _():
            piece(base_blk, ROW_BLK)

    slot_rows = [MOE_RB] * (YS_SLOTS - 1) + [ROW_BLK]

    def y_copy(slot, r0, col0):
        dst_row = pl.multiple_of(row0 + r0, ROW_BLK)
        return pltpu.make_async_copy(ystage.at[slot, pl.ds(0, slot_rows[slot])],
                                     ys_ref.at[pl.ds(dst_row, slot_rows[slot]), pl.ds(col0, MOE_CW)],
                                     ysem.at[slot])

    def y_drain_all():
        for slot in range(YS_SLOTS):
            @pl.when(yflag[slot] != 0)
            def _(slot=slot):
                y_copy(slot, 0, 0).wait()
                yflag[slot] = 0

    @pl.when(jnp.logical_and(valid, c >= nc))
    def _():
        col0 = pl.multiple_of((c - nc) * MOE_CW, MOE_CW)
        y_drain_all()

        def compute(r0, rows):
            a = jnp.concatenate([act[cc, pl.ds(r0, rows), :] for cc in range(nc)], axis=1)
            return jnp.dot(a, wd_ref[...].astype(BF16), preferred_element_type=F32) + bd_ref[pl.ds(c - nc, 1), :]

        def group(slots, base):
            for j, slot in enumerate(slots):
                ystage[slot, 0:slot_rows[slot], :] = compute(pl.multiple_of(base + j * MOE_RB, ROW_BLK),
                                                             slot_rows[slot])
            for j, slot in enumerate(slots):
                y_copy(slot, base + j * MOE_RB, col0).start()

        def quad(q, carry):
            @pl.when(q > 0)
            def _():
                for slot in (0, 1, 2, 3):
                    y_copy(slot, 0, 0).wait()

            group((0, 1, 2, 3), pl.multiple_of(q * 4 * MOE_RB, ROW_BLK))
            return carry

        lax.fori_loop(0, n_quad, quad, 0)

        @pl.when(n_quad > 0)
        def _():
            for slot in (0, 1, 2, 3):
                yflag[slot] = 1

        @pl.when(has_pair)
        def _():
            group((4, 5), base_pair)
            yflag[4] = 1
            yflag[5] = 1

        @pl.when(has_rb)
        def _():
            group((6,), base_rb)
            yflag[6] = 1

        @pl.when(has_blk)
        def _():
            group((7,), base_blk)
            yflag[7] = 1

    @pl.when(last)
    def _():
        def wait(b, carry):
            for col in range(n_col):
                tail_copy(b, col).wait()
            return carry
        lax.fori_loop(used_blk, n_blk, wait, 0)
        y_drain_all()


def _moe(xs, item_e, item_row0, item_nblk, meta, w_gate_up, b_gate_up, w_down, b_down):
    n_rows, d = xs.shape
    n_exp, _, ff2 = w_gate_up.shape
    ff = ff2 // 2
    assert w_gate_up.shape[1] == d and w_down.shape[1:] == (ff, d) and ff == d
    nc = ff // MOE_CW
    ni = item_e.shape[0]
    rbuf = MOE_MAX_BLK * ROW_BLK

    def item(i, n):
        return jnp.minimum(i, n[0] - 1)

    def c1(i, c, n):
        return jnp.where(i < n[0], jnp.minimum(c, nc - 1), nc - 1)

    def c2(i, c, n):
        return jnp.where(i < n[0], jnp.maximum(c - nc, 0), nc - 1)

    bgu3 = b_gate_up.reshape(n_exp, 2 * nc, MOE_CW)
    bd3 = b_down.reshape(n_exp, nc, MOE_CW)
    in_specs = [
        pl.BlockSpec(memory_space=pl.ANY),
        pl.BlockSpec((None, d, MOE_CW), lambda i, c, e, r, b, n: (e[item(i, n)], 0, c1(i, c, n))),
        pl.BlockSpec((None, d, MOE_CW), lambda i, c, e, r, b, n: (e[item(i, n)], 0, nc + c1(i, c, n))),
        pl.BlockSpec((None, 2 * nc, MOE_CW), lambda i, c, e, r, b, n: (e[item(i, n)], 0, 0)),
        pl.BlockSpec((None, ff, MOE_CW), lambda i, c, e, r, b, n: (e[item(i, n)], 0, c2(i, c, n))),
        pl.BlockSpec((None, nc, MOE_CW), lambda i, c, e, r, b, n: (e[item(i, n)], 0, 0)),
    ]
    grid_spec = pltpu.PrefetchScalarGridSpec(
        num_scalar_prefetch=4,
        grid=(ni, 2 * nc),
        in_specs=in_specs,
        out_specs=pl.BlockSpec(memory_space=pl.ANY),
        scratch_shapes=[pltpu.VMEM((rbuf, d), F32),
                        pltpu.VMEM((nc, rbuf, MOE_CW), BF16),
                        pltpu.VMEM((YS_SLOTS, MOE_RB, MOE_CW), F32),
                        pltpu.VMEM((ROW_BLK, MOE_CW), F32),
                        pltpu.SMEM((YS_SLOTS,), I32),
                        pltpu.SemaphoreType.DMA(()),
                        pltpu.SemaphoreType.DMA((YS_SLOTS,)),
                        pltpu.SemaphoreType.DMA(())],
    )
    need = (rbuf * d * 4 + rbuf * ff * 2 + YS_SLOTS * MOE_RB * MOE_CW * 4 + ROW_BLK * MOE_CW * 4
            + 2 * 3 * d * MOE_CW * 4)
    return pl.pallas_call(
        functools.partial(_moe_kernel, nc=nc, n_blk=n_rows // ROW_BLK),
        grid_spec=grid_spec,
        out_shape=jax.ShapeDtypeStruct((n_rows, d), F32),
        compiler_params=_params(("arbitrary", "arbitrary"), need),
        name="moe_experts",
    )(item_e, item_row0, item_nblk, meta, xs, w_gate_up, w_gate_up, bgu3, w_down, bd3)


def _combine_kernel(pos_ref, posn_ref, ys_ref, gate_ref, x1_ref, g2_ref, lng_ref, lnb_ref, o_ref, gbuf, sem, *, tc):
    i = pl.program_id(0)
    slot = i % 2

    def issue(p_ref, s):
        def body(g, carry):
            for tt in range(SUBLANES):
                for k in range(TOP_K):
                    p = p_ref[0, 0, (g * SUBLANES + tt) * TOP_K + k]
                    pltpu.make_async_copy(ys_ref.at[pl.ds(p, 1)], gbuf.at[s, k, g, pl.ds(tt, 1)],
                                          sem.at[s]).start(priority=k % 2)
            return carry
        lax.fori_loop(0, tc // SUBLANES, body, 0)

    @pl.when(i == 0)
    def _():
        issue(pos_ref, 0)

    @pl.when(i + 1 < pl.num_programs(0))
    def _():
        issue(posn_ref, 1 - slot)

    rows = ys_ref.at[pl.ds(0, tc)]
    for k in range(TOP_K):
        pltpu.make_async_copy(rows, rows, sem.at[slot]).wait()
    gt = gate_ref[...]
    d = x1_ref.shape[-1]
    ffn = gt[:, 0:1] * gbuf[slot, 0].reshape(tc, d)
    for k in range(1, TOP_K):
        ffn = ffn + gt[:, k:k + 1] * gbuf[slot, k].reshape(tc, d)
    o_ref[...] = _layer_norm(DN_ALPHA * x1_ref[...] + g2_ref[...] * ffn, lng_ref[...], lnb_ref[...])


def _combine(pos, ys, gates, x1, g2, ln_g, ln_b):
    b, l, d = x1.shape
    t = b * l
    tc = min(l, 128)
    per_b = l // tc
    n_steps = t // tc
    pos3 = pos.reshape(n_steps, 1, tc * TOP_K)
    vec = pl.BlockSpec((1, d), lambda i: (0, 0))
    out = pl.pallas_call(
        functools.partial(_combine_kernel, tc=tc),
        grid=(n_steps,),
        in_specs=[pl.BlockSpec((1, 1, tc * TOP_K), lambda i: (i, 0, 0), memory_space=pltpu.SMEM),
                  pl.BlockSpec((1, 1, tc * TOP_K), lambda i: (jnp.minimum(i + 1, n_steps - 1), 0, 0),
                               memory_space=pltpu.SMEM),
                  pl.BlockSpec(memory_space=pl.ANY),
                  pl.BlockSpec((tc, TOP_K), lambda i: (i, 0)),
                  pl.BlockSpec((tc, d), lambda i: (i, 0)),
                  pl.BlockSpec((None, 1, d), lambda i: (i // per_b, 0, 0)),
                  vec, vec],
        out_specs=pl.BlockSpec((tc, d), lambda i: (i, 0)),
        out_shape=jax.ShapeDtypeStruct((t, d), F32),
        scratch_shapes=[pltpu.VMEM((2, TOP_K, tc // SUBLANES, SUBLANES, d), F32), pltpu.SemaphoreType.DMA((2,))],
        compiler_params=_params(("arbitrary",), (2 * TOP_K + 2 * 2) * tc * d * 4),
        name="moe_combine",
    )(pos3, pos3, ys, gates, x1.reshape(t, d), g2, ln_g, ln_b)
    return out.reshape(b, l, d)


def _moe_items(counts, n_blk_total):
    n_exp = counts.shape[0]
    nblk = (counts + ROW_BLK - 1) // ROW_BLK
    blk_end = jnp.cumsum(nblk)
    blk0 = blk_end - nblk
    n_it = (nblk + MOE_MAX_BLK - 1) // MOE_MAX_BLK
    it_end = jnp.cumsum(n_it)
    it0 = it_end - n_it
    ni = n_exp + n_blk_total // MOE_MAX_BLK
    j = jnp.arange(ni, dtype=I32)
    e = jnp.minimum(jnp.sum(j[:, None] >= it_end[None, :], axis=1), n_exp - 1).astype(I32)
    local = j - it0[e]
    item_nblk = jnp.clip(nblk[e] - local * MOE_MAX_BLK, 0, MOE_MAX_BLK).astype(I32)
    item_row0 = ((blk0[e] + local * MOE_MAX_BLK) * ROW_BLK).astype(I32)
    n_items = it_end[-1].astype(I32)
    item_nblk = jnp.where(j < n_items, item_nblk, 0)
    item_row0 = jnp.where(j < n_items, item_row0, 0)
    row_start = (blk0 * ROW_BLK).astype(F32)
    used_blk = blk_end[-1].astype(I32)
    meta = jnp.stack([n_items, used_blk])
    blk = jnp.arange(n_blk_total, dtype=I32)
    partial = jnp.any((blk[:, None] == (blk_end - 1)[None, :]) & ((counts % ROW_BLK) != 0)[None, :], axis=1)
    fill = jnp.logical_or(partial, blk >= used_blk).astype(I32)
    return e, item_row0, item_nblk, meta, row_start, fill


def _mixer_stream(x, mod, hist, h0, pos0, w):
    b, l, d = x.shape
    sh1, sc1, g1, sh2, sc2, g2 = [m.reshape(b, 1, d) for m in jnp.split(mod, 6, axis=-1)]
    u_pool, u_ssm = _inproj(x, sc1, sh1, w["w_in"])
    hist16 = jnp.concatenate([jnp.zeros((b, HIST_ROWS - hist.shape[1], hist.shape[2]), F32), hist.astype(F32)], axis=1)
    y_pool, new_hist = _pool(u_pool, hist16, w["w_pool"], w["pool_scale"], pos0)
    y_ssm, h_new = _ssm(u_ssm, w["ssm"], h0)
    x1, hp, idx, gates, cnt = _outproj(y_pool, y_ssm, x, g1, sc2, sh2, w["w_out"], w["ln1_g"], w["ln1_b"],
                                       w["w_router"], w["b_router"], w["n_experts"])
    s_re, s_im = _slabs_to_state(h_new, SSM_STATE)
    return dict(x1=x1, hp=hp, idx=idx, gates=gates, cnt=cnt, g2=g2,
                hist=new_hist[:, 1:, :], s_re=s_re, s_im=s_im)


def kernel(x_prompt, x_sample, cache_pool, state_ssm_re, state_ssm_im, c_prompt, c_sample, w_ada, b_ada, w_in, w_pool, pool_scale, lambda_re, lambda_im, log_dt, ssm_b_re, ssm_b_im, ssm_c_re, ssm_c_im, d_skip, w_glu, b_glu, w_out, ln1_g, ln1_b, w_router, b_router, w_gate_up, b_gate_up, w_down, b_down, ln2_g, ln2_b):
    assert w_ada.shape[0] == DEPTH
    bp, lp, d = x_prompt.shape
    bs, ls, _ = x_sample.shape
    assert bp == SUBLANES and bs == SUBLANES, "the S5 kernel puts the 8 streams on sublanes"
    n_exp = w_router.shape[-1]
    dp = w_pool.shape[1] * w_pool.shape[2]

    w = dict(
        w_in=w_in[0].astype(BF16),
        w_pool=w_pool[0].astype(BF16),
        pool_scale=pool_scale[0].astype(F32).reshape(1, dp),
        ssm=_ssm_params(lambda_re[0], lambda_im[0], log_dt[0], ssm_b_re[0], ssm_b_im[0], ssm_c_re[0], ssm_c_im[0],
                        d_skip[0], w_glu[0], b_glu[0], SUBLANES),
        w_out=w_out[0].astype(BF16),
        ln1_g=ln1_g[0].reshape(1, d), ln1_b=ln1_b[0].reshape(1, d),
        w_router=jnp.pad(w_router[0], ((0, 0), (0, LANES - n_exp))).astype(BF16),
        b_router=jnp.pad(b_router[0].astype(F32), (0, LANES - n_exp)).reshape(1, LANES),
        n_experts=n_exp,
    )

    mod = _ada(jnp.concatenate([c_prompt, c_sample], axis=0), w_ada[0], b_ada[0].reshape(1, -1))
    n_slab = lambda_re.shape[1]
    p = _mixer_stream(x_prompt, mod[:bp], jnp.zeros((bp, HIST_ROWS - 1, dp), F32),
                      jnp.zeros((n_slab, bp, LANES), F32), 0, w)
    s = _mixer_stream(x_sample, mod[bp:], cache_pool[0], _state_to_slabs(state_ssm_re[0], state_ssm_im[0]),
                      PAST_LEN, w)

    tp, ts = bp * lp, bs * ls
    n_assign = (tp + ts) * TOP_K
    n_blk_total = -(-n_assign // ROW_BLK) + n_exp
    n_rows = n_blk_total * ROW_BLK
    counts = (p["cnt"][0, :n_exp] + s["cnt"][0, :n_exp]).astype(I32)
    item_e, item_row0, item_nblk, meta, row_start, fill = _moe_items(counts, n_blk_total)
    start_pad = jnp.zeros((SUBLANES, LANES), F32).at[0, :n_exp].set(row_start)
    pos = _plan(jnp.concatenate([p["idx"], s["idx"]], axis=0), start_pad)
    pos_p, pos_s = pos[:tp], pos[tp:]
    xs = _dispatch(pos_p, pos_s, p["hp"], s["hp"], fill, n_rows)
    ys = _moe(xs, item_e, item_row0, item_nblk, meta, w_gate_up[0], b_gate_up[0], w_down[0], b_down[0])
    g2l, b2l = ln2_g[0].reshape(1, d), ln2_b[0].reshape(1, d)
    y_p = _combine(pos_p, ys, p["gates"], p["x1"], p["g2"], g2l, b2l)
    y_s = _combine(pos_s, ys, s["gates"], s["x1"], s["g2"], g2l, b2l)

    return (y_p, y_s, p["hist"][None], p["s_re"][None], p["s_im"][None],
            s["hist"][None], s["s_re"][None], s["s_im"][None])
```

```python
import functools
import math

import jax
import jax.numpy as jnp
from jax import lax
from jax.experimental import pallas as pl
from jax.experimental.pallas import tpu as pltpu

F32 = jnp.float32
BF16 = jnp.bfloat16
I32 = jnp.int32
U32 = jnp.uint32

POOL_WINDOWS = (2, 4, 8, 16)
HIST_ROWS = 16
SSM_CH = 16
SSM_STATE = 64
TOP_K = 4
SWIGLU_LIMIT = 7.0
SWIGLU_ALPHA = 1.702
LN_EPS = 1e-5
DEPTH = 1
DN_ALPHA = (2 * DEPTH) ** 0.25
PAST_LEN = 2048

LANES = 128
SUBLANES = 8
MXU_DIM = 256
V7X_VMEM_BYTES = 64 * 1024 * 1024
MIB = 1024 * 1024

ROW_BLK = 256
MOE_RB = 512
MOE_CW = 256
MOE_OW = 512
MOE_MAX_BLK = 10
YS_SLOTS = 8


def _params(sem, need_bytes):
    limit = min(need_bytes + 16 * MIB, V7X_VMEM_BYTES - 4 * MIB)
    return pltpu.CompilerParams(dimension_semantics=sem, vmem_limit_bytes=limit)


def _layer_norm(v, g, b):
    mu = jnp.mean(v, axis=-1, keepdims=True)
    vc = v - mu
    var = jnp.mean(vc * vc, axis=-1, keepdims=True)
    return vc * lax.rsqrt(var + LN_EPS) * g + b


def _ada_kernel(c_ref, w_ref, b_ref, o_ref):
    c = c_ref[...]
    s = c * jax.nn.sigmoid(c)
    o_ref[...] = jnp.dot(s.astype(BF16), w_ref[...].astype(BF16), preferred_element_type=F32) + b_ref[...]


def _ada(c_all, w_ada, b_ada):
    r, d = c_all.shape
    n = w_ada.shape[1]
    tn = 1536
    return pl.pallas_call(
        _ada_kernel,
        grid=(n // tn,),
        in_specs=[pl.BlockSpec((r, d), lambda j: (0, 0)),
                  pl.BlockSpec((d, tn), lambda j: (0, j)),
                  pl.BlockSpec((1, tn), lambda j: (0, j))],
        out_specs=pl.BlockSpec((r, tn), lambda j: (0, j)),
        out_shape=jax.ShapeDtypeStruct((r, n), F32),
        compiler_params=_params(("arbitrary",), 2 * (d * tn * 4 + r * (d + tn) * 4)),
        name="ada_mod",
    )(c_all, w_ada, b_ada)


def _inproj_kernel(x_ref, sc_ref, sh_ref, w_ref, up_ref, us_ref):
    h = x_ref[...] * (1.0 + sc_ref[...]) + sh_ref[...]
    u = jnp.dot(h.astype(BF16), w_ref[...], preferred_element_type=F32)
    dp = up_ref.shape[-1]
    up_ref[...] = u[:, :dp]
    us_ref[...] = u[:, dp:]


def _inproj(x, sc, sh, w_bf):
    b, l, d = x.shape
    dm = w_bf.shape[1]
    dp = dm // 2
    tm = min(l, 512)
    row = pl.BlockSpec((None, tm, d), lambda bi, i: (bi, i, 0))
    mod = pl.BlockSpec((None, 1, d), lambda bi, i: (bi, 0, 0))
    half = pl.BlockSpec((None, tm, dp), lambda bi, i: (bi, i, 0))
    return pl.pallas_call(
        _inproj_kernel,
        grid=(b, l // tm),
        in_specs=[row, mod, mod, pl.BlockSpec((d, dm), lambda bi, i: (0, 0))],
        out_specs=[half, half],
        out_shape=[jax.ShapeDtypeStruct((b, l, dp), F32), jax.ShapeDtypeStruct((b, l, dm - dp), F32)],
        compiler_params=_params(("arbitrary", "arbitrary"), 2 * (d * dm * 2 + tm * (d + dm) * 4)),
        name="in_proj",
    )(x, sc, sh, w_bf)


def _pool_kernel(u_ref, hist_ref, w_ref, scale_ref, y_ref, nh_ref, hbuf, *, tt, pos0):
    i = pl.program_id(1)

    @pl.when(i == 0)
    def _():
        hbuf[...] = hist_ref[...]

    u = u_ref[...]
    hb = hbuf[...]
    u16 = u.astype(BF16)
    h16 = hb.astype(BF16)
    pc = u.shape[1] // len(POOL_WINDOWS)
    d_main = lax.broadcasted_iota(I32, (tt, tt), 0) - lax.broadcasted_iota(I32, (tt, tt), 1)
    d_hist = (lax.broadcasted_iota(I32, (tt, HIST_ROWS), 0) + HIST_ROWS
              - lax.broadcasted_iota(I32, (tt, HIST_ROWS), 1))
    pos = (pos0 + i * tt + lax.broadcasted_iota(I32, (tt, 1), 0)).astype(F32)
    for g, w in enumerate(POOL_WINDOWS):
        sl = slice(g * pc, (g + 1) * pc)
        bm = jnp.where(d_main >= 0, jnp.where(d_main < w, 1.0, 0.0), 0.0).astype(BF16)
        bh = jnp.where(d_hist < w, 1.0, 0.0).astype(BF16)
        s = (jnp.dot(bm, u16[:, sl], preferred_element_type=F32)
             + jnp.dot(bh, h16[:, sl], preferred_element_type=F32))
        cnt = jnp.minimum(pos + 1.0, float(w))
        diff = s / cnt - u[:, sl]
        y = jnp.dot(diff.astype(BF16), w_ref[g], preferred_element_type=F32) * scale_ref[:, sl]
        y_ref[:, sl] = y.astype(y_ref.dtype)
    hbuf[...] = u[tt - HIST_ROWS:, :]

    @pl.when(i == pl.num_programs(1) - 1)
    def _():
        nh_ref[...] = hbuf[...]


def _pool(u_pool, hist16, w_pool_bf, pool_scale, pos0):
    b, l, dp = u_pool.shape
    tt = min(l, 256)
    g, pc, _ = w_pool_bf.shape
    kern = functools.partial(_pool_kernel, tt=tt, pos0=pos0)
    return pl.pallas_call(
        kern,
        grid=(b, l // tt),
        in_specs=[pl.BlockSpec((None, tt, dp), lambda bi, i: (bi, i, 0)),
                  pl.BlockSpec((None, HIST_ROWS, dp), lambda bi, i: (bi, 0, 0)),
                  pl.BlockSpec((g, pc, pc), lambda bi, i: (0, 0, 0)),
                  pl.BlockSpec((1, dp), lambda bi, i: (0, 0))],
        out_specs=[pl.BlockSpec((None, tt, dp), lambda bi, i: (bi, i, 0)),
                   pl.BlockSpec((None, HIST_ROWS, dp), lambda bi, i: (bi, 0, 0))],
        out_shape=[jax.ShapeDtypeStruct((b, l, dp), BF16), jax.ShapeDtypeStruct((b, HIST_ROWS, dp), F32)],
        scratch_shapes=[pltpu.VMEM((HIST_ROWS, dp), F32)],
        compiler_params=_params(("arbitrary", "arbitrary"), 2 * (tt * dp * 6 + g * pc * pc * 2 + 2 * HIST_ROWS * dp * 4)),
        name="pool_mixer",
    )(u_pool, hist16, w_pool_bf, pool_scale)


SSM_PAIR_GROUP = 8
SSM_TIME_UNROLL = 2


def _ssm_kernel(u_ref, bq_ref, cq_ref, wg_ref, ar_ref, ai_ref, dsk_ref, bgl_ref, h0_ref,
                y_ref, hn_ref, s_ref, ubuf, hst, bc_ref, cc_ref, *, tt, pitch, nb):
    i = pl.program_id(0)
    n_slab = s_ref.shape[0]
    n_chunk, ppc, rb, cb = bq_ref.shape
    slab_per_chunk = n_slab // n_chunk
    cw = bc_ref.shape[1]

    @pl.when(i == 0)
    def _():
        s_ref[...] = jnp.zeros_like(s_ref)
        ubuf[...] = jnp.zeros_like(ubuf)
        hst[...] = h0_ref[...]
        bc_ref[...] = jnp.zeros_like(bc_ref)
        lane = lax.broadcasted_iota(I32, (rb, cw), 1) - lax.broadcasted_iota(I32, (rb, cw), 0)
        for k in range(n_chunk):
            for q in range(ppc):
                bc_ref[k, q * rb:(q + 1) * rb, q * cb:(q + 1) * cb] = bq_ref[k, q]
                place = jnp.where(lane == q * rb, 1.0, 0.0).astype(BF16)
                cc_ref[k, q * cb:(q + 1) * cb, :] = jnp.dot(cq_ref[k, q], place,
                                                             preferred_element_type=F32).astype(BF16)

    u = u_ref[...].reshape(nb * tt, u_ref.shape[-1])
    u16 = u.astype(BF16)
    for k in range(n_chunk):
        bu = jnp.dot(u16[:, k * cw:(k + 1) * cw], bc_ref[k], preferred_element_type=F32)
        for j in range(slab_per_chunk):
            for b in range(nb):
                s_ref[k * slab_per_chunk + j, b * pitch:b * pitch + tt, :] = (
                    bu[b * tt:(b + 1) * tt, j * LANES:(j + 1) * LANES])
    for b in range(nb):
        ubuf[b * pitch:b * pitch + tt, :] = u[b * tt:(b + 1) * tt, :]

    n_pair = n_slab // 2
    for pg in range(n_pair // SSM_PAIR_GROUP):
        qs = [pg * SSM_PAIR_GROUP + j for j in range(SSM_PAIR_GROUP)]
        ars = [ar_ref[q] for q in qs]
        ais = [ai_ref[q] for q in qs]

        def body(t2, carry, qs=qs, ars=ars, ais=ais):
            hr, hi = list(carry[0]), list(carry[1])
            for dt in range(SSM_TIME_UNROLL):
                rows = pl.ds(t2 * SSM_TIME_UNROLL + dt, nb, stride=pitch)
                for j, q in enumerate(qs):
                    bur = s_ref[2 * q, rows, :]
                    bui = s_ref[2 * q + 1, rows, :]
                    r_ = ars[j] * hr[j] - ais[j] * hi[j] + bur
                    i_ = ars[j] * hi[j] + ais[j] * hr[j] + bui
                    s_ref[2 * q, rows, :] = r_
                    s_ref[2 * q + 1, rows, :] = i_
                    hr[j], hi[j] = r_, i_
            return tuple(hr), tuple(hi)

        init = (tuple(hst[2 * q] for q in qs), tuple(hst[2 * q + 1] for q in qs))
        hr, hi = lax.fori_loop(0, tt // SSM_TIME_UNROLL, body, init)
        for j, q in enumerate(qs):
            hst[2 * q] = hr[j]
            hst[2 * q + 1] = hi[j]

    ycs = []
    for n in range(n_chunk):
        hc = jnp.concatenate([s_ref[n * slab_per_chunk + j] for j in range(slab_per_chunk)], axis=1)
        ycs.append(jnp.dot(hc.astype(BF16), cc_ref[n], preferred_element_type=F32))
    y = jnp.concatenate(ycs, axis=1) + dsk_ref[...] * ubuf[...]
    y = jax.nn.gelu(y)
    y16 = y.astype(BF16)
    z = jnp.concatenate([jnp.dot(y16[:, k * cw:(k + 1) * cw], wg_ref[k], preferred_element_type=F32)
                         for k in range(n_chunk)], axis=1) + bgl_ref[...]
    out = y * jax.nn.sigmoid(z)
    for b in range(nb):
        y_ref[b] = out[b * pitch:b * pitch + tt].astype(y_ref.dtype)

    @pl.when(i == pl.num_programs(0) - 1)
    def _():
        hn_ref[...] = hst[...]


def _ssm(u_ssm, prm, h0):
    b, l, ds = u_ssm.shape
    tt = min(l, 64)
    pitch = tt + 4
    n_slab = h0.shape[0]
    n_chunk, ppc, rb, cb = prm["bq"].shape
    kern = functools.partial(_ssm_kernel, tt=tt, pitch=pitch, nb=b)

    def const(a):
        nd = a.ndim
        return pl.BlockSpec(a.shape, lambda i, nd=nd: (0,) * nd)

    args = (u_ssm, prm["bq"], prm["cq"], prm["wg"], prm["ar"], prm["ai"], prm["dsk"], prm["bgl"], h0)
    dense = n_chunk * (ppc * rb) * (ppc * cb) * 2
    return pl.pallas_call(
        kern,
        grid=(l // tt,),
        in_specs=[pl.BlockSpec((b, tt, ds), lambda i: (0, i, 0))] + [const(a) for a in args[1:]],
        out_specs=[pl.BlockSpec((b, tt, ds), lambda i: (0, i, 0)),
                   pl.BlockSpec((n_slab, b, LANES), lambda i: (0, 0, 0))],
        out_shape=[jax.ShapeDtypeStruct((b, l, ds), BF16), jax.ShapeDtypeStruct((n_slab, b, LANES), F32)],
        scratch_shapes=[pltpu.VMEM((n_slab, b * pitch, LANES), F32),
                        pltpu.VMEM((b * pitch, ds), F32),
                        pltpu.VMEM((n_slab, b, LANES), F32),
                        pltpu.VMEM((n_chunk, ppc * rb, ppc * cb), BF16),
                        pltpu.VMEM((n_chunk, ppc * cb, ppc * rb), BF16)],
        compiler_params=_params(("arbitrary",), (n_slab * LANES + ds) * b * pitch * 4 + 2 * b * tt * ds * 6
                                + 2 * dense + 2 * sum(a.size * a.dtype.itemsize for a in args[1:])
                                + prm["cq"].size * 2 * (LANES // rb - 1) * 2),
        name="ssm_mixer",
    )(*args)


def _ssm_params(lambda_re, lambda_im, log_dt, b_re, b_im, c_re, c_im, d_skip, w_glu, b_glu, nb):
    g, p = lambda_re.shape
    ch = b_re.shape[-1]
    gpc = MXU_DIM // ch
    n_chunk = g // gpc
    ppc = gpc // 2
    lr, li = lambda_re.astype(F32), lambda_im.astype(F32)
    dt = jnp.exp(log_dt.astype(F32))[:, None]
    mag = jnp.exp(lr * dt)
    abar_re, abar_im = mag * jnp.cos(li * dt), mag * jnp.sin(li * dt)
    nr, ni = abar_re - 1.0, abar_im
    den = lr * lr + li * li
    k_re = (nr * lr + ni * li) / den
    k_im = (ni * lr - nr * li) / den
    br, bi = b_re.astype(F32), b_im.astype(F32)
    bb_re = k_re[..., None] * br - k_im[..., None] * bi
    bb_im = k_re[..., None] * bi + k_im[..., None] * br
    eye_2 = jnp.eye(2, dtype=F32)
    bb = jnp.stack([bb_re, bb_im]).reshape(2, n_chunk, ppc, 2, p, ch)
    bq = jnp.einsum("rkqgpc,gy->kqgcryp", bb, eye_2).astype(BF16).reshape(n_chunk, ppc, 2 * ch, 2 * 2 * p)
    cm = jnp.stack([c_re.astype(F32), -c_im.astype(F32)]).reshape(2, n_chunk, ppc, 2, ch, p)
    cq = jnp.einsum("rkqgcp,gy->kqrgpyc", cm, eye_2).astype(BF16).reshape(n_chunk, ppc, 2 * 2 * p, 2 * ch)
    wg = jnp.einsum("kgce,gx->kgcxe", w_glu.astype(F32).reshape(n_chunk, gpc, ch, ch),
                    jnp.eye(gpc, dtype=F32)).reshape(n_chunk, gpc * ch, gpc * ch)
    n_pair = g // 2
    ar = jnp.broadcast_to(abar_re.reshape(n_pair, 1, 2 * p), (n_pair, nb, 2 * p))
    ai = jnp.broadcast_to(abar_im.reshape(n_pair, 1, 2 * p), (n_pair, nb, 2 * p))
    return dict(bq=bq, cq=cq, wg=wg.astype(BF16), ar=ar, ai=ai,
                dsk=d_skip.astype(F32).reshape(1, g * ch), bgl=b_glu.astype(F32).reshape(1, g * ch))


def _state_to_slabs(s_re, s_im):
    b, g, p = s_re.shape
    re = s_re.astype(F32).reshape(b, g // 2, 2 * p).transpose(1, 0, 2)
    im = s_im.astype(F32).reshape(b, g // 2, 2 * p).transpose(1, 0, 2)
    return jnp.stack([re, im], axis=1).reshape(g, b, 2 * p)


def _slabs_to_state(h, p):
    n_slab, b, _ = h.shape
    h = h.reshape(n_slab // 2, 2, b, 2 * p)
    re = h[:, 0].transpose(1, 0, 2).reshape(b, n_slab, p)
    im = h[:, 1].transpose(1, 0, 2).reshape(b, n_slab, p)
    return re, im


def _outproj_kernel(yp_ref, ys_ref, x_ref, g1_ref, sc2_ref, sh2_ref, w_ref, lng_ref, lnb_ref, wr_ref, br_ref,
                    x1_ref, hp_ref, idx_ref, gate_ref, cnt_ref, *, n_experts, n_split):
    first = jnp.logical_and(pl.program_id(0) == 0, pl.program_id(1) == 0)

    @pl.when(first)
    def _():
        cnt_ref[...] = jnp.zeros_like(cnt_ref)

    tm = x_ref.shape[0]
    th = tm // n_split
    for h in range(n_split):
        rows = slice(h * th, (h + 1) * th)
        ymix = jnp.concatenate([yp_ref[rows, :], ys_ref[rows, :]], axis=1)
        mix = jnp.dot(ymix, w_ref[...], preferred_element_type=F32)
        x1 = _layer_norm(DN_ALPHA * x_ref[rows, :] + g1_ref[...] * mix, lng_ref[...], lnb_ref[...])
        x1_ref[rows, :] = x1
        h2 = x1 * (1.0 + sc2_ref[...]) + sh2_ref[...]
        hp_ref[rows, :] = h2
        logits = jnp.dot(h2.astype(BF16), wr_ref[...], preferred_element_type=F32) + br_ref[...]
        lane = lax.broadcasted_iota(I32, (th, LANES), 1).astype(F32)
        cur = jnp.where(lane < n_experts, logits, -jnp.inf)
        vals, onehot = [], jnp.zeros((th, LANES), F32)
        for k in range(TOP_K):
            m = jnp.max(cur, axis=1, keepdims=True)
            sel = jnp.min(jnp.where(cur == m, lane, float(LANES)), axis=1, keepdims=True)
            hit = lane == sel
            idx_ref[rows, k:k + 1] = sel.astype(I32)
            vals.append(m)
            onehot = onehot + jnp.where(hit, 1.0, 0.0)
            cur = jnp.where(hit, -jnp.inf, cur)
        es = [jnp.exp(v - vals[0]) for v in vals]
        den = es[0] + es[1] + es[2] + es[3]
        for k in range(TOP_K):
            gate_ref[rows, k:k + 1] = es[k] / den
        cnt_ref[0:1, :] += jnp.sum(onehot, axis=0, keepdims=True)


def _outproj(y_pool, y_ssm, x, g1, sc2, sh2, w_out_bf, ln_g, ln_b, wr_bf, br_pad, n_experts):
    b, l, d = x.shape
    dp = y_pool.shape[-1]
    tm = min(l, 512)
    n_split = 2 if tm % 512 == 0 else 1
    nl = l // tm
    t = b * l
    row = pl.BlockSpec((None, tm, d), lambda bi, i: (bi, i, 0))
    halfrow = pl.BlockSpec((None, tm, dp), lambda bi, i: (bi, i, 0))
    mod = pl.BlockSpec((None, 1, d), lambda bi, i: (bi, 0, 0))
    vec = pl.BlockSpec((1, d), lambda bi, i: (0, 0))
    tok4 = pl.BlockSpec((tm, TOP_K), lambda bi, i: (bi * nl + i, 0))
    kern = functools.partial(_outproj_kernel, n_experts=n_experts, n_split=n_split)
    return pl.pallas_call(
        kern,
        grid=(b, nl),
        in_specs=[halfrow, halfrow, row, mod, mod, mod,
                  pl.BlockSpec((d, d), lambda bi, i: (0, 0)), vec, vec,
                  pl.BlockSpec((d, LANES), lambda bi, i: (0, 0)),
                  pl.BlockSpec((1, LANES), lambda bi, i: (0, 0))],
        out_specs=[row,
                   pl.BlockSpec((tm, d), lambda bi, i: (bi * nl + i, 0)),
                   tok4, tok4,
                   pl.BlockSpec((SUBLANES, LANES), lambda bi, i: (0, 0))],
        out_shape=[jax.ShapeDtypeStruct((b, l, d), F32),
                   jax.ShapeDtypeStruct((t, d), F32),
                   jax.ShapeDtypeStruct((t, TOP_K), I32),
                   jax.ShapeDtypeStruct((t, TOP_K), F32),
                   jax.ShapeDtypeStruct((SUBLANES, LANES), F32)],
        compiler_params=_params(("arbitrary", "arbitrary"), 2 * (d * d * 2 + d * LANES * 2 + tm * d * 14)),
        name="out_proj_router",
    )(y_pool, y_ssm, x, g1, sc2, sh2, w_out_bf, ln_g, ln_b, wr_bf, br_pad)


def _plan_kernel(idx_ref, start_ref, pos_ref, carry):
    i = pl.program_id(0)

    @pl.when(i == 0)
    def _():
        carry[...] = jnp.zeros_like(carry)

    idx = idx_ref[...]
    tp = idx.shape[0]
    lane = lax.broadcasted_iota(I32, (tp, LANES), 1)
    hits = [lane == idx[:, k:k + 1] for k in range(TOP_K)]
    onehot = jnp.zeros((tp, LANES), F32)
    for h in hits:
        onehot = onehot + jnp.where(h, 1.0, 0.0)
    below = jnp.where(lax.broadcasted_iota(I32, (tp, tp), 0) > lax.broadcasted_iota(I32, (tp, tp), 1),
                      1.0, 0.0).astype(BF16)
    excl = jnp.dot(below, onehot.astype(BF16), preferred_element_type=F32)
    slot = start_ref[0:1, :] + carry[0:1, :] + excl
    for k in range(TOP_K):
        pos_ref[:, k:k + 1] = jnp.sum(jnp.where(hits[k], slot, 0.0), axis=1, keepdims=True).astype(I32)
    carry[0:1, :] += jnp.sum(onehot, axis=0, keepdims=True)


def _plan(idx_all, start_pad):
    t = idx_all.shape[0]
    tp = next(c for c in (512, 384, 256, 128, 64, 32, 16, 8) if t % c == 0)
    return pl.pallas_call(
        _plan_kernel,
        grid=(t // tp,),
        in_specs=[pl.BlockSpec((tp, TOP_K), lambda i: (i, 0)),
                  pl.BlockSpec((SUBLANES, LANES), lambda i: (0, 0))],
        out_specs=pl.BlockSpec((tp, TOP_K), lambda i: (i, 0)),
        out_shape=jax.ShapeDtypeStruct((t, TOP_K), I32),
        scratch_shapes=[pltpu.VMEM((SUBLANES, LANES), F32)],
        compiler_params=_params(("arbitrary",), 4 * tp * LANES * 4),
        name="moe_plan",
    )(idx_all, start_pad)


def _dispatch_kernel(fill_ref, posp_ref, poss_ref, hp_ref, hs_ref, xs_ref, zbuf, sem, zsem, *, n_p_steps, n_blk):
    i = pl.program_id(0)

    def fill_copy(b):
        return pltpu.make_async_copy(zbuf, xs_ref.at[pl.ds(pl.multiple_of(b * ROW_BLK, ROW_BLK), ROW_BLK)], zsem)

    @pl.when(i == 0)
    def _():
        zbuf[...] = jnp.zeros_like(zbuf)

        def start(b, carry):
            @pl.when(fill_ref[b] != 0)
            def _():
                fill_copy(b).start()
            return carry

        def wait(b, carry):
            @pl.when(fill_ref[b] != 0)
            def _():
                fill_copy(b).wait()
            return carry

        lax.fori_loop(0, n_blk, start, 0)
        lax.fori_loop(0, n_blk, wait, 0)

    def scatter(src_ref, pos_ref):
        n_grp = src_ref.shape[0]

        def body(g, carry):
            for tt in range(SUBLANES):
                for k in range(TOP_K):
                    p = pos_ref[0, 0, (g * SUBLANES + tt) * TOP_K + k]
                    pltpu.make_async_copy(src_ref.at[g, pl.ds(tt, 1)], xs_ref.at[pl.ds(p, 1)],
                                          sem).start(priority=k % 2)
            return carry

        lax.fori_loop(0, n_grp, body, 0)
        rows = xs_ref.at[pl.ds(0, n_grp * SUBLANES)]
        for k in range(TOP_K):
            pltpu.make_async_copy(rows, rows, sem).wait()

    @pl.when(i < n_p_steps)
    def _():
        scatter(hp_ref, posp_ref)

    @pl.when(i >= n_p_steps)
    def _():
        scatter(hs_ref, poss_ref)


def _dispatch(pos_p, pos_s, h_prompt, h_sample, fill_flags, n_rows):
    tp, w = h_prompt.shape
    ts = h_sample.shape[0]
    tdp, tds = math.gcd(tp, 512), math.gcd(ts, 512)
    n_p_steps, n_s_steps = tp // tdp, ts // tds
    n_blk = fill_flags.shape[0]
    grid_spec = pltpu.PrefetchScalarGridSpec(
        num_scalar_prefetch=1,
        grid=(n_p_steps + n_s_steps,),
        in_specs=[pl.BlockSpec((1, 1, tdp * TOP_K), lambda i, f: (jnp.minimum(i, n_p_steps - 1), 0, 0),
                               memory_space=pltpu.SMEM),
                  pl.BlockSpec((1, 1, tds * TOP_K), lambda i, f: (jnp.maximum(i - n_p_steps, 0), 0, 0),
                               memory_space=pltpu.SMEM),
                  pl.BlockSpec((tdp // SUBLANES, SUBLANES, w), lambda i, f: (jnp.minimum(i, n_p_steps - 1), 0, 0)),
                  pl.BlockSpec((tds // SUBLANES, SUBLANES, w), lambda i, f: (jnp.maximum(i - n_p_steps, 0), 0, 0))],
        out_specs=pl.BlockSpec(memory_space=pl.ANY),
        scratch_shapes=[pltpu.VMEM((ROW_BLK, w), F32),
                        pltpu.SemaphoreType.DMA(()), pltpu.SemaphoreType.DMA(())],
    )
    need = 2 * (tdp + tds) * w * 4 + ROW_BLK * w * 4
    return pl.pallas_call(
        functools.partial(_dispatch_kernel, n_p_steps=n_p_steps, n_blk=n_blk),
        grid_spec=grid_spec,
        out_shape=jax.ShapeDtypeStruct((n_rows, w), F32),
        compiler_params=_params(("arbitrary",), need),
        name="moe_dispatch",
    )(fill_flags, pos_p.reshape(n_p_steps, 1, tdp * TOP_K), pos_s.reshape(n_s_steps, 1, tds * TOP_K),
      h_prompt.reshape(tp // SUBLANES, SUBLANES, w), h_sample.reshape(ts // SUBLANES, SUBLANES, w))


def _moe_kernel(ie_ref, ir_ref, inb_ref, meta_ref, xs_ref, wg_ref, wu_ref, bgu_ref, wd_ref, bd_ref,
                ys_ref, xbuf, act, ystage, zbuf, yflag, xsem, ysem, zsem, *, nc, n_blk):
    i = pl.program_id(0)
    c = pl.program_id(1)
    n_items = meta_ref[0]
    used_blk = meta_ref[1]
    valid = i < n_items
    nb = inb_ref[i]
    row0 = pl.multiple_of(ir_ref[i], ROW_BLK)
    n_quad = lax.shift_right_logical(nb, 3)
    has_pair = (nb & 4) != 0
    has_rb = (nb & 2) != 0
    has_blk = (nb & 1) != 0
    base_pair = pl.multiple_of(n_quad * 4 * MOE_RB, ROW_BLK)
    base_rb = pl.multiple_of(base_pair + jnp.where(has_pair, 2 * MOE_RB, 0), ROW_BLK)
    base_blk = pl.multiple_of(base_rb + jnp.where(has_rb, MOE_RB, 0), ROW_BLK)
    first = jnp.logical_and(i == 0, c == 0)
    last = jnp.logical_and(i == pl.num_programs(0) - 1, c == pl.num_programs(1) - 1)

    def x_copy(it, j):
        src0 = pl.multiple_of(ir_ref[it] + j * ROW_BLK, ROW_BLK)
        return pltpu.make_async_copy(xs_ref.at[pl.ds(src0, ROW_BLK)],
                                     xbuf.at[pl.ds(pl.multiple_of(j * ROW_BLK, ROW_BLK), ROW_BLK)], xsem)

    def x_start(it):
        def start(j, carry):
            x_copy(it, j).start()
            return carry
        lax.fori_loop(0, inb_ref[it], start, 0)

    def x_wait(it):
        def wait(j, carry):
            x_copy(it, j).wait()
            return carry
        lax.fori_loop(0, inb_ref[it], wait, 0)

    def tail_copy(b, col):
        return pltpu.make_async_copy(
            zbuf, ys_ref.at[pl.ds(pl.multiple_of(b * ROW_BLK, ROW_BLK), ROW_BLK), pl.ds(col * MOE_CW, MOE_CW)], zsem)

    n_col = ys_ref.shape[1] // MOE_CW

    @pl.when(first)
    def _():
        x_start(0)
        zbuf[...] = jnp.zeros_like(zbuf)
        for slot in range(YS_SLOTS):
            yflag[slot] = 0

        def start(b, carry):
            for col in range(n_col):
                tail_copy(b, col).start()
            return carry
        lax.fori_loop(used_blk, n_blk, start, 0)

    @pl.when(jnp.logical_and(valid, c == 0))
    def _():
        x_wait(i)

    @pl.when(jnp.logical_and(c == nc, i + 1 < n_items))
    def _():
        x_start(i + 1)

    @pl.when(jnp.logical_and(valid, c < nc))
    def _():
        def piece(r0, rows):
            w, bias = [], []
            bg, bu = bgu_ref[pl.ds(c, 1), :], bgu_ref[pl.ds(nc + c, 1), :]
            for s in range(MOE_CW // LANES):
                cols = slice(s * LANES, (s + 1) * LANES)
                w += [wg_ref[:, cols].astype(BF16), wu_ref[:, cols].astype(BF16)]
                bias += [bg[:, cols], bu[:, cols]]
            x = xbuf[pl.ds(r0, rows), :].astype(BF16)
            gu = (jnp.dot(x, jnp.concatenate(w, axis=1), preferred_element_type=F32)
                  + jnp.concatenate(bias, axis=1))
            for s in range(MOE_CW // LANES):
                gate = jnp.minimum(gu[:, 2 * s * LANES:(2 * s + 1) * LANES], SWIGLU_LIMIT)
                up = jnp.clip(gu[:, (2 * s + 1) * LANES:(2 * s + 2) * LANES], -SWIGLU_LIMIT, SWIGLU_LIMIT)
                a = (up + 1.0) * (gate * jax.nn.sigmoid(SWIGLU_ALPHA * gate))
                act[c, pl.ds(r0, rows), s * LANES:(s + 1) * LANES] = a.astype(BF16)

        def quad(q, carry):
            r0 = pl.multiple_of(q * 4 * MOE_RB, ROW_BLK)
            for j in range(4):
                piece(pl.multiple_of(r0 + j * MOE_RB, ROW_BLK), MOE_RB)
            return carry

        lax.fori_loop(0, n_quad, quad, 0)

        @pl.when(has_pair)
        def _():
            for j in range(2):
                piece(pl.multiple_of(base_pair + j * MOE_RB, ROW_BLK), MOE_RB)

        @pl.when(has_rb)
        def _():
            piece(base_rb, MOE_RB)

        @pl.when(has_blk)
        def _():
            piece(base_blk, ROW_BLK)

    slot_rows = [MOE_RB] * (YS_SLOTS - 1) + [ROW_BLK]

    def y_copy(slot, r0, col0):
        dst_row = pl.multiple_of(row0 + r0, ROW_BLK)
        return pltpu.make_async_copy(ystage.at[slot, pl.ds(0, slot_rows[slot])],
                                     ys_ref.at[pl.ds(dst_row, slot_rows[slot]), pl.ds(col0, MOE_OW)],
                                     ysem.at[slot])

    def y_drain_all():
        for slot in range(YS_SLOTS):
            @pl.when(yflag[slot] != 0)
            def _(slot=slot):
                y_copy(slot, 0, 0).wait()
                yflag[slot] = 0

    @pl.when(jnp.logical_and(valid, c >= nc))
    def _():
        col0 = pl.multiple_of((c - nc) * MOE_OW, MOE_OW)
        y_drain_all()

        def compute(r0, rows):
            a = jnp.concatenate([act[cc, pl.ds(r0, rows), :] for cc in range(nc)], axis=1)
            return jnp.dot(a, wd_ref[...].astype(BF16), preferred_element_type=F32) + bd_ref[pl.ds(c - nc, 1), :]

        def group(slots, base):
            for j, slot in enumerate(slots):
                ystage[slot, 0:slot_rows[slot], :] = compute(pl.multiple_of(base + j * MOE_RB, ROW_BLK),
                                                             slot_rows[slot])
            for j, slot in enumerate(slots):
                y_copy(slot, base + j * MOE_RB, col0).start()

        def quad(q, carry):
            @pl.when(q > 0)
            def _():
                for slot in (0, 1, 2, 3):
                    y_copy(slot, 0, 0).wait()

            group((0, 1, 2, 3), pl.multiple_of(q * 4 * MOE_RB, ROW_BLK))
            return carry

        lax.fori_loop(0, n_quad, quad, 0)

        @pl.when(n_quad > 0)
        def _():
            for slot in (0, 1, 2, 3):
                yflag[slot] = 1

        @pl.when(has_pair)
        def _():
            group((4, 5), base_pair)
            yflag[4] = 1
            yflag[5] = 1

        @pl.when(has_rb)
        def _():
            group((6,), base_rb)
            yflag[6] = 1

        @pl.when(has_blk)
        def _():
            group((7,), base_blk)
            yflag[7] = 1

    @pl.when(last)
    def _():
        def wait(b, carry):
            for col in range(n_col):
                tail_copy(b, col).wait()
            return carry
        lax.fori_loop(used_blk, n_blk, wait, 0)
        y_drain_all()


def _moe(xs, item_e, item_row0, item_nblk, meta, w_gate_up, b_gate_up, w_down, b_down):
    n_rows, d = xs.shape
    n_exp, _, ff2 = w_gate_up.shape
    ff = ff2 // 2
    assert w_gate_up.shape[1] == d and w_down.shape[1:] == (ff, d) and ff == d
    nc = ff // MOE_CW
    nc2 = d // MOE_OW
    ni = item_e.shape[0]
    rbuf = MOE_MAX_BLK * ROW_BLK

    def item(i, n):
        return jnp.minimum(i, n[0] - 1)

    def c1(i, c, n):
        return jnp.where(i < n[0], jnp.minimum(c, nc - 1), nc - 1)

    def c2(i, c, n):
        return jnp.where(i < n[0], jnp.maximum(c - nc, 0), nc2 - 1)

    bgu3 = b_gate_up.reshape(n_exp, 2 * nc, MOE_CW)
    bd3 = b_down.reshape(n_exp, nc2, MOE_OW)
    in_specs = [
        pl.BlockSpec(memory_space=pl.ANY),
        pl.BlockSpec((None, d, MOE_CW), lambda i, c, e, r, b, n: (e[item(i, n)], 0, c1(i, c, n))),
        pl.BlockSpec((None, d, MOE_CW), lambda i, c, e, r, b, n: (e[item(i, n)], 0, nc + c1(i, c, n))),
        pl.BlockSpec((None, 2 * nc, MOE_CW), lambda i, c, e, r, b, n: (e[item(i, n)], 0, 0)),
        pl.BlockSpec((None, ff, MOE_OW), lambda i, c, e, r, b, n: (e[item(i, n)], 0, c2(i, c, n))),
        pl.BlockSpec((None, nc2, MOE_OW), lambda i, c, e, r, b, n: (e[item(i, n)], 0, 0)),
    ]
    grid_spec = pltpu.PrefetchScalarGridSpec(
        num_scalar_prefetch=4,
        grid=(ni, nc + nc2),
        in_specs=in_specs,
        out_specs=pl.BlockSpec(memory_space=pl.ANY),
        scratch_shapes=[pltpu.VMEM((rbuf, d), F32),
                        pltpu.VMEM((nc, rbuf, MOE_CW), BF16),
                        pltpu.VMEM((YS_SLOTS, MOE_RB, MOE_OW), F32),
                        pltpu.VMEM((ROW_BLK, MOE_CW), F32),
                        pltpu.SMEM((YS_SLOTS,), I32),
                        pltpu.SemaphoreType.DMA(()),
                        pltpu.SemaphoreType.DMA((YS_SLOTS,)),
                        pltpu.SemaphoreType.DMA(())],
    )
    need = (rbuf * d * 4 + rbuf * ff * 2 + YS_SLOTS * MOE_RB * MOE_OW * 4 + ROW_BLK * MOE_CW * 4
            + 2 * 2 * d * MOE_CW * 4 + 2 * ff * MOE_OW * 4)
    return pl.pallas_call(
        functools.partial(_moe_kernel, nc=nc, n_blk=n_rows // ROW_BLK),
        grid_spec=grid_spec,
        out_shape=jax.ShapeDtypeStruct((n_rows, d), F32),
        compiler_params=_params(("arbitrary", "arbitrary"), need),
        name="moe_experts",
    )(item_e, item_row0, item_nblk, meta, xs, w_gate_up, w_gate_up, bgu3, w_down, bd3)


def _combine_kernel(pos_ref, posn_ref, ys_ref, gate_ref, x1_ref, g2_ref, lng_ref, lnb_ref, o_ref, gbuf, sem, *, tc):
    i = pl.program_id(0)
    slot = i % 2

    def issue(p_ref, s):
        def body(g, carry):
            for tt in range(SUBLANES):
                for k in range(TOP_K):
                    p = p_ref[0, 0, (g * SUBLANES + tt) * TOP_K + k]
                    pltpu.make_async_copy(ys_ref.at[pl.ds(p, 1)], gbuf.at[s, k, g, pl.ds(tt, 1)],
                                          sem.at[s]).start(priority=k % 2)
            return carry
        lax.fori_loop(0, tc // SUBLANES, body, 0)

    @pl.when(i == 0)
    def _():
        issue(pos_ref, 0)

    @pl.when(i + 1 < pl.num_programs(0))
    def _():
        issue(posn_ref, 1 - slot)

    rows = ys_ref.at[pl.ds(0, tc)]
    for k in range(TOP_K):
        pltpu.make_async_copy(rows, rows, sem.at[slot]).wait()
    gt = gate_ref[...]
    d = x1_ref.shape[-1]
    ffn = gt[:, 0:1] * gbuf[slot, 0].reshape(tc, d)
    for k in range(1, TOP_K):
        ffn = ffn + gt[:, k:k + 1] * gbuf[slot, k].reshape(tc, d)
    o_ref[...] = _layer_norm(DN_ALPHA * x1_ref[...] + g2_ref[...] * ffn, lng_ref[...], lnb_ref[...])


def _combine(pos, ys, gates, x1, g2, ln_g, ln_b):
    b, l, d = x1.shape
    t = b * l
    tc = min(l, 128)
    per_b = l // tc
    n_steps = t // tc
    pos3 = pos.reshape(n_steps, 1, tc * TOP_K)
    vec = pl.BlockSpec((1, d), lambda i: (0, 0))
    out = pl.pallas_call(
        functools.partial(_combine_kernel, tc=tc),
        grid=(n_steps,),
        in_specs=[pl.BlockSpec((1, 1, tc * TOP_K), lambda i: (i, 0, 0), memory_space=pltpu.SMEM),
                  pl.BlockSpec((1, 1, tc * TOP_K), lambda i: (jnp.minimum(i + 1, n_steps - 1), 0, 0),
                               memory_space=pltpu.SMEM),
                  pl.BlockSpec(memory_space=pl.ANY),
                  pl.BlockSpec((tc, TOP_K), lambda i: (i, 0)),
                  pl.BlockSpec((tc, d), lambda i: (i, 0)),
                  pl.BlockSpec((None, 1, d), lambda i: (i // per_b, 0, 0)),
                  vec, vec],
        out_specs=pl.BlockSpec((tc, d), lambda i: (i, 0)),
        out_shape=jax.ShapeDtypeStruct((t, d), F32),
        scratch_shapes=[pltpu.VMEM((2, TOP_K, tc // SUBLANES, SUBLANES, d), F32), pltpu.SemaphoreType.DMA((2,))],
        compiler_params=_params(("arbitrary",), (2 * TOP_K + 2 * 2) * tc * d * 4),
        name="moe_combine",
    )(pos3, pos3, ys, gates, x1.reshape(t, d), g2, ln_g, ln_b)
    return out.reshape(b, l, d)


def _moe_items(counts, n_blk_total):
    n_exp = counts.shape[0]
    nblk = (counts + ROW_BLK - 1) // ROW_BLK
    blk_end = jnp.cumsum(nblk)
    blk0 = blk_end - nblk
    n_it = (nblk + MOE_MAX_BLK - 1) // MOE_MAX_BLK
    it_end = jnp.cumsum(n_it)
    it0 = it_end - n_it
    ni = n_exp + n_blk_total // MOE_MAX_BLK
    j = jnp.arange(ni, dtype=I32)
    e = jnp.minimum(jnp.sum(j[:, None] >= it_end[None, :], axis=1), n_exp - 1).astype(I32)
    local = j - it0[e]
    item_nblk = jnp.clip(nblk[e] - local * MOE_MAX_BLK, 0, MOE_MAX_BLK).astype(I32)
    item_row0 = ((blk0[e] + local * MOE_MAX_BLK) * ROW_BLK).astype(I32)
    n_items = it_end[-1].astype(I32)
    item_nblk = jnp.where(j < n_items, item_nblk, 0)
    item_row0 = jnp.where(j < n_items, item_row0, 0)
    row_start = (blk0 * ROW_BLK).astype(F32)
    used_blk = blk_end[-1].astype(I32)
    meta = jnp.stack([n_items, used_blk])
    blk = jnp.arange(n_blk_total, dtype=I32)
    partial = jnp.any((blk[:, None] == (blk_end - 1)[None, :]) & ((counts % ROW_BLK) != 0)[None, :], axis=1)
    fill = jnp.logical_or(partial, blk >= used_blk).astype(I32)
    return e, item_row0, item_nblk, meta, row_start, fill


def _mixer_stream(x, mod, hist, h0, pos0, w):
    b, l, d = x.shape
    sh1, sc1, g1, sh2, sc2, g2 = [m.reshape(b, 1, d) for m in jnp.split(mod, 6, axis=-1)]
    u_pool, u_ssm = _inproj(x, sc1, sh1, w["w_in"])
    hist16 = jnp.concatenate([jnp.zeros((b, HIST_ROWS - hist.shape[1], hist.shape[2]), F32), hist.astype(F32)], axis=1)
    y_pool, new_hist = _pool(u_pool, hist16, w["w_pool"], w["pool_scale"], pos0)
    y_ssm, h_new = _ssm(u_ssm, w["ssm"], h0)
    x1, hp, idx, gates, cnt = _outproj(y_pool, y_ssm, x, g1, sc2, sh2, w["w_out"], w["ln1_g"], w["ln1_b"],
                                       w["w_router"], w["b_router"], w["n_experts"])
    s_re, s_im = _slabs_to_state(h_new, SSM_STATE)
    return dict(x1=x1, hp=hp, idx=idx, gates=gates, cnt=cnt, g2=g2,
                hist=new_hist[:, 1:, :], s_re=s_re, s_im=s_im)


def kernel(x_prompt, x_sample, cache_pool, state_ssm_re, state_ssm_im, c_prompt, c_sample, w_ada, b_ada, w_in, w_pool, pool_scale, lambda_re, lambda_im, log_dt, ssm_b_re, ssm_b_im, ssm_c_re, ssm_c_im, d_skip, w_glu, b_glu, w_out, ln1_g, ln1_b, w_router, b_router, w_gate_up, b_gate_up, w_down, b_down, ln2_g, ln2_b):
    assert w_ada.shape[0] == DEPTH
    bp, lp, d = x_prompt.shape
    bs, ls, _ = x_sample.shape
    assert bp == SUBLANES and bs == SUBLANES, "the S5 kernel puts the 8 streams on sublanes"
    n_exp = w_router.shape[-1]
    dp = w_pool.shape[1] * w_pool.shape[2]

    w = dict(
        w_in=w_in[0].astype(BF16),
        w_pool=w_pool[0].astype(BF16),
        pool_scale=pool_scale[0].astype(F32).reshape(1, dp),
        ssm=_ssm_params(lambda_re[0], lambda_im[0], log_dt[0], ssm_b_re[0], ssm_b_im[0], ssm_c_re[0], ssm_c_im[0],
                        d_skip[0], w_glu[0], b_glu[0], SUBLANES),
        w_out=w_out[0].astype(BF16),
        ln1_g=ln1_g[0].reshape(1, d), ln1_b=ln1_b[0].reshape(1, d),
        w_router=jnp.pad(w_router[0], ((0, 0), (0, LANES - n_exp))).astype(BF16),
        b_router=jnp.pad(b_router[0].astype(F32), (0, LANES - n_exp)).reshape(1, LANES),
        n_experts=n_exp,
    )

    mod = _ada(jnp.concatenate([c_prompt, c_sample], axis=0), w_ada[0], b_ada[0].reshape(1, -1))
    n_slab = lambda_re.shape[1]
    p = _mixer_stream(x_prompt, mod[:bp], jnp.zeros((bp, HIST_ROWS - 1, dp), F32),
                      jnp.zeros((n_slab, bp, LANES), F32), 0, w)
    s = _mixer_stream(x_sample, mod[bp:], cache_pool[0], _state_to_slabs(state_ssm_re[0], state_ssm_im[0]),
                      PAST_LEN, w)

    tp, ts = bp * lp, bs * ls
    n_assign = (tp + ts) * TOP_K
    n_blk_total = -(-n_assign // ROW_BLK) + n_exp
    n_rows = n_blk_total * ROW_BLK
    counts = (p["cnt"][0, :n_exp] + s["cnt"][0, :n_exp]).astype(I32)
    item_e, item_row0, item_nblk, meta, row_start, fill = _moe_items(counts, n_blk_total)
    start_pad = jnp.zeros((SUBLANES, LANES), F32).at[0, :n_exp].set(row_start)
    pos = _plan(jnp.concatenate([p["idx"], s["idx"]], axis=0), start_pad)
    pos_p, pos_s = pos[:tp], pos[tp:]
    xs = _dispatch(pos_p, pos_s, p["hp"], s["hp"], fill, n_rows)
    ys = _moe(xs, item_e, item_row0, item_nblk, meta, w_gate_up[0], b_gate_up[0], w_down[0], b_down[0])
    g2l, b2l = ln2_g[0].reshape(1, d), ln2_b[0].reshape(1, d)
    y_p = _combine(pos_p, ys, p["gates"], p["x1"], p["g2"], g2l, b2l)
    y_s = _combine(pos_s, ys, s["gates"], s["x1"], s["g2"], g2l, b2l)

    return (y_p, y_s, p["hist"][None], p["s_re"][None], p["s_im"][None],
            s["hist"][None], s["s_re"][None], s["s_im"][None])
```

```python
import functools
import math

import jax
import jax.numpy as jnp
from jax import lax
from jax.experimental import pallas as pl
from jax.experimental.pallas import tpu as pltpu

F32 = jnp.float32
BF16 = jnp.bfloat16
I32 = jnp.int32

POOL_WINDOWS = (2, 4, 8, 16)
HIST_ROWS = 16
SSM_STATE = 64
TOP_K = 4
SWIGLU_LIMIT = 7.0
SWIGLU_ALPHA = 1.702
LN_EPS = 1e-5
DEPTH = 1
DN_ALPHA = (2 * DEPTH) ** 0.25
PAST_LEN = 2048

LANES = 128
SUBLANES = 8
MXU_DIM = 256
V7X_VMEM_BYTES = 64 * 1024 * 1024
MIB = 1024 * 1024

ROW_BLK = 256
MOE_RB = 512
MOE_CW = 256
MOE_OW = 512
MOE_MAX_BLK = 10
YS_SLOTS = 8


def _params(sem, need_bytes):
    limit = min(need_bytes + 16 * MIB, V7X_VMEM_BYTES - 4 * MIB)
    return pltpu.CompilerParams(dimension_semantics=sem, vmem_limit_bytes=limit)


def _layer_norm(v, g, b):
    mu = jnp.mean(v, axis=-1, keepdims=True)
    vc = v - mu
    var = jnp.mean(vc * vc, axis=-1, keepdims=True)
    return vc * lax.rsqrt(var + LN_EPS) * g + b


def _ada_kernel(c_ref, w_ref, b_ref, o_ref):
    c = c_ref[...]
    s = c * jax.nn.sigmoid(c)
    o_ref[...] = jnp.dot(s.astype(BF16), w_ref[...].astype(BF16), preferred_element_type=F32) + b_ref[...]


def _ada(c_all, w_ada, b_ada):
    r, d = c_all.shape
    n = w_ada.shape[1]
    tn = 1536
    return pl.pallas_call(
        _ada_kernel,
        grid=(n // tn,),
        in_specs=[pl.BlockSpec((r, d), lambda j: (0, 0)),
                  pl.BlockSpec((d, tn), lambda j: (0, j)),
                  pl.BlockSpec((1, tn), lambda j: (0, j))],
        out_specs=pl.BlockSpec((r, tn), lambda j: (0, j)),
        out_shape=jax.ShapeDtypeStruct((r, n), F32),
        compiler_params=_params(("arbitrary",), 2 * (d * tn * 4 + r * (d + tn) * 4)),
        name="ada_mod",
    )(c_all, w_ada, b_ada)


def _inproj_kernel(x_ref, sc_ref, sh_ref, w_ref, up_ref, us_ref):
    h = x_ref[...] * (1.0 + sc_ref[...]) + sh_ref[...]
    u = jnp.dot(h.astype(BF16), w_ref[...], preferred_element_type=F32)
    dp = up_ref.shape[-1]
    up_ref[...] = u[:, :dp]
    us_ref[...] = u[:, dp:]


def _inproj(x, sc, sh, w_bf):
    b, l, d = x.shape
    dm = w_bf.shape[1]
    dp = dm // 2
    tm = min(l, 512)
    row = pl.BlockSpec((None, tm, d), lambda bi, i: (bi, i, 0))
    mod = pl.BlockSpec((None, 1, d), lambda bi, i: (bi, 0, 0))
    half = pl.BlockSpec((None, tm, dp), lambda bi, i: (bi, i, 0))
    return pl.pallas_call(
        _inproj_kernel,
        grid=(b, l // tm),
        in_specs=[row, mod, mod, pl.BlockSpec((d, dm), lambda bi, i: (0, 0))],
        out_specs=[half, half],
        out_shape=[jax.ShapeDtypeStruct((b, l, dp), F32), jax.ShapeDtypeStruct((b, l, dm - dp), F32)],
        compiler_params=_params(("arbitrary", "arbitrary"), 2 * (d * dm * 2 + tm * (d + dm) * 4)),
        name="in_proj",
    )(x, sc, sh, w_bf)


def _pool_kernel(u_ref, hist_ref, w_ref, scale_ref, y_ref, nh_ref, hbuf, *, tt, pos0):
    i = pl.program_id(1)

    @pl.when(i == 0)
    def _():
        hbuf[...] = hist_ref[...]

    u = u_ref[...]
    hb = hbuf[...]
    u16 = u.astype(BF16)
    h16 = hb.astype(BF16)
    pc = u.shape[1] // len(POOL_WINDOWS)
    d_main = lax.broadcasted_iota(I32, (tt, tt), 0) - lax.broadcasted_iota(I32, (tt, tt), 1)
    d_hist = (lax.broadcasted_iota(I32, (tt, HIST_ROWS), 0) + HIST_ROWS
              - lax.broadcasted_iota(I32, (tt, HIST_ROWS), 1))
    pos = (pos0 + i * tt + lax.broadcasted_iota(I32, (tt, 1), 0)).astype(F32)
    for g, w in enumerate(POOL_WINDOWS):
        sl = slice(g * pc, (g + 1) * pc)
        bm = jnp.where(d_main >= 0, jnp.where(d_main < w, 1.0, 0.0), 0.0).astype(BF16)
        bh = jnp.where(d_hist < w, 1.0, 0.0).astype(BF16)
        s = (jnp.dot(bm, u16[:, sl], preferred_element_type=F32)
             + jnp.dot(bh, h16[:, sl], preferred_element_type=F32))
        cnt = jnp.minimum(pos + 1.0, float(w))
        diff = s / cnt - u[:, sl]
        y = jnp.dot(diff.astype(BF16), w_ref[g], preferred_element_type=F32) * scale_ref[:, sl]
        y_ref[:, sl] = y.astype(y_ref.dtype)
    hbuf[...] = u[tt - HIST_ROWS:, :]

    @pl.when(i == pl.num_programs(1) - 1)
    def _():
        nh_ref[...] = hbuf[...]


def _pool(u_pool, hist16, w_pool_bf, pool_scale, pos0):
    b, l, dp = u_pool.shape
    tt = min(l, 256)
    g, pc, _ = w_pool_bf.shape
    kern = functools.partial(_pool_kernel, tt=tt, pos0=pos0)
    return pl.pallas_call(
        kern,
        grid=(b, l // tt),
        in_specs=[pl.BlockSpec((None, tt, dp), lambda bi, i: (bi, i, 0)),
                  pl.BlockSpec((None, HIST_ROWS, dp), lambda bi, i: (bi, 0, 0)),
                  pl.BlockSpec((g, pc, pc), lambda bi, i: (0, 0, 0)),
                  pl.BlockSpec((1, dp), lambda bi, i: (0, 0))],
        out_specs=[pl.BlockSpec((None, tt, dp), lambda bi, i: (bi, i, 0)),
                   pl.BlockSpec((None, HIST_ROWS, dp), lambda bi, i: (bi, 0, 0))],
        out_shape=[jax.ShapeDtypeStruct((b, l, dp), BF16), jax.ShapeDtypeStruct((b, HIST_ROWS, dp), F32)],
        scratch_shapes=[pltpu.VMEM((HIST_ROWS, dp), F32)],
        compiler_params=_params(("arbitrary", "arbitrary"), 2 * (tt * dp * 6 + g * pc * pc * 2 + 2 * HIST_ROWS * dp * 4)),
        name="pool_mixer",
    )(u_pool, hist16, w_pool_bf, pool_scale)


SSM_PAIR_GROUP = 8
SSM_TIME_UNROLL = 4


def _ssm_kernel(u_ref, bq_ref, cq_ref, wg_ref, ar_ref, ai_ref, dsk_ref, bgl_ref, h0_ref,
                y_ref, hn_ref, s_ref, ubuf, hst, bc_ref, cc_ref, *, tt, pitch, nb):
    i = pl.program_id(0)
    n_slab = s_ref.shape[0]
    n_chunk, ppc, rb, cb = bq_ref.shape
    slab_per_chunk = n_slab // n_chunk
    cw = bc_ref.shape[1]

    @pl.when(i == 0)
    def _():
        s_ref[...] = jnp.zeros_like(s_ref)
        ubuf[...] = jnp.zeros_like(ubuf)
        hst[...] = h0_ref[...]
        bc_ref[...] = jnp.zeros_like(bc_ref)
        lane = lax.broadcasted_iota(I32, (rb, cw), 1) - lax.broadcasted_iota(I32, (rb, cw), 0)
        for k in range(n_chunk):
            for q in range(ppc):
                bc_ref[k, q * rb:(q + 1) * rb, q * cb:(q + 1) * cb] = bq_ref[k, q]
                place = jnp.where(lane == q * rb, 1.0, 0.0).astype(BF16)
                cc_ref[k, q * cb:(q + 1) * cb, :] = jnp.dot(cq_ref[k, q], place,
                                                             preferred_element_type=F32).astype(BF16)

    u = u_ref[...].reshape(nb * tt, u_ref.shape[-1])
    u16 = u.astype(BF16)
    for k in range(n_chunk):
        bu = jnp.dot(u16[:, k * cw:(k + 1) * cw], bc_ref[k], preferred_element_type=F32)
        for j in range(slab_per_chunk):
            for b in range(nb):
                s_ref[k * slab_per_chunk + j, b * pitch:b * pitch + tt, :] = (
                    bu[b * tt:(b + 1) * tt, j * LANES:(j + 1) * LANES])
    for b in range(nb):
        ubuf[b * pitch:b * pitch + tt, :] = u[b * tt:(b + 1) * tt, :]

    n_pair = n_slab // 2
    for pg in range(n_pair // SSM_PAIR_GROUP):
        qs = [pg * SSM_PAIR_GROUP + j for j in range(SSM_PAIR_GROUP)]
        ars = [ar_ref[q] for q in qs]
        ais = [ai_ref[q] for q in qs]

        def body(t2, carry, qs=qs, ars=ars, ais=ais):
            hr, hi = list(carry[0]), list(carry[1])
            for dt in range(SSM_TIME_UNROLL):
                rows = pl.ds(t2 * SSM_TIME_UNROLL + dt, nb, stride=pitch)
                for j, q in enumerate(qs):
                    bur = s_ref[2 * q, rows, :]
                    bui = s_ref[2 * q + 1, rows, :]
                    r_ = ars[j] * hr[j] - ais[j] * hi[j] + bur
                    i_ = ars[j] * hi[j] + ais[j] * hr[j] + bui
                    s_ref[2 * q, rows, :] = r_
                    s_ref[2 * q + 1, rows, :] = i_
                    hr[j], hi[j] = r_, i_
            return tuple(hr), tuple(hi)

        init = (tuple(hst[2 * q] for q in qs), tuple(hst[2 * q + 1] for q in qs))
        hr, hi = lax.fori_loop(0, tt // SSM_TIME_UNROLL, body, init)
        for j, q in enumerate(qs):
            hst[2 * q] = hr[j]
            hst[2 * q + 1] = hi[j]

    ycs = []
    for n in range(n_chunk):
        hc = jnp.concatenate([s_ref[n * slab_per_chunk + j] for j in range(slab_per_chunk)], axis=1)
        ycs.append(jnp.dot(hc.astype(BF16), cc_ref[n], preferred_element_type=F32))
    y = jnp.concatenate(ycs, axis=1) + dsk_ref[...] * ubuf[...]
    y = jax.nn.gelu(y)
    y16 = y.astype(BF16)
    z = jnp.concatenate([jnp.dot(y16[:, k * cw:(k + 1) * cw], wg_ref[k], preferred_element_type=F32)
                         for k in range(n_chunk)], axis=1) + bgl_ref[...]
    out = y * jax.nn.sigmoid(z)
    for b in range(nb):
        y_ref[b] = out[b * pitch:b * pitch + tt].astype(y_ref.dtype)

    @pl.when(i == pl.num_programs(0) - 1)
    def _():
        hn_ref[...] = hst[...]


def _ssm(u_ssm, prm, h0):
    b, l, ds = u_ssm.shape
    tt = min(l, 64)
    pitch = tt + 4
    n_slab = h0.shape[0]
    n_chunk, ppc, rb, cb = prm["bq"].shape
    kern = functools.partial(_ssm_kernel, tt=tt, pitch=pitch, nb=b)

    def const(a):
        nd = a.ndim
        return pl.BlockSpec(a.shape, lambda i, nd=nd: (0,) * nd)

    args = (u_ssm, prm["bq"], prm["cq"], prm["wg"], prm["ar"], prm["ai"], prm["dsk"], prm["bgl"], h0)
    dense = n_chunk * (ppc * rb) * (ppc * cb) * 2
    return pl.pallas_call(
        kern,
        grid=(l // tt,),
        in_specs=[pl.BlockSpec((b, tt, ds), lambda i: (0, i, 0))] + [const(a) for a in args[1:]],
        out_specs=[pl.BlockSpec((b, tt, ds), lambda i: (0, i, 0)),
                   pl.BlockSpec((n_slab, b, LANES), lambda i: (0, 0, 0))],
        out_shape=[jax.ShapeDtypeStruct((b, l, ds), BF16), jax.ShapeDtypeStruct((n_slab, b, LANES), F32)],
        scratch_shapes=[pltpu.VMEM((n_slab, b * pitch, LANES), F32),
                        pltpu.VMEM((b * pitch, ds), F32),
                        pltpu.VMEM((n_slab, b, LANES), F32),
                        pltpu.VMEM((n_chunk, ppc * rb, ppc * cb), BF16),
                        pltpu.VMEM((n_chunk, ppc * cb, ppc * rb), BF16)],
        compiler_params=_params(("arbitrary",), (n_slab * LANES + ds) * b * pitch * 4 + 2 * b * tt * ds * 6
                                + 2 * dense + 2 * sum(a.size * a.dtype.itemsize for a in args[1:])
                                + prm["cq"].size * 2 * (LANES // rb - 1) * 2),
        name="ssm_mixer",
    )(*args)


def _ssm_params(lambda_re, lambda_im, log_dt, b_re, b_im, c_re, c_im, d_skip, w_glu, b_glu, nb):
    g, p = lambda_re.shape
    ch = b_re.shape[-1]
    gpc = MXU_DIM // ch
    n_chunk = g // gpc
    ppc = gpc // 2
    lr, li = lambda_re.astype(F32), lambda_im.astype(F32)
    dt = jnp.exp(log_dt.astype(F32))[:, None]
    mag = jnp.exp(lr * dt)
    abar_re, abar_im = mag * jnp.cos(li * dt), mag * jnp.sin(li * dt)
    nr, ni = abar_re - 1.0, abar_im
    den = lr * lr + li * li
    k_re = (nr * lr + ni * li) / den
    k_im = (ni * lr - nr * li) / den
    br, bi = b_re.astype(F32), b_im.astype(F32)
    bb_re = k_re[..., None] * br - k_im[..., None] * bi
    bb_im = k_re[..., None] * bi + k_im[..., None] * br
    eye_2 = jnp.eye(2, dtype=F32)
    bb = jnp.stack([bb_re, bb_im]).reshape(2, n_chunk, ppc, 2, p, ch)
    bq = jnp.einsum("rkqgpc,gy->kqgcryp", bb, eye_2).astype(BF16).reshape(n_chunk, ppc, 2 * ch, 2 * 2 * p)
    cm = jnp.stack([c_re.astype(F32), -c_im.astype(F32)]).reshape(2, n_chunk, ppc, 2, ch, p)
    cq = jnp.einsum("rkqgcp,gy->kqrgpyc", cm, eye_2).astype(BF16).reshape(n_chunk, ppc, 2 * 2 * p, 2 * ch)
    wg = jnp.einsum("kgce,gx->kgcxe", w_glu.astype(F32).reshape(n_chunk, gpc, ch, ch),
                    jnp.eye(gpc, dtype=F32)).reshape(n_chunk, gpc * ch, gpc * ch)
    n_pair = g // 2
    ar = jnp.broadcast_to(abar_re.reshape(n_pair, 1, 2 * p), (n_pair, nb, 2 * p))
    ai = jnp.broadcast_to(abar_im.reshape(n_pair, 1, 2 * p), (n_pair, nb, 2 * p))
    return dict(bq=bq, cq=cq, wg=wg.astype(BF16), ar=ar, ai=ai,
                dsk=d_skip.astype(F32).reshape(1, g * ch), bgl=b_glu.astype(F32).reshape(1, g * ch))


def _state_to_slabs(s_re, s_im):
    b, g, p = s_re.shape
    re = s_re.astype(F32).reshape(b, g // 2, 2 * p).transpose(1, 0, 2)
    im = s_im.astype(F32).reshape(b, g // 2, 2 * p).transpose(1, 0, 2)
    return jnp.stack([re, im], axis=1).reshape(g, b, 2 * p)


def _slabs_to_state(h, p):
    n_slab, b, _ = h.shape
    h = h.reshape(n_slab // 2, 2, b, 2 * p)
    re = h[:, 0].transpose(1, 0, 2).reshape(b, n_slab, p)
    im = h[:, 1].transpose(1, 0, 2).reshape(b, n_slab, p)
    return re, im


def _outproj_kernel(yp_ref, ys_ref, x_ref, g1_ref, sc2_ref, sh2_ref, w_ref, lng_ref, lnb_ref, wr_ref, br_ref,
                    x1_ref, hp_ref, idx_ref, gate_ref, cnt_ref, *, n_experts, n_split):
    first = jnp.logical_and(pl.program_id(0) == 0, pl.program_id(1) == 0)

    @pl.when(first)
    def _():
        cnt_ref[...] = jnp.zeros_like(cnt_ref)

    tm = x_ref.shape[0]
    th = tm // n_split
    for h in range(n_split):
        rows = slice(h * th, (h + 1) * th)
        ymix = jnp.concatenate([yp_ref[rows, :], ys_ref[rows, :]], axis=1)
        mix = jnp.dot(ymix, w_ref[...], preferred_element_type=F32)
        x1 = _layer_norm(DN_ALPHA * x_ref[rows, :] + g1_ref[...] * mix, lng_ref[...], lnb_ref[...])
        x1_ref[rows, :] = x1
        h2 = x1 * (1.0 + sc2_ref[...]) + sh2_ref[...]
        hp_ref[rows, :] = h2
        logits = jnp.dot(h2.astype(BF16), wr_ref[...], preferred_element_type=F32) + br_ref[...]
        lane = lax.broadcasted_iota(I32, (th, LANES), 1).astype(F32)
        cur = jnp.where(lane < n_experts, logits, -jnp.inf)
        vals, onehot = [], jnp.zeros((th, LANES), F32)
        for k in range(TOP_K):
            m = jnp.max(cur, axis=1, keepdims=True)
            sel = jnp.min(jnp.where(cur == m, lane, float(LANES)), axis=1, keepdims=True)
            hit = lane == sel
            idx_ref[rows, k:k + 1] = sel.astype(I32)
            vals.append(m)
            onehot = onehot + jnp.where(hit, 1.0, 0.0)
            cur = jnp.where(hit, -jnp.inf, cur)
        es = [jnp.exp(v - vals[0]) for v in vals]
        den = es[0] + es[1] + es[2] + es[3]
        for k in range(TOP_K):
            gate_ref[rows, k:k + 1] = es[k] / den
        cnt_ref[0:1, :] += jnp.sum(onehot, axis=0, keepdims=True)


def _outproj(y_pool, y_ssm, x, g1, sc2, sh2, w_out_bf, ln_g, ln_b, wr_bf, br_pad, n_experts):
    b, l, d = x.shape
    dp = y_pool.shape[-1]
    tm = min(l, 512)
    n_split = 2 if tm % 512 == 0 else 1
    nl = l // tm
    t = b * l
    row = pl.BlockSpec((None, tm, d), lambda bi, i: (bi, i, 0))
    halfrow = pl.BlockSpec((None, tm, dp), lambda bi, i: (bi, i, 0))
    mod = pl.BlockSpec((None, 1, d), lambda bi, i: (bi, 0, 0))
    vec = pl.BlockSpec((1, d), lambda bi, i: (0, 0))
    tok4 = pl.BlockSpec((tm, TOP_K), lambda bi, i: (bi * nl + i, 0))
    kern = functools.partial(_outproj_kernel, n_experts=n_experts, n_split=n_split)
    return pl.pallas_call(
        kern,
        grid=(b, nl),
        in_specs=[halfrow, halfrow, row, mod, mod, mod,
                  pl.BlockSpec((d, d), lambda bi, i: (0, 0)), vec, vec,
                  pl.BlockSpec((d, LANES), lambda bi, i: (0, 0)),
                  pl.BlockSpec((1, LANES), lambda bi, i: (0, 0))],
        out_specs=[row,
                   pl.BlockSpec((tm, d), lambda bi, i: (bi * nl + i, 0)),
                   tok4, tok4,
                   pl.BlockSpec((SUBLANES, LANES), lambda bi, i: (0, 0))],
        out_shape=[jax.ShapeDtypeStruct((b, l, d), F32),
                   jax.ShapeDtypeStruct((t, d), F32),
                   jax.ShapeDtypeStruct((t, TOP_K), I32),
                   jax.ShapeDtypeStruct((t, TOP_K), F32),
                   jax.ShapeDtypeStruct((SUBLANES, LANES), F32)],
        compiler_params=_params(("arbitrary", "arbitrary"), 2 * (d * d * 2 + d * LANES * 2 + tm * d * 14)),
        name="out_proj_router",
    )(y_pool, y_ssm, x, g1, sc2, sh2, w_out_bf, ln_g, ln_b, wr_bf, br_pad)


def _plan_kernel(idx_ref, start_ref, pos_ref, carry):
    i = pl.program_id(0)

    @pl.when(i == 0)
    def _():
        carry[...] = jnp.zeros_like(carry)

    idx = idx_ref[...]
    tp = idx.shape[0]
    lane = lax.broadcasted_iota(I32, (tp, LANES), 1)
    hits = [lane == idx[:, k:k + 1] for k in range(TOP_K)]
    onehot = jnp.zeros((tp, LANES), F32)
    for h in hits:
        onehot = onehot + jnp.where(h, 1.0, 0.0)
    below = jnp.where(lax.broadcasted_iota(I32, (tp, tp), 0) > lax.broadcasted_iota(I32, (tp, tp), 1),
                      1.0, 0.0).astype(BF16)
    excl = jnp.dot(below, onehot.astype(BF16), preferred_element_type=F32)
    slot = start_ref[0:1, :] + carry[0:1, :] + excl
    for k in range(TOP_K):
        pos_ref[:, k:k + 1] = jnp.sum(jnp.where(hits[k], slot, 0.0), axis=1, keepdims=True).astype(I32)
    carry[0:1, :] += jnp.sum(onehot, axis=0, keepdims=True)


def _plan(idx_all, start_pad):
    t = idx_all.shape[0]
    tp = next(c for c in (512, 384, 256, 128, 64, 32, 16, 8) if t % c == 0)
    return pl.pallas_call(
        _plan_kernel,
        grid=(t // tp,),
        in_specs=[pl.BlockSpec((tp, TOP_K), lambda i: (i, 0)),
                  pl.BlockSpec((SUBLANES, LANES), lambda i: (0, 0))],
        out_specs=pl.BlockSpec((tp, TOP_K), lambda i: (i, 0)),
        out_shape=jax.ShapeDtypeStruct((t, TOP_K), I32),
        scratch_shapes=[pltpu.VMEM((SUBLANES, LANES), F32)],
        compiler_params=_params(("arbitrary",), 4 * tp * LANES * 4),
        name="moe_plan",
    )(idx_all, start_pad)


def _dispatch_kernel(fill_ref, posp_ref, poss_ref, hp_ref, hs_ref, xs_ref, zbuf, sem, zsem, *, n_p_steps, n_blk):
    i = pl.program_id(0)

    def fill_copy(b):
        return pltpu.make_async_copy(zbuf, xs_ref.at[pl.ds(pl.multiple_of(b * ROW_BLK, ROW_BLK), ROW_BLK)], zsem)

    @pl.when(i == 0)
    def _():
        zbuf[...] = jnp.zeros_like(zbuf)

        def start(b, carry):
            @pl.when(fill_ref[b] != 0)
            def _():
                fill_copy(b).start()
            return carry

        def wait(b, carry):
            @pl.when(fill_ref[b] != 0)
            def _():
                fill_copy(b).wait()
            return carry

        lax.fori_loop(0, n_blk, start, 0)
        lax.fori_loop(0, n_blk, wait, 0)

    def scatter(src_ref, pos_ref):
        n_grp = src_ref.shape[0]

        def body(g, carry):
            for tt in range(SUBLANES):
                for k in range(TOP_K):
                    p = pos_ref[0, 0, (g * SUBLANES + tt) * TOP_K + k]
                    pltpu.make_async_copy(src_ref.at[g, pl.ds(tt, 1)], xs_ref.at[pl.ds(p, 1)],
                                          sem).start(priority=k % 2)
            return carry

        lax.fori_loop(0, n_grp, body, 0)
        rows = xs_ref.at[pl.ds(0, n_grp * SUBLANES)]
        for k in range(TOP_K):
            pltpu.make_async_copy(rows, rows, sem).wait()

    @pl.when(i < n_p_steps)
    def _():
        scatter(hp_ref, posp_ref)

    @pl.when(i >= n_p_steps)
    def _():
        scatter(hs_ref, poss_ref)


def _dispatch(pos_p, pos_s, h_prompt, h_sample, fill_flags, n_rows):
    tp, w = h_prompt.shape
    ts = h_sample.shape[0]
    tdp, tds = math.gcd(tp, 512), math.gcd(ts, 512)
    n_p_steps, n_s_steps = tp // tdp, ts // tds
    n_blk = fill_flags.shape[0]
    grid_spec = pltpu.PrefetchScalarGridSpec(
        num_scalar_prefetch=1,
        grid=(n_p_steps + n_s_steps,),
        in_specs=[pl.BlockSpec((1, 1, tdp * TOP_K), lambda i, f: (jnp.minimum(i, n_p_steps - 1), 0, 0),
                               memory_space=pltpu.SMEM),
                  pl.BlockSpec((1, 1, tds * TOP_K), lambda i, f: (jnp.maximum(i - n_p_steps, 0), 0, 0),
                               memory_space=pltpu.SMEM),
                  pl.BlockSpec((tdp // SUBLANES, SUBLANES, w), lambda i, f: (jnp.minimum(i, n_p_steps - 1), 0, 0)),
                  pl.BlockSpec((tds // SUBLANES, SUBLANES, w), lambda i, f: (jnp.maximum(i - n_p_steps, 0), 0, 0))],
        out_specs=pl.BlockSpec(memory_space=pl.ANY),
        scratch_shapes=[pltpu.VMEM((ROW_BLK, w), F32),
                        pltpu.SemaphoreType.DMA(()), pltpu.SemaphoreType.DMA(())],
    )
    need = 2 * (tdp + tds) * w * 4 + ROW_BLK * w * 4
    return pl.pallas_call(
        functools.partial(_dispatch_kernel, n_p_steps=n_p_steps, n_blk=n_blk),
        grid_spec=grid_spec,
        out_shape=jax.ShapeDtypeStruct((n_rows, w), F32),
        compiler_params=_params(("arbitrary",), need),
        name="moe_dispatch",
    )(fill_flags, pos_p.reshape(n_p_steps, 1, tdp * TOP_K), pos_s.reshape(n_s_steps, 1, tds * TOP_K),
      h_prompt.reshape(tp // SUBLANES, SUBLANES, w), h_sample.reshape(ts // SUBLANES, SUBLANES, w))


def _moe_kernel(ie_ref, ir_ref, inb_ref, meta_ref, xs_ref, wg_ref, wu_ref, bgu_ref, wd_ref, bd_ref,
                ys_ref, xbuf, act, ystage, zbuf, yflag, xsem, ysem, zsem, *, nc, n_blk):
    i = pl.program_id(0)
    c = pl.program_id(1)
    n_items = meta_ref[0]
    used_blk = meta_ref[1]
    valid = i < n_items
    nb = inb_ref[i]
    row0 = pl.multiple_of(ir_ref[i], ROW_BLK)
    n_quad = lax.shift_right_logical(nb, 3)
    has_pair = (nb & 4) != 0
    has_rb = (nb & 2) != 0
    has_blk = (nb & 1) != 0
    base_pair = pl.multiple_of(n_quad * 4 * MOE_RB, ROW_BLK)
    base_rb = pl.multiple_of(base_pair + jnp.where(has_pair, 2 * MOE_RB, 0), ROW_BLK)
    base_blk = pl.multiple_of(base_rb + jnp.where(has_rb, MOE_RB, 0), ROW_BLK)
    first = jnp.logical_and(i == 0, c == 0)
    last = jnp.logical_and(i == pl.num_programs(0) - 1, c == pl.num_programs(1) - 1)

    def x_copy(it, j):
        src0 = pl.multiple_of(ir_ref[it] + j * ROW_BLK, ROW_BLK)
        return pltpu.make_async_copy(xs_ref.at[pl.ds(src0, ROW_BLK)],
                                     xbuf.at[pl.ds(pl.multiple_of(j * ROW_BLK, ROW_BLK), ROW_BLK)], xsem)

    def x_start(it):
        def start(j, carry):
            x_copy(it, j).start()
            return carry
        lax.fori_loop(0, inb_ref[it], start, 0)

    def x_wait(it):
        def wait(j, carry):
            x_copy(it, j).wait()
            return carry
        lax.fori_loop(0, inb_ref[it], wait, 0)

    def tail_copy(b, col):
        return pltpu.make_async_copy(
            zbuf, ys_ref.at[pl.ds(pl.multiple_of(b * ROW_BLK, ROW_BLK), ROW_BLK), pl.ds(col * MOE_CW, MOE_CW)], zsem)

    n_col = ys_ref.shape[1] // MOE_CW

    @pl.when(first)
    def _():
        x_start(0)
        zbuf[...] = jnp.zeros_like(zbuf)
        for slot in range(YS_SLOTS):
            yflag[slot] = 0

        def start(b, carry):
            for col in range(n_col):
                tail_copy(b, col).start()
            return carry
        lax.fori_loop(used_blk, n_blk, start, 0)

    @pl.when(jnp.logical_and(valid, c == 0))
    def _():
        x_wait(i)

    @pl.when(jnp.logical_and(c == nc, i + 1 < n_items))
    def _():
        x_start(i + 1)

    @pl.when(jnp.logical_and(valid, c < nc))
    def _():
        def piece(r0, rows):
            w, bias = [], []
            bg, bu = bgu_ref[pl.ds(c, 1), :], bgu_ref[pl.ds(nc + c, 1), :]
            for s in range(MOE_CW // LANES):
                cols = slice(s * LANES, (s + 1) * LANES)
                w += [wg_ref[:, cols].astype(BF16), wu_ref[:, cols].astype(BF16)]
                bias += [bg[:, cols], bu[:, cols]]
            x = xbuf[pl.ds(r0, rows), :].astype(BF16)
            gu = (jnp.dot(x, jnp.concatenate(w, axis=1), preferred_element_type=F32)
                  + jnp.concatenate(bias, axis=1))
            for s in range(MOE_CW // LANES):
                gate = jnp.minimum(gu[:, 2 * s * LANES:(2 * s + 1) * LANES], SWIGLU_LIMIT)
                up = jnp.clip(gu[:, (2 * s + 1) * LANES:(2 * s + 2) * LANES], -SWIGLU_LIMIT, SWIGLU_LIMIT)
                a = (up + 1.0) * (gate * jax.nn.sigmoid(SWIGLU_ALPHA * gate))
                act[c, pl.ds(r0, rows), s * LANES:(s + 1) * LANES] = a.astype(BF16)

        def quad(q, carry):
            r0 = pl.multiple_of(q * 4 * MOE_RB, ROW_BLK)
            for j in range(4):
                piece(pl.multiple_of(r0 + j * MOE_RB, ROW_BLK), MOE_RB)
            return carry

        lax.fori_loop(0, n_quad, quad, 0)

        @pl.when(has_pair)
        def _():
            for j in range(2):
                piece(pl.multiple_of(base_pair + j * MOE_RB, ROW_BLK), MOE_RB)

        @pl.when(has_rb)
        def _():
            piece(base_rb, MOE_RB)

        @pl.when(has_blk)
        def _():
            piece(base_blk, ROW_BLK)

    slot_rows = [MOE_RB] * (YS_SLOTS - 1) + [ROW_BLK]

    def y_copy(slot, r0, col0):
        dst_row = pl.multiple_of(row0 + r0, ROW_BLK)
        return pltpu.make_async_copy(ystage.at[slot, pl.ds(0, slot_rows[slot])],
                                     ys_ref.at[pl.ds(dst_row, slot_rows[slot]), pl.ds(col0, MOE_OW)],
                                     ysem.at[slot])

    def y_drain_all():
        for slot in range(YS_SLOTS):
            @pl.when(yflag[slot] != 0)
            def _(slot=slot):
                y_copy(slot, 0, 0).wait()
                yflag[slot] = 0

    @pl.when(jnp.logical_and(valid, c >= nc))
    def _():
        col0 = pl.multiple_of((c - nc) * MOE_OW, MOE_OW)
        y_drain_all()

        def compute(r0, rows):
            a = jnp.concatenate([act[cc, pl.ds(r0, rows), :] for cc in range(nc)], axis=1)
            return jnp.dot(a, wd_ref[...].astype(BF16), preferred_element_type=F32) + bd_ref[pl.ds(c - nc, 1), :]

        def group(slots, base):
            for j, slot in enumerate(slots):
                ystage[slot, 0:slot_rows[slot], :] = compute(pl.multiple_of(base + j * MOE_RB, ROW_BLK),
                                                             slot_rows[slot])
            for j, slot in enumerate(slots):
                y_copy(slot, base + j * MOE_RB, col0).start()

        def quad(q, carry):
            @pl.when(q > 0)
            def _():
                for slot in (0, 1, 2, 3):
                    y_copy(slot, 0, 0).wait()

            group((0, 1, 2, 3), pl.multiple_of(q * 4 * MOE_RB, ROW_BLK))
            return carry

        lax.fori_loop(0, n_quad, quad, 0)

        @pl.when(n_quad > 0)
        def _():
            for slot in (0, 1, 2, 3):
                yflag[slot] = 1

        @pl.when(has_pair)
        def _():
            group((4, 5), base_pair)
            yflag[4] = 1
            yflag[5] = 1

        @pl.when(has_rb)
        def _():
            group((6,), base_rb)
            yflag[6] = 1

        @pl.when(has_blk)
        def _():
            group((7,), base_blk)
            yflag[7] = 1

    @pl.when(last)
    def _():
        def wait(b, carry):
            for col in range(n_col):
                tail_copy(b, col).wait()
            return carry
        lax.fori_loop(used_blk, n_blk, wait, 0)
        y_drain_all()


def _moe(xs, item_e, item_row0, item_nblk, meta, w_gate_up, b_gate_up, w_down, b_down):
    n_rows, d = xs.shape
    n_exp, _, ff2 = w_gate_up.shape
    ff = ff2 // 2
    assert w_gate_up.shape[1] == d and w_down.shape[1:] == (ff, d) and ff == d
    nc = ff // MOE_CW
    nc2 = d // MOE_OW
    ni = item_e.shape[0]
    rbuf = MOE_MAX_BLK * ROW_BLK

    def item(i, n):
        return jnp.minimum(i, n[0] - 1)

    def c1(i, c, n):
        return jnp.where(i < n[0], jnp.minimum(c, nc - 1), nc - 1)

    def c2(i, c, n):
        return jnp.where(i < n[0], jnp.maximum(c - nc, 0), nc2 - 1)

    bgu3 = b_gate_up.reshape(n_exp, 2 * nc, MOE_CW)
    bd3 = b_down.reshape(n_exp, nc2, MOE_OW)
    in_specs = [
        pl.BlockSpec(memory_space=pl.ANY),
        pl.BlockSpec((None, d, MOE_CW), lambda i, c, e, r, b, n: (e[item(i, n)], 0, c1(i, c, n))),
        pl.BlockSpec((None, d, MOE_CW), lambda i, c, e, r, b, n: (e[item(i, n)], 0, nc + c1(i, c, n))),
        pl.BlockSpec((None, 2 * nc, MOE_CW), lambda i, c, e, r, b, n: (e[item(i, n)], 0, 0)),
        pl.BlockSpec((None, ff, MOE_OW), lambda i, c, e, r, b, n: (e[item(i, n)], 0, c2(i, c, n))),
        pl.BlockSpec((None, nc2, MOE_OW), lambda i, c, e, r, b, n: (e[item(i, n)], 0, 0)),
    ]
    grid_spec = pltpu.PrefetchScalarGridSpec(
        num_scalar_prefetch=4,
        grid=(ni, nc + nc2),
        in_specs=in_specs,
        out_specs=pl.BlockSpec(memory_space=pl.ANY),
        scratch_shapes=[pltpu.VMEM((rbuf, d), F32),
                        pltpu.VMEM((nc, rbuf, MOE_CW), BF16),
                        pltpu.VMEM((YS_SLOTS, MOE_RB, MOE_OW), F32),
                        pltpu.VMEM((ROW_BLK, MOE_CW), F32),
                        pltpu.SMEM((YS_SLOTS,), I32),
                        pltpu.SemaphoreType.DMA(()),
                        pltpu.SemaphoreType.DMA((YS_SLOTS,)),
                        pltpu.SemaphoreType.DMA(())],
    )
    need = (rbuf * d * 4 + rbuf * ff * 2 + YS_SLOTS * MOE_RB * MOE_OW * 4 + ROW_BLK * MOE_CW * 4
            + 2 * 2 * d * MOE_CW * 4 + 2 * ff * MOE_OW * 4)
    return pl.pallas_call(
        functools.partial(_moe_kernel, nc=nc, n_blk=n_rows // ROW_BLK),
        grid_spec=grid_spec,
        out_shape=jax.ShapeDtypeStruct((n_rows, d), F32),
        compiler_params=_params(("arbitrary", "arbitrary"), need),
        name="moe_experts",
    )(item_e, item_row0, item_nblk, meta, xs, w_gate_up, w_gate_up, bgu3, w_down, bd3)


def _combine_kernel(pos_ref, posn_ref, ys_ref, gate_ref, x1_ref, g2_ref, lng_ref, lnb_ref, o_ref, gbuf, sem, *, tc):
    i = pl.program_id(0)
    slot = i % 2

    def issue(p_ref, s):
        def body(g, carry):
            for tt in range(SUBLANES):
                for k in range(TOP_K):
                    p = p_ref[0, 0, (g * SUBLANES + tt) * TOP_K + k]
                    pltpu.make_async_copy(ys_ref.at[pl.ds(p, 1)], gbuf.at[s, k, g, pl.ds(tt, 1)],
                                          sem.at[s]).start(priority=k % 2)
            return carry
        lax.fori_loop(0, tc // SUBLANES, body, 0)

    @pl.when(i == 0)
    def _():
        issue(pos_ref, 0)

    @pl.when(i + 1 < pl.num_programs(0))
    def _():
        issue(posn_ref, 1 - slot)

    rows = ys_ref.at[pl.ds(0, tc)]
    for k in range(TOP_K):
        pltpu.make_async_copy(rows, rows, sem.at[slot]).wait()
    gt = gate_ref[...]
    d = x1_ref.shape[-1]
    ffn = gt[:, 0:1] * gbuf[slot, 0].reshape(tc, d)
    for k in range(1, TOP_K):
        ffn = ffn + gt[:, k:k + 1] * gbuf[slot, k].reshape(tc, d)
    o_ref[...] = _layer_norm(DN_ALPHA * x1_ref[...] + g2_ref[...] * ffn, lng_ref[...], lnb_ref[...])


def _combine(pos, ys, gates, x1, g2, ln_g, ln_b):
    b, l, d = x1.shape
    t = b * l
    tc = min(l, 256)
    per_b = l // tc
    n_steps = t // tc
    pos3 = pos.reshape(n_steps, 1, tc * TOP_K)
    vec = pl.BlockSpec((1, d), lambda i: (0, 0))
    out = pl.pallas_call(
        functools.partial(_combine_kernel, tc=tc),
        grid=(n_steps,),
        in_specs=[pl.BlockSpec((1, 1, tc * TOP_K), lambda i: (i, 0, 0), memory_space=pltpu.SMEM),
                  pl.BlockSpec((1, 1, tc * TOP_K), lambda i: (jnp.minimum(i + 1, n_steps - 1), 0, 0),
                               memory_space=pltpu.SMEM),
                  pl.BlockSpec(memory_space=pl.ANY),
                  pl.BlockSpec((tc, TOP_K), lambda i: (i, 0)),
                  pl.BlockSpec((tc, d), lambda i: (i, 0)),
                  pl.BlockSpec((None, 1, d), lambda i: (i // per_b, 0, 0)),
                  vec, vec],
        out_specs=pl.BlockSpec((tc, d), lambda i: (i, 0)),
        out_shape=jax.ShapeDtypeStruct((t, d), F32),
        scratch_shapes=[pltpu.VMEM((2, TOP_K, tc // SUBLANES, SUBLANES, d), F32), pltpu.SemaphoreType.DMA((2,))],
        compiler_params=_params(("arbitrary",), (2 * TOP_K + 2 * 2) * tc * d * 4),
        name="moe_combine",
    )(pos3, pos3, ys, gates, x1.reshape(t, d), g2, ln_g, ln_b)
    return out.reshape(b, l, d)


def _moe_items(counts, n_blk_total):
    n_exp = counts.shape[0]
    nblk = (counts + ROW_BLK - 1) // ROW_BLK
    blk_end = jnp.cumsum(nblk)
    blk0 = blk_end - nblk
    n_it = (nblk + MOE_MAX_BLK - 1) // MOE_MAX_BLK
    it_end = jnp.cumsum(n_it)
    it0 = it_end - n_it
    ni = n_exp + n_blk_total // MOE_MAX_BLK
    j = jnp.arange(ni, dtype=I32)
    e = jnp.minimum(jnp.sum(j[:, None] >= it_end[None, :], axis=1), n_exp - 1).astype(I32)
    local = j - it0[e]
    item_nblk = jnp.clip(nblk[e] - local * MOE_MAX_BLK, 0, MOE_MAX_BLK).astype(I32)
    item_row0 = ((blk0[e] + local * MOE_MAX_BLK) * ROW_BLK).astype(I32)
    n_items = it_end[-1].astype(I32)
    item_nblk = jnp.where(j < n_items, item_nblk, 0)
    item_row0 = jnp.where(j < n_items, item_row0, 0)
    row_start = (blk0 * ROW_BLK).astype(F32)
    used_blk = blk_end[-1].astype(I32)
    meta = jnp.stack([n_items, used_blk])
    blk = jnp.arange(n_blk_total, dtype=I32)
    partial = jnp.any((blk[:, None] == (blk_end - 1)[None, :]) & ((counts % ROW_BLK) != 0)[None, :], axis=1)
    fill = jnp.logical_or(partial, blk >= used_blk).astype(I32)
    return e, item_row0, item_nblk, meta, row_start, fill


def _mixer_stream(x, mod, hist, h0, pos0, w):
    b, l, d = x.shape
    sh1, sc1, g1, sh2, sc2, g2 = [m.reshape(b, 1, d) for m in jnp.split(mod, 6, axis=-1)]
    u_pool, u_ssm = _inproj(x, sc1, sh1, w["w_in"])
    hist16 = jnp.concatenate([jnp.zeros((b, HIST_ROWS - hist.shape[1], hist.shape[2]), F32), hist.astype(F32)], axis=1)
    y_pool, new_hist = _pool(u_pool, hist16, w["w_pool"], w["pool_scale"], pos0)
    y_ssm, h_new = _ssm(u_ssm, w["ssm"], h0)
    x1, hp, idx, gates, cnt = _outproj(y_pool, y_ssm, x, g1, sc2, sh2, w["w_out"], w["ln1_g"], w["ln1_b"],
                                       w["w_router"], w["b_router"], w["n_experts"])
    s_re, s_im = _slabs_to_state(h_new, SSM_STATE)
    return dict(x1=x1, hp=hp, idx=idx, gates=gates, cnt=cnt, g2=g2,
                hist=new_hist[:, 1:, :], s_re=s_re, s_im=s_im)


def kernel(x_prompt, x_sample, cache_pool, state_ssm_re, state_ssm_im, c_prompt, c_sample, w_ada, b_ada, w_in, w_pool, pool_scale, lambda_re, lambda_im, log_dt, ssm_b_re, ssm_b_im, ssm_c_re, ssm_c_im, d_skip, w_glu, b_glu, w_out, ln1_g, ln1_b, w_router, b_router, w_gate_up, b_gate_up, w_down, b_down, ln2_g, ln2_b):
    assert w_ada.shape[0] == DEPTH
    bp, lp, d = x_prompt.shape
    bs, ls, _ = x_sample.shape
    assert bp == SUBLANES and bs == SUBLANES, "the S5 kernel puts the 8 streams on sublanes"
    n_exp = w_router.shape[-1]
    dp = w_pool.shape[1] * w_pool.shape[2]

    w = dict(
        w_in=w_in[0].astype(BF16),
        w_pool=w_pool[0].astype(BF16),
        pool_scale=pool_scale[0].astype(F32).reshape(1, dp),
        ssm=_ssm_params(lambda_re[0], lambda_im[0], log_dt[0], ssm_b_re[0], ssm_b_im[0], ssm_c_re[0], ssm_c_im[0],
                        d_skip[0], w_glu[0], b_glu[0], SUBLANES),
        w_out=w_out[0].astype(BF16),
        ln1_g=ln1_g[0].reshape(1, d), ln1_b=ln1_b[0].reshape(1, d),
        w_router=jnp.pad(w_router[0], ((0, 0), (0, LANES - n_exp))).astype(BF16),
        b_router=jnp.pad(b_router[0].astype(F32), (0, LANES - n_exp)).reshape(1, LANES),
        n_experts=n_exp,
    )

    mod = _ada(jnp.concatenate([c_prompt, c_sample], axis=0), w_ada[0], b_ada[0].reshape(1, -1))
    n_slab = lambda_re.shape[1]
    p = _mixer_stream(x_prompt, mod[:bp], jnp.zeros((bp, HIST_ROWS - 1, dp), F32),
                      jnp.zeros((n_slab, bp, LANES), F32), 0, w)
    s = _mixer_stream(x_sample, mod[bp:], cache_pool[0], _state_to_slabs(state_ssm_re[0], state_ssm_im[0]),
                      PAST_LEN, w)

    tp, ts = bp * lp, bs * ls
    n_assign = (tp + ts) * TOP_K
    n_blk_total = -(-n_assign // ROW_BLK) + n_exp
    n_rows = n_blk_total * ROW_BLK
    counts = (p["cnt"][0, :n_exp] + s["cnt"][0, :n_exp]).astype(I32)
    item_e, item_row0, item_nblk, meta, row_start, fill = _moe_items(counts, n_blk_total)
    start_pad = jnp.zeros((SUBLANES, LANES), F32).at[0, :n_exp].set(row_start)
    pos = _plan(jnp.concatenate([p["idx"], s["idx"]], axis=0), start_pad)
    pos_p, pos_s = pos[:tp], pos[tp:]
    xs = _dispatch(pos_p, pos_s, p["hp"], s["hp"], fill, n_rows)
    ys = _moe(xs, item_e, item_row0, item_nblk, meta, w_gate_up[0], b_gate_up[0], w_down[0], b_down[0])
    g2l, b2l = ln2_g[0].reshape(1, d), ln2_b[0].reshape(1, d)
    y_p = _combine(pos_p, ys, p["gates"], p["x1"], p["g2"], g2l, b2l)
    y_s = _combine(pos_s, ys, s["gates"], s["x1"], s["g2"], g2l, b2l)

    return (y_p, y_s, p["hist"][None], p["s_re"][None], p["s_im"][None],
            s["hist"][None], s["s_re"][None], s["s_im"][None])
```
